```python
import jax, jax.numpy as jnp
from jax import lax
import numpy as np

D_MODEL = 1024
BATCH = 16
SEQ = 4096
DEPTH = 1

CONV_WIDTH = 512
CONV_K = 3
N_HEADS = 8
HEAD_DIM = 64
ATTN_WIDTH = N_HEADS * HEAD_DIM
IDX_HEADS = 8
IDX_DIM = 64
TOPK_ATTN = 256
Q_BLOCK = 128
PEER_HEADS = 8
PEER_N_KEYS = 128
PEER_N_EXPERTS = PEER_N_KEYS * PEER_N_KEYS
PEER_KEY_DIM = 128
PEER_HALF = PEER_KEY_DIM // 2
PEER_TOPK = 16
PEER_CHUNK = 128
N_MOD = 6
EPS = 1e-6

kernel_name = 'hybrid_conv_dsa_peer_block'


def _in_proj_sizes():
    return [CONV_WIDTH, CONV_WIDTH, CONV_WIDTH,
            ATTN_WIDTH, ATTN_WIDTH, ATTN_WIDTH,
            IDX_HEADS * IDX_DIM, IDX_DIM, IDX_HEADS,
            D_MODEL, D_MODEL]


def rms_norm(x, w):
    xf = x.astype(jnp.float32)
    y = xf * lax.rsqrt(jnp.mean(xf * xf, axis=-1, keepdims=True) + EPS)
    return (y * w.astype(jnp.float32)).astype(x.dtype)


def causal_dwconv(u, w):
    s = u.shape[1]
    pad = jnp.pad(u, ((0, 0), (CONV_K - 1, 0), (0, 0)))
    out = w[0] * pad[:, 0:s]
    for j in range(1, CONV_K):
        out = out + w[j] * pad[:, j:j + s]
    return out


def dsa_attention(q, k, v, q_idx, k_idx, w_idx):
    b, s = q.shape[0], q.shape[1]
    n_sel = min(TOPK_ATTN, s // 4)
    n_blk = s // Q_BLOCK
    key_pos = jnp.arange(s)
    k_idx_f = k_idx.astype(jnp.float32)
    gather = jax.vmap(lambda t, i: t[i])

    def to_blocks(t):
        return jnp.swapaxes(t.reshape(b, n_blk, Q_BLOCK, *t.shape[2:]), 0, 1)

    def block_fn(args):
        blk, qb, qib, wb = args
        q_pos = blk * Q_BLOCK + jnp.arange(Q_BLOCK)
        logits = jnp.einsum('bqhd,bsd->bqhs', qib.astype(jnp.float32), k_idx_f)
        score = jnp.einsum('bqhs,bqh->bqs', jax.nn.relu(logits), wb.astype(jnp.float32))
        causal = key_pos[None, :] <= q_pos[:, None]
        score = jnp.where(causal[None], score, -jnp.inf)
        _, sel = lax.top_k(score, n_sel)
        ks = gather(k, sel)
        vs = gather(v, sel)
        att = jnp.einsum('bqhd,bqkhd->bqhk', qb, ks).astype(jnp.float32) * (HEAD_DIM ** -0.5)
        valid = sel <= q_pos[None, :, None]
        att = jnp.where(valid[:, :, None, :], att, -jnp.inf)
        p = jax.nn.softmax(att, axis=-1).astype(vs.dtype)
        return jnp.einsum('bqhk,bqkhd->bqhd', p, vs)

    out = lax.map(block_fn, (jnp.arange(n_blk), to_blocks(q), to_blocks(q_idx), to_blocks(w_idx)))
    return jnp.swapaxes(out, 0, 1).reshape(b, s, N_HEADS * HEAD_DIM)


def peer_ffn(h, w_q, sub_keys, u, v):
    b, s, d = h.shape
    tok = h.reshape(-1, PEER_CHUNK, d)

    def chunk_fn(xc):
        q = (xc @ w_q).reshape(-1, PEER_HEADS, 2, PEER_HALF)
        s1 = jnp.einsum('chd,hnd->chn', q[:, :, 0], sub_keys[:, 0]).astype(jnp.float32)
        s2 = jnp.einsum('chd,hnd->chn', q[:, :, 1], sub_keys[:, 1]).astype(jnp.float32)
        v1, i1 = lax.top_k(s1, PEER_TOPK)
        v2, i2 = lax.top_k(s2, PEER_TOPK)
        cand = (v1[..., :, None] + v2[..., None, :]).reshape(v1.shape[0], PEER_HEADS, PEER_TOPK * PEER_TOPK)
        sc, ci = lax.top_k(cand, PEER_TOPK)
        e1 = jnp.take_along_axis(i1, ci // PEER_TOPK, axis=-1)
        e2 = jnp.take_along_axis(i2, ci % PEER_TOPK, axis=-1)
        expert = e1 * PEER_N_KEYS + e2
        g = jax.nn.softmax(sc, axis=-1)
        ue = u[expert]
        ve = v[expert]
        act = jax.nn.gelu(jnp.einsum('chkd,cd->chk', ue, xc).astype(jnp.float32), approximate=False)
        return jnp.einsum('chk,chkd->cd', (g * act).astype(ve.dtype), ve)

    return lax.map(chunk_fn, tok).reshape(b, s, d)


def setup_inputs(seed: int = 0) -> dict:
    key = jax.random.key(seed)
    ks = jax.random.split(key, 20)
    f32 = jnp.float32
    d = D_MODEL
    n_in = sum(_in_proj_sizes())
    nrm = lambda k, shape, scale: jax.random.normal(k, shape, f32) * scale
    return {
        'x': nrm(ks[0], (BATCH, SEQ, d), 1.0),
        'c': nrm(ks[1], (BATCH, d), 1.0),
        'w_ada': nrm(ks[2], (DEPTH, d, N_MOD * d), 0.5 * d ** -0.5),
        'b_ada': nrm(ks[3], (DEPTH, N_MOD * d), 0.01),
        'norm1_w': 1.0 + nrm(ks[4], (DEPTH, d), 0.02),
        'w_in': nrm(ks[5], (DEPTH, d, n_in), d ** -0.5),
        'conv_w': nrm(ks[6], (DEPTH, CONV_K, CONV_WIDTH), CONV_K ** -0.5),
        'w_conv_out': nrm(ks[7], (DEPTH, CONV_WIDTH, d), CONV_WIDTH ** -0.5),
        'q_norm_w': 1.0 + nrm(ks[8], (DEPTH, HEAD_DIM), 0.02),
        'k_norm_w': 1.0 + nrm(ks[9], (DEPTH, HEAD_DIM), 0.02),
        'w_attn_out': nrm(ks[10], (DEPTH, ATTN_WIDTH, d), ATTN_WIDTH ** -0.5),
        'w_o': nrm(ks[11], (DEPTH, d, d), d ** -0.5),
        'norm2_w': 1.0 + nrm(ks[12], (DEPTH, d), 0.02),
        'w_peer_q': nrm(ks[13], (DEPTH, d, PEER_HEADS * PEER_KEY_DIM), d ** -0.5),
        'peer_sub_keys': nrm(ks[14], (DEPTH, PEER_HEADS, 2, PEER_N_KEYS, PEER_HALF), PEER_HALF ** -0.5),
        'peer_u': nrm(ks[15], (DEPTH, PEER_N_EXPERTS, d), d ** -0.5),
        'peer_v': nrm(ks[16], (DEPTH, PEER_N_EXPERTS, d), 1.0),
    }


def reference(x, c, w_ada, b_ada, norm1_w, w_in, conv_w, w_conv_out, q_norm_w, k_norm_w,
              w_attn_out, w_o, norm2_w, w_peer_q, peer_sub_keys, peer_u, peer_v):
    b, s, _ = x.shape
    offsets = [int(o) for o in np.cumsum(_in_proj_sizes())[:-1]]
    for layer in range(DEPTH):
        mod = jax.nn.silu(c) @ w_ada[layer] + b_ada[layer]
        sh1, sc1, g1, sh2, sc2, g2 = jnp.split(mod[:, None, :], N_MOD, axis=-1)

        h = rms_norm(x, norm1_w[layer]) * (1.0 + sc1) + sh1
        proj = h @ w_in[layer]
        (cb, cc, cx, q, k, v, qi, ki, wi, gate_conv, gate_attn) = jnp.split(proj, offsets, axis=-1)

        y_conv = (cb * causal_dwconv(cc * cx, conv_w[layer])) @ w_conv_out[layer]

        q = rms_norm(q.reshape(b, s, N_HEADS, HEAD_DIM), q_norm_w[layer])
        k = rms_norm(k.reshape(b, s, N_HEADS, HEAD_DIM), k_norm_w[layer])
        v = v.reshape(b, s, N_HEADS, HEAD_DIM)
        qi = qi.reshape(b, s, IDX_HEADS, IDX_DIM)
        y_attn = dsa_attention(q, k, v, qi, ki, wi) @ w_attn_out[layer]

        mix = jax.nn.sigmoid(gate_conv) * y_conv + jax.nn.sigmoid(gate_attn) * y_attn
        x = x + g1 * (mix @ w_o[layer])

        h2 = rms_norm(x, norm2_w[layer]) * (1.0 + sc2) + sh2
        x = x + g2 * peer_ffn(h2, w_peer_q[layer], peer_sub_keys[layer], peer_u[layer], peer_v[layer])
    return x
```

```python
import functools

import jax
import jax.numpy as jnp
from jax import lax
from jax.experimental import pallas as pl
from jax.experimental.pallas import tpu as pltpu

F32 = jnp.float32
BF16 = jnp.bfloat16
I32 = jnp.int32

N_HEADS = 8
HEAD_DIM = 64
IDX_HEADS = 8
IDX_DIM = 64
TOPK_ATTN = 256
PEER_HEADS = 8
PEER_N_KEYS = 128
PEER_HALF = 64
PEER_TOPK = 16
CONV_K = 3
N_MOD = 6
EPS = 1e-6

LANES = 128
INT_MIN = -(2 ** 31)
NEG_BIG = -1e30
VMEM_LIMIT = 56 * 1024 * 1024

BT_PROJ = 256
BQ = 256
BK = 256
TT_PEER = 512
EC_PEER = 1024


def _split_bf16(a):
    hi = a.astype(BF16)
    lo = (a - hi.astype(F32)).astype(BF16)
    return hi, lo


def _nt_dot(a, b):
    return lax.dot_general(a, b, (((1,), (1,)), ((), ())), preferred_element_type=F32)


def _mod_kernel(c_ref, w_ref, b_ref, o_ref):
    c = c_ref[...]
    s = c * jax.nn.sigmoid(c)
    s_hi, s_lo = _split_bf16(s)
    w_hi, w_lo = _split_bf16(w_ref[...])
    acc = jnp.dot(s_hi, w_hi, preferred_element_type=F32)
    acc += jnp.dot(s_hi, w_lo, preferred_element_type=F32)
    acc += jnp.dot(s_lo, w_hi, preferred_element_type=F32)
    o_ref[...] = acc + b_ref[...]


def _mod_call(c, w_ada, b_ada):
    b, d = c.shape
    n = w_ada.shape[1]
    bn = d
    return pl.pallas_call(
        _mod_kernel,
        out_shape=jax.ShapeDtypeStruct((b, n), F32),
        grid=(n // bn,),
        in_specs=[
            pl.BlockSpec((b, d), lambda j: (0, 0)),
            pl.BlockSpec((d, bn), lambda j: (0, j)),
            pl.BlockSpec((1, bn), lambda j: (0, j)),
        ],
        out_specs=pl.BlockSpec((b, bn), lambda j: (0, j)),
        compiler_params=pltpu.CompilerParams(
            dimension_semantics=("arbitrary",), vmem_limit_bytes=VMEM_LIMIT),
        name="adaln_mod",
    )(c, w_ada, b_ada.reshape(1, n))


def _inproj_kernel(x_ref, xh_ref, mod_ref, n1_ref, wconv_ref, wqkv_ref, wqi_ref, wkw_ref,
                   wg_ref, convw_ref, wco_ref, qn_ref, kn_ref, gmat_ref,
                   gyc_ref, sga_ref, q_ref, k_ref, v_ref, qi_ref, kw_ref, *, tiles_per_seq):
    i = pl.program_id(0)
    cw = convw_ref.shape[1]
    aw = q_ref.shape[1]
    d = x_ref.shape[1]
    bt = x_ref.shape[0]
    sh1 = mod_ref[0, 0:1, :]
    sc1 = mod_ref[0, 1:2, :]

    def norm_mod(xv):
        ms = jnp.mean(xv * xv, axis=-1, keepdims=True)
        y = xv * lax.rsqrt(ms + EPS) * n1_ref[...]
        return (y * (1.0 + sc1) + sh1).astype(BF16)

    h = norm_mod(x_ref[...])
    hh = norm_mod(xh_ref[...])

    pc = jnp.dot(h, wconv_ref[...], preferred_element_type=F32)
    cb, cc, cx = pc[:, :cw], pc[:, cw:2 * cw], pc[:, 2 * cw:]
    u = cc * cx
    ph = jnp.dot(hh, wconv_ref[:, cw:], preferred_element_type=F32)
    uh = ph[:, :cw] * ph[:, cw:]
    uh = jnp.where(i % tiles_per_seq == 0, 0.0, uh)
    row = lax.broadcasted_iota(I32, (bt, cw), 0)
    u1 = jnp.where(row == 0, uh[7:8, :], pltpu.roll(u, 1, 0))
    u2 = pltpu.roll(u, 2, 0)
    u2 = jnp.where(row == 0, uh[6:7, :], jnp.where(row == 1, uh[7:8, :], u2))
    conv = convw_ref[0:1, :] * u2 + convw_ref[1:2, :] * u1 + convw_ref[2:3, :] * u
    yc = jnp.dot((cb * conv).astype(BF16), wco_ref[...], preferred_element_type=F32)

    pg = jnp.dot(h, wg_ref[...], preferred_element_type=F32)
    gyc_ref[...] = jax.nn.sigmoid(pg[:, :d]) * yc
    sga_ref[...] = jax.nn.sigmoid(pg[:, d:])

    pq = jnp.dot(h, wqkv_ref[...], preferred_element_type=F32)
    gmat = gmat_ref[...]

    def head_norm(t, w_row):
        hi, lo = _split_bf16(t * t)
        ss = (jnp.dot(hi, gmat, preferred_element_type=F32)
              + jnp.dot(lo, gmat, preferred_element_type=F32))
        return t * lax.rsqrt(ss * (1.0 / HEAD_DIM) + EPS) * w_row

    qn = head_norm(pq[:, :aw], qn_ref[...])
    q_ref[...] = (qn * (HEAD_DIM ** -0.5)).astype(BF16)
    k_ref[...] = head_norm(pq[:, aw:2 * aw], kn_ref[...]).astype(BF16)
    v_ref[...] = pq[:, 2 * aw:].astype(BF16)

    qi_ref[...] = jnp.dot(h, wqi_ref[...], preferred_element_type=F32).astype(BF16)
    kw_ref[...] = jnp.dot(h, wkw_ref[...], preferred_element_type=F32)


def _inproj_call(xf, mod3, n1, wconv, wqkv, wqi, wkw, wg, convw, wco, qn_t, kn_t, gmat, seq):
    t, d = xf.shape
    bt = BT_PROJ
    tiles_per_seq = seq // bt
    cw = convw.shape[1]
    aw = qn_t.shape[1]
    const = lambda i: (0, 0)
    tile = lambda i: (i, 0)
    halo = lambda i: (jnp.maximum(i * (bt // 8) - 1, 0), 0)
    kern = functools.partial(_inproj_kernel, tiles_per_seq=tiles_per_seq)
    return pl.pallas_call(
        kern,
        out_shape=[
            jax.ShapeDtypeStruct((t, d), F32),
            jax.ShapeDtypeStruct((t, d), F32),
            jax.ShapeDtypeStruct((t, aw), BF16),
            jax.ShapeDtypeStruct((t, aw), BF16),
            jax.ShapeDtypeStruct((t, aw), BF16),
            jax.ShapeDtypeStruct((t, aw), BF16),
            jax.ShapeDtypeStruct((t, LANES), F32),
        ],
        grid=(t // bt,),
        in_specs=[
            pl.BlockSpec((bt, d), tile),
            pl.BlockSpec((8, d), halo),
            pl.BlockSpec((1, N_MOD, d), lambda i: (i // tiles_per_seq, 0, 0)),
            pl.BlockSpec((1, d), const),
            pl.BlockSpec(wconv.shape, const),
            pl.BlockSpec(wqkv.shape, const),
            pl.BlockSpec(wqi.shape, const),
            pl.BlockSpec(wkw.shape, const),
            pl.BlockSpec(wg.shape, const),
            pl.BlockSpec(convw.shape, const),
            pl.BlockSpec(wco.shape, const),
            pl.BlockSpec((1, aw), const),
            pl.BlockSpec((1, aw), const),
            pl.BlockSpec(gmat.shape, const),
        ],
        out_specs=[
            pl.BlockSpec((bt, d), tile),
            pl.BlockSpec((bt, d), tile),
            pl.BlockSpec((bt, aw), tile),
            pl.BlockSpec((bt, aw), tile),
            pl.BlockSpec((bt, aw), tile),
            pl.BlockSpec((bt, aw), tile),
            pl.BlockSpec((bt, LANES), tile),
        ],
        compiler_params=pltpu.CompilerParams(
            dimension_semantics=("arbitrary",), vmem_limit_bytes=VMEM_LIMIT),
        name="inproj",
    )(xf, xf, mod3, n1, wconv, wqkv, wqi, wkw, wg, convw, wco, qn_t, kn_t, gmat)


def _dsa_kernel(q_ref, qi_ref, kwq_ref, k_ref, v_ref, kwk_ref, tri_ref, o_ref,
                keys_ref, bias_ref, wb_ref, m_ref, l_ref, acc_ref, *, n_sel):
    i = pl.program_id(1)
    bq = q_ref.shape[0]
    bk = BK
    n_tiles = (i * bq) // bk + bq // bk
    nslab = bk // LANES

    for h in range(IDX_HEADS):
        wb_ref[h] = jnp.broadcast_to(kwq_ref[:, IDX_DIM + h:IDX_DIM + h + 1], (bq, LANES))

    q_pos = i * bq + lax.broadcasted_iota(I32, (bq, bk), 0)

    def score_tile(j, carry):
        ki = kwk_ref[pl.ds(pl.multiple_of(j * bk, bk), bk), :][:, :IDX_DIM].astype(BF16)
        score = jnp.zeros((bq, bk), F32)
        for h in range(IDX_HEADS):
            logits = _nt_dot(qi_ref[:, h * IDX_DIM:(h + 1) * IDX_DIM], ki)
            w = wb_ref[h]
            score = score + jnp.concatenate([w] * nslab, axis=1) * jnp.maximum(logits, 0.0)
        bits = pltpu.bitcast(score, I32)
        key = jnp.where(bits < 0, bits ^ jnp.int32(0x7FFFFFFF), bits)
        key = jnp.where(bits == jnp.int32(INT_MIN), 0, key)
        key_pos = j * bk + lax.broadcasted_iota(I32, (bq, bk), 1)
        keys_ref[j] = jnp.where(key_pos <= q_pos, key, jnp.int32(INT_MIN))
        return carry

    lax.fori_loop(0, n_tiles, score_tile, 0)

    def count_ge(cand, strict):
        cand_b = jnp.broadcast_to(cand, (bq, LANES))

        def body(j, cnt):
            kt = keys_ref[j]
            for s in range(nslab):
                ks = kt[:, s * LANES:(s + 1) * LANES]
                hit = (ks > cand_b) if strict else (ks >= cand_b)
                cnt = cnt + jnp.where(hit, 1, 0)
            return cnt

        cnt = lax.fori_loop(0, n_tiles, body, jnp.zeros((bq, LANES), I32))
        return jnp.sum(cnt, axis=-1, keepdims=True)

    def bit_step(b, t_u):
        cand_u = t_u | lax.shift_left(jnp.int32(1), 31 - b)
        total = count_ge(cand_u ^ jnp.int32(INT_MIN), False)
        return jnp.where(total >= n_sel, cand_u, t_u)

    t_u = lax.fori_loop(0, 32, bit_step, jnp.zeros((bq, 1), I32))
    thr = t_u ^ jnp.int32(INT_MIN)
    n_gt = count_ge(thr, True)
    need = jnp.where(thr == jnp.int32(INT_MIN), 0, n_sel - n_gt).astype(F32)

    tri = tri_ref[...]

    def bias_tile(j, carry):
        kt = keys_ref[j]
        eq = kt == thr
        eq_f = jnp.where(eq, 1.0, 0.0)
        prefix = jnp.dot(eq_f.astype(BF16), tri, preferred_element_type=F32) + carry
        sel = (kt > thr) | (eq & (prefix <= need))
        bias_ref[j] = jnp.where(sel, 0.0, NEG_BIG)
        return carry + jnp.sum(eq_f, axis=-1, keepdims=True)

    lax.fori_loop(0, n_tiles, bias_tile, jnp.zeros((bq, 1), F32))

    m_ref[...] = jnp.full(m_ref.shape, NEG_BIG, F32)
    l_ref[...] = jnp.zeros(l_ref.shape, F32)
    acc_ref[...] = jnp.zeros(acc_ref.shape, F32)

    def attn_tile(j, carry):
        row0 = pl.multiple_of(j * bk, bk)
        kt = k_ref[pl.ds(row0, bk), :]
        vt = v_ref[pl.ds(row0, bk), :]
        bias = bias_ref[j]
        for h in range(N_HEADS):
            hs = slice(h * HEAD_DIM, (h + 1) * HEAD_DIM)
            s = _nt_dot(q_ref[:, hs], kt[:, hs]) + bias
            slabs = [s[:, c * LANES:(c + 1) * LANES] for c in range(nslab)]
            smax = functools.reduce(jnp.maximum, slabs)
            m_prev = m_ref[h]
            m_new = jnp.maximum(m_prev, jnp.max(smax, axis=-1, keepdims=True))
            alpha = jnp.exp(m_prev - m_new)
            ps = [jnp.exp(sl - m_new) for sl in slabs]
            l_ref[h] = alpha * l_ref[h] + functools.reduce(jnp.add, ps)
            p = jnp.concatenate(ps, axis=1).astype(BF16)
            pv = jnp.dot(p, vt[:, hs], preferred_element_type=F32)
            acc_ref[h] = alpha[:, :HEAD_DIM] * acc_ref[h] + pv
            m_ref[h] = m_new
        return carry

    lax.fori_loop(0, n_tiles, attn_tile, 0)

    outs = []
    for h in range(N_HEADS):
        l_row = jnp.sum(l_ref[h], axis=-1, keepdims=True)
        outs.append(acc_ref[h] / l_row)
    o_ref[...] = jnp.concatenate(outs, axis=1).astype(o_ref.dtype)


def _dsa_call(q, qi, kw, k, v, tri, batch, seq):
    t, aw = q.shape
    bq = BQ
    nq = seq // bq
    n_sel = min(TOPK_ATTN, seq // 4)
    qtile = lambda b, i: (b * nq + i, 0)
    whole = lambda b, i: (b, 0)
    kern = functools.partial(_dsa_kernel, n_sel=n_sel)
    return pl.pallas_call(
        kern,
        out_shape=jax.ShapeDtypeStruct((t, aw), BF16),
        grid=(batch, nq),
        in_specs=[
            pl.BlockSpec((bq, aw), qtile),
            pl.BlockSpec((bq, aw), qtile),
            pl.BlockSpec((bq, LANES), qtile),
            pl.BlockSpec((seq, aw), whole),
            pl.BlockSpec((seq, aw), whole),
            pl.BlockSpec((seq, LANES), whole),
            pl.BlockSpec(tri.shape, lambda b, i: (0, 0)),
        ],
        out_specs=pl.BlockSpec((bq, aw), qtile),
        scratch_shapes=[
            pltpu.VMEM((seq // BK, bq, BK), I32),
            pltpu.VMEM((seq // BK, bq, BK), F32),
            pltpu.VMEM((IDX_HEADS, bq, LANES), F32),
            pltpu.VMEM((N_HEADS, bq, LANES), F32),
            pltpu.VMEM((N_HEADS, bq, LANES), F32),
            pltpu.VMEM((N_HEADS, bq, HEAD_DIM), F32),
        ],
        compiler_params=pltpu.CompilerParams(
            dimension_semantics=("arbitrary", "arbitrary"), vmem_limit_bytes=VMEM_LIMIT),
        name="dsa_attention",
    )(q, qi, kw, k, v, kw, tri)


def _post_kernel(x_ref, attn_ref, gyc_ref, sga_ref, mod_ref, wao_ref, wo_ref, o_ref):
    g1 = mod_ref[0, 2:3, :]
    y_attn = jnp.dot(attn_ref[...], wao_ref[...], preferred_element_type=F32)
    mix = gyc_ref[...] + sga_ref[...] * y_attn
    delta = jnp.dot(mix.astype(BF16), wo_ref[...], preferred_element_type=F32)
    o_ref[...] = x_ref[...] + g1 * delta


def _post_call(xf, attn, gyc, sga, mod3, wao, wo, seq):
    t, d = xf.shape
    bt = BT_PROJ
    tiles_per_seq = seq // bt
    aw = attn.shape[1]
    tile = lambda i: (i, 0)
    const = lambda i: (0, 0)
    return pl.pallas_call(
        _post_kernel,
        out_shape=jax.ShapeDtypeStruct((t, d), F32),
        grid=(t // bt,),
        in_specs=[
            pl.BlockSpec((bt, d), tile),
            pl.BlockSpec((bt, aw), tile),
            pl.BlockSpec((bt, d), tile),
            pl.BlockSpec((bt, d), tile),
            pl.BlockSpec((1, N_MOD, d), lambda i: (i // tiles_per_seq, 0, 0)),
            pl.BlockSpec(wao.shape, const),
            pl.BlockSpec(wo.shape, const),
        ],
        out_specs=pl.BlockSpec((bt, d), tile),
        compiler_params=pltpu.CompilerParams(
            dimension_semantics=("arbitrary",), vmem_limit_bytes=VMEM_LIMIT),
        name="post_mix",
    )(xf, attn, gyc, sga, mod3, wao, wo)


def _col_max(a):
    return jnp.max(a, axis=0, keepdims=True)


def _peer_kernel(x_ref, mod_ref, n2_ref, wqt_ref, sk_ref, u_ref, vt_ref, o_ref,
                 h2_ref, s1_ref, s2_ref, ap_ref, bp_ref, thr_ref, w_ref, acc_ref):
    e = pl.program_id(1)
    n_e = pl.num_programs(1)
    tt = x_ref.shape[0]
    nk = PEER_N_KEYS
    n_i = u_ref.shape[0] // nk
    nslab = tt // LANES

    @pl.when(e == 0)
    def _route():
        xv = x_ref[...]
        sh2 = mod_ref[0, 3:4, :]
        sc2 = mod_ref[0, 4:5, :]
        ms = jnp.mean(xv * xv, axis=-1, keepdims=True)
        h2 = ((xv * lax.rsqrt(ms + EPS) * n2_ref[...]) * (1.0 + sc2) + sh2).astype(BF16)
        h2_ref[...] = h2
        qt = _nt_dot(wqt_ref[...], h2).astype(BF16)
        for h in range(PEER_HEADS):
            for p, dst in ((0, s1_ref), (1, s2_ref)):
                r0 = (h * 2 + p) * PEER_HALF
                dst[h] = jnp.dot(sk_ref[h, p], qt[r0:r0 + PEER_HALF, :],
                                 preferred_element_type=F32)

        neg_inf = jnp.float32(-jnp.inf)

        def top_vals(arrs, n):
            vals = []
            for _ in range(n):
                m = functools.reduce(jnp.maximum, [_col_max(a) for a in arrs])
                vals.append(m)
                arrs = [jnp.where(a == m, neg_inf, a) for a in arrs]
            return vals

        def stack_rows(rows):
            rid = lax.broadcasted_iota(I32, (len(rows), LANES), 0)
            out = jnp.zeros((len(rows), LANES), F32)
            for r, v in enumerate(rows):
                out = jnp.where(rid == r, v, out)
            return out

        def route_head(h, carry):
            for c in range(nslab):
                ls = slice(c * LANES, (c + 1) * LANES)
                s1 = s1_ref[h, :, ls]
                s2 = s2_ref[h, :, ls]
                a = top_vals([s1], PEER_TOPK)
                b = top_vals([s2], PEER_TOPK)
                b16 = stack_rows(b)
                b8 = stack_rows(b[:8])
                cands = [a[0] + b16] + [a[r] + b8 for r in range(1, PEER_TOPK)]
                t = top_vals(cands, PEER_TOPK)
                z = functools.reduce(jnp.add, [jnp.exp(tk - t[0]) for tk in t])
                thr_ref[h, :, ls] = jnp.broadcast_to(t[PEER_TOPK - 1], (8, LANES))
                ap_ref[h, :, ls] = jnp.exp(s1 - a[0])
                bp_ref[h, :, ls] = jnp.exp(s2 - b[0]) / z
            return carry

        lax.fori_loop(0, PEER_HEADS, route_head, 0)
        acc_ref[...] = jnp.zeros(acc_ref.shape, F32)

    irow0 = pl.multiple_of(e * n_i, n_i)
    for c in range(nslab):
        ls = slice(c * LANES, (c + 1) * LANES)
        s1_blk = [s1_ref[h, pl.ds(irow0, n_i), ls] for h in range(PEER_HEADS)]
        ap_blk = [ap_ref[h, pl.ds(irow0, n_i), ls] for h in range(PEER_HEADS)]
        for ii in range(n_i):
            acc = jnp.zeros((nk, LANES), F32)
            for h in range(PEER_HEADS):
                s = s1_blk[h][ii:ii + 1] + s2_ref[h, :, ls]
                hit = s >= thr_ref[h, 0:1, ls]
                acc = acc + jnp.where(hit, bp_ref[h, :, ls], 0.0) * ap_blk[h][ii:ii + 1]
            w_ref[ii * nk:(ii + 1) * nk, ls] = acc

    act = _nt_dot(u_ref[...], h2_ref[...])
    gelu = 0.5 * act * (1.0 + lax.erf(act * (2.0 ** -0.5)))
    wa = (w_ref[...] * gelu).astype(BF16)
    acc_ref[...] += jnp.dot(vt_ref[...], wa, preferred_element_type=F32)

    @pl.when(e == n_e - 1)
    def _finish():
        g2 = mod_ref[0, 5:6, :]
        o_ref[...] = x_ref[...] + g2 * acc_ref[...].T


def _peer_call(x1, mod3, n2, wqt, sk, u, vt, seq):
    t, d = x1.shape
    tt = TT_PEER
    ec = EC_PEER
    n_exp = u.shape[0]
    tiles_per_seq = seq // tt
    tile = lambda i, e: (i, 0)
    const2 = lambda i, e: (0, 0)
    return pl.pallas_call(
        _peer_kernel,
        out_shape=jax.ShapeDtypeStruct((t, d), F32),
        grid=(t // tt, n_exp // ec),
        in_specs=[
            pl.BlockSpec((tt, d), tile),
            pl.BlockSpec((1, N_MOD, d), lambda i, e: (i // tiles_per_seq, 0, 0)),
            pl.BlockSpec((1, d), const2),
            pl.BlockSpec(wqt.shape, const2),
            pl.BlockSpec(sk.shape, lambda i, e: (0, 0, 0, 0)),
            pl.BlockSpec((ec, d), lambda i, e: (e, 0)),
            pl.BlockSpec((d, ec), lambda i, e: (0, e)),
        ],
        out_specs=pl.BlockSpec((tt, d), tile),
        scratch_shapes=[
            pltpu.VMEM((tt, d), BF16),
            pltpu.VMEM((PEER_HEADS, PEER_N_KEYS, tt), F32),
            pltpu.VMEM((PEER_HEADS, PEER_N_KEYS, tt), F32),
            pltpu.VMEM((PEER_HEADS, PEER_N_KEYS, tt), F32),
            pltpu.VMEM((PEER_HEADS, PEER_N_KEYS, tt), F32),
            pltpu.VMEM((PEER_HEADS, 8, tt), F32),
            pltpu.VMEM((ec, tt), F32),
            pltpu.VMEM((d, tt), F32),
        ],
        compiler_params=pltpu.CompilerParams(
            dimension_semantics=("arbitrary", "arbitrary"), vmem_limit_bytes=VMEM_LIMIT),
        name="peer",
    )(x1, mod3, n2, wqt, sk, u, vt)


def kernel(x, c, w_ada, b_ada, norm1_w, w_in, conv_w, w_conv_out, q_norm_w, k_norm_w,
           w_attn_out, w_o, norm2_w, w_peer_q, peer_sub_keys, peer_u, peer_v):
    b, s, d = x.shape
    depth = w_ada.shape[0]
    cw = conv_w.shape[2]
    aw = N_HEADS * HEAD_DIM
    iw = IDX_HEADS * IDX_DIM
    assert s % BQ == 0 and s % TT_PEER == 0 and s % BT_PROJ == 0
    assert peer_u.shape[1] == PEER_N_KEYS * PEER_N_KEYS

    gidx = jnp.arange(aw) // HEAD_DIM
    gmat = (gidx[:, None] == gidx[None, :]).astype(BF16)
    tri = (jnp.arange(BK)[:, None] <= jnp.arange(BK)[None, :]).astype(BF16)

    xf = x.reshape(b * s, d)
    for layer in range(depth):
        w = w_in[layer]
        o = 0
        wconv = w[:, o:o + 3 * cw].astype(BF16); o += 3 * cw
        wqkv = w[:, o:o + 3 * aw].astype(BF16); o += 3 * aw
        wqi = w[:, o:o + iw].astype(BF16); o += iw
        wkw = jnp.pad(w[:, o:o + IDX_DIM + IDX_HEADS], ((0, 0), (0, LANES - IDX_DIM - IDX_HEADS))).astype(BF16)
        o += IDX_DIM + IDX_HEADS
        wg = w[:, o:o + 2 * d].astype(BF16)
        qn_t = jnp.tile(q_norm_w[layer], N_HEADS).reshape(1, aw)
        kn_t = jnp.tile(k_norm_w[layer], N_HEADS).reshape(1, aw)

        mod3 = _mod_call(c, w_ada[layer], b_ada[layer]).reshape(b, N_MOD, d)
        gyc, sga, q, k, v, qi, kw = _inproj_call(
            xf, mod3, norm1_w[layer].reshape(1, d), wconv, wqkv, wqi, wkw, wg, conv_w[layer],
            w_conv_out[layer].astype(BF16), qn_t, kn_t, gmat, s)
        attn = _dsa_call(q, qi, kw, k, v, tri, b, s)
        x1 = _post_call(xf, attn, gyc, sga, mod3, w_attn_out[layer].astype(BF16),
                        w_o[layer].astype(BF16), s)
        xf = _peer_call(
            x1, mod3, norm2_w[layer].reshape(1, d), w_peer_q[layer].T.astype(BF16),
            peer_sub_keys[layer].astype(BF16), peer_u[layer].astype(BF16),
            peer_v[layer].T.astype(BF16), s)
    return xf.reshape(b, s, d)
```

```python
import functools

import jax
import jax.numpy as jnp
from jax import lax
from jax.experimental import pallas as pl
from jax.experimental.pallas import tpu as pltpu

F32 = jnp.float32
BF16 = jnp.bfloat16
I32 = jnp.int32

N_HEADS = 8
HEAD_DIM = 64
IDX_HEADS = 8
IDX_DIM = 64
TOPK_ATTN = 256
PEER_HEADS = 8
PEER_N_KEYS = 128
PEER_HALF = 64
PEER_TOPK = 16
CONV_K = 3
N_MOD = 6
EPS = 1e-6

LANES = 128
INT_MIN = -(2 ** 31)
NEG_BIG = -1e30
VMEM_LIMIT = 56 * 1024 * 1024

BT_PROJ = 256
BQ = 256
BK = 256
TT_PEER = 512
EC_PEER = 1024


def _split_bf16(a):
    hi = a.astype(BF16)
    lo = (a - hi.astype(F32)).astype(BF16)
    return hi, lo


def _nt_dot(a, b):
    return lax.dot_general(a, b, (((1,), (1,)), ((), ())), preferred_element_type=F32)


def _mod_kernel(c_ref, w_ref, b_ref, o_ref):
    c = c_ref[...]
    s = c * jax.nn.sigmoid(c)
    s_hi, s_lo = _split_bf16(s)
    w_hi, w_lo = _split_bf16(w_ref[...])
    acc = jnp.dot(s_hi, w_hi, preferred_element_type=F32)
    acc += jnp.dot(s_hi, w_lo, preferred_element_type=F32)
    acc += jnp.dot(s_lo, w_hi, preferred_element_type=F32)
    o_ref[...] = acc + b_ref[...]


def _mod_call(c, w_ada, b_ada):
    b, d = c.shape
    n = w_ada.shape[1]
    bn = d
    return pl.pallas_call(
        _mod_kernel,
        out_shape=jax.ShapeDtypeStruct((b, n), F32),
        grid=(n // bn,),
        in_specs=[
            pl.BlockSpec((b, d), lambda j: (0, 0)),
            pl.BlockSpec((d, bn), lambda j: (0, j)),
            pl.BlockSpec((1, bn), lambda j: (0, j)),
        ],
        out_specs=pl.BlockSpec((b, bn), lambda j: (0, j)),
        compiler_params=pltpu.CompilerParams(
            dimension_semantics=("arbitrary",), vmem_limit_bytes=VMEM_LIMIT),
        name="adaln_mod",
    )(c, w_ada, b_ada.reshape(1, n))


def _inproj_kernel(x_ref, xh_ref, mod_ref, n1_ref, wconv_ref, wqkv_ref, wqi_ref, wkw_ref,
                   wg_ref, convw_ref, wco_ref, qn_ref, kn_ref, gmat_ref,
                   gyc_ref, sga_ref, q_ref, k_ref, v_ref, qi_ref, kw_ref, *, tiles_per_seq):
    i = pl.program_id(0)
    cw = convw_ref.shape[1]
    aw = q_ref.shape[1]
    d = x_ref.shape[1]
    bt = x_ref.shape[0]
    sh1 = mod_ref[0, 0:1, :]
    sc1 = mod_ref[0, 1:2, :]

    def norm_mod(xv):
        ms = jnp.mean(xv * xv, axis=-1, keepdims=True)
        y = xv * lax.rsqrt(ms + EPS) * n1_ref[...]
        return (y * (1.0 + sc1) + sh1).astype(BF16)

    h = norm_mod(x_ref[...])
    hh = norm_mod(xh_ref[...])

    pc = jnp.dot(h, wconv_ref[...], preferred_element_type=F32)
    cb, cc, cx = pc[:, :cw], pc[:, cw:2 * cw], pc[:, 2 * cw:]
    u = cc * cx
    ph = jnp.dot(hh, wconv_ref[:, cw:], preferred_element_type=F32)
    uh = ph[:, :cw] * ph[:, cw:]
    uh = jnp.where(i % tiles_per_seq == 0, 0.0, uh)
    row = lax.broadcasted_iota(I32, (bt, cw), 0)
    u1 = jnp.where(row == 0, uh[7:8, :], pltpu.roll(u, 1, 0))
    u2 = pltpu.roll(u, 2, 0)
    u2 = jnp.where(row == 0, uh[6:7, :], jnp.where(row == 1, uh[7:8, :], u2))
    conv = convw_ref[0:1, :] * u2 + convw_ref[1:2, :] * u1 + convw_ref[2:3, :] * u
    yc = jnp.dot((cb * conv).astype(BF16), wco_ref[...], preferred_element_type=F32)

    pg = jnp.dot(h, wg_ref[...], preferred_element_type=F32)
    gyc_ref[...] = jax.nn.sigmoid(pg[:, :d]) * yc
    sga_ref[...] = jax.nn.sigmoid(pg[:, d:])

    pq = jnp.dot(h, wqkv_ref[...], preferred_element_type=F32)
    gmat = gmat_ref[...]

    def head_norm(t, w_row):
        hi, lo = _split_bf16(t * t)
        ss = (jnp.dot(hi, gmat, preferred_element_type=F32)
              + jnp.dot(lo, gmat, preferred_element_type=F32))
        return t * lax.rsqrt(ss * (1.0 / HEAD_DIM) + EPS) * w_row

    qn = head_norm(pq[:, :aw], qn_ref[...])
    q_ref[0] = (qn * (HEAD_DIM ** -0.5)).T.astype(BF16)
    k_ref[...] = head_norm(pq[:, aw:2 * aw], kn_ref[...]).astype(BF16)
    v_ref[0] = pq[:, 2 * aw:].T.astype(BF16)

    qi_ref[0] = jnp.dot(h, wqi_ref[...], preferred_element_type=F32).T.astype(BF16)
    kw_ref[...] = jnp.dot(h, wkw_ref[...], preferred_element_type=F32)


def _inproj_call(xf, mod3, n1, wconv, wqkv, wqi, wkw, wg, convw, wco, qn_t, kn_t, gmat, seq):
    t, d = xf.shape
    bt = BT_PROJ
    tiles_per_seq = seq // bt
    cw = convw.shape[1]
    aw = qn_t.shape[1]
    const = lambda i: (0, 0)
    tile = lambda i: (i, 0)
    tile_t = lambda i: (i, 0, 0)
    halo = lambda i: (jnp.maximum(i * (bt // 8) - 1, 0), 0)
    kern = functools.partial(_inproj_kernel, tiles_per_seq=tiles_per_seq)
    return pl.pallas_call(
        kern,
        out_shape=[
            jax.ShapeDtypeStruct((t, d), F32),
            jax.ShapeDtypeStruct((t, d), F32),
            jax.ShapeDtypeStruct((t // bt, aw, bt), BF16),
            jax.ShapeDtypeStruct((t, aw), BF16),
            jax.ShapeDtypeStruct((t // bt, aw, bt), BF16),
            jax.ShapeDtypeStruct((t // bt, aw, bt), BF16),
            jax.ShapeDtypeStruct((t, LANES), F32),
        ],
        grid=(t // bt,),
        in_specs=[
            pl.BlockSpec((bt, d), tile),
            pl.BlockSpec((8, d), halo),
            pl.BlockSpec((1, N_MOD, d), lambda i: (i // tiles_per_seq, 0, 0)),
            pl.BlockSpec((1, d), const),
            pl.BlockSpec(wconv.shape, const),
            pl.BlockSpec(wqkv.shape, const),
            pl.BlockSpec(wqi.shape, const),
            pl.BlockSpec(wkw.shape, const),
            pl.BlockSpec(wg.shape, const),
            pl.BlockSpec(convw.shape, const),
            pl.BlockSpec(wco.shape, const),
            pl.BlockSpec((1, aw), const),
            pl.BlockSpec((1, aw), const),
            pl.BlockSpec(gmat.shape, const),
        ],
        out_specs=[
            pl.BlockSpec((bt, d), tile),
            pl.BlockSpec((bt, d), tile),
            pl.BlockSpec((1, aw, bt), tile_t),
            pl.BlockSpec((bt, aw), tile),
            pl.BlockSpec((1, aw, bt), tile_t),
            pl.BlockSpec((1, aw, bt), tile_t),
            pl.BlockSpec((bt, LANES), tile),
        ],
        compiler_params=pltpu.CompilerParams(
            dimension_semantics=("arbitrary",), vmem_limit_bytes=VMEM_LIMIT),
        name="inproj",
    )(xf, xf, mod3, n1, wconv, wqkv, wqi, wkw, wg, convw, wco, qn_t, kn_t, gmat)


def _tree_sum(parts):
    parts = list(parts)
    while len(parts) > 1:
        nxt = [parts[a] + parts[a + 1] for a in range(0, len(parts) - 1, 2)]
        if len(parts) % 2:
            nxt.append(parts[-1])
        parts = nxt
    return parts[0]


def _dsa_kernel(qt_ref, qit_ref, kwq_ref, k_ref, vt_ref, kwk_ref, tri_ref, o_ref,
                keys_ref, bias_ref, qz_ref, qiz_ref, m_ref, l_ref, acc_ref, *, n_sel):
    i = pl.program_id(1)
    bq = qt_ref.shape[2]
    bk = BK
    n_tiles = (i * bq) // bk + bq // bk
    ngrp = bk // 8

    zpad = jnp.zeros((LANES - HEAD_DIM, bq), BF16)
    for h in range(N_HEADS):
        qh = qt_ref[0, h * HEAD_DIM:(h + 1) * HEAD_DIM, :]
        qz_ref[h] = jnp.concatenate([qh, zpad] if h % 2 == 0 else [zpad, qh], axis=0)
        qiz_ref[h] = jnp.concatenate([qit_ref[0, h * IDX_DIM:(h + 1) * IDX_DIM, :], zpad], axis=0)
    wt = kwq_ref[...].T

    q_pos = i * bq + lax.broadcasted_iota(I32, (bk, bq), 1)

    def score_tile(j, carry):
        kw = kwk_ref[pl.ds(pl.multiple_of(j * bk, bk), bk), :].astype(BF16)
        score = jnp.zeros((bk, bq), F32)
        for h in range(IDX_HEADS):
            logits = jnp.dot(kw, qiz_ref[h], preferred_element_type=F32)
            score = score + wt[IDX_DIM + h:IDX_DIM + h + 1, :] * jnp.maximum(logits, 0.0)
        bits = pltpu.bitcast(score, I32)
        key = jnp.where(bits < 0, bits ^ jnp.int32(0x7FFFFFFF), bits)
        key = jnp.where(bits == jnp.int32(INT_MIN), 0, key)
        key_pos = j * bk + lax.broadcasted_iota(I32, (bk, bq), 0)
        keys_ref[j] = jnp.where(key_pos <= q_pos, key, jnp.int32(INT_MIN))
        return carry

    lax.fori_loop(0, n_tiles, score_tile, 0)

    def count_ge(cand, strict):
        cand_b = jnp.broadcast_to(cand, (8, bq))

        def body(j, cnt):
            hits = []
            for g in range(ngrp):
                ks = keys_ref[j, g * 8:(g + 1) * 8, :]
                hit = (ks > cand_b) if strict else (ks >= cand_b)
                hits.append(jnp.where(hit, 1, 0))
            return cnt + _tree_sum(hits)

        cnt = lax.fori_loop(0, n_tiles, body, jnp.zeros((8, bq), I32))
        return jnp.sum(cnt, axis=0, keepdims=True)

    def bit_step(b, t_u):
        cand_u = t_u | lax.shift_left(jnp.int32(1), 31 - b)
        total = count_ge(cand_u ^ jnp.int32(INT_MIN), False)
        return jnp.where(total >= n_sel, cand_u, t_u)

    t_u = lax.fori_loop(0, 32, bit_step, jnp.zeros((1, bq), I32))
    thr = t_u ^ jnp.int32(INT_MIN)
    n_gt = count_ge(thr, True)
    need = jnp.where(thr == jnp.int32(INT_MIN), 0, n_sel - n_gt).astype(F32)

    tri = tri_ref[...]

    def bias_tile(j, carry):
        kt = keys_ref[j]
        eq = kt == thr
        eq_f = jnp.where(eq, 1.0, 0.0)
        prefix = jnp.dot(tri, eq_f.astype(BF16), preferred_element_type=F32) + carry
        sel = (kt > thr) | (eq & (prefix <= need))
        bias_ref[j] = jnp.where(sel, 0.0, NEG_BIG)
        return prefix[bk - 1:bk, :]

    lax.fori_loop(0, n_tiles, bias_tile, jnp.zeros((1, bq), F32))

    m_ref[...] = jnp.full(m_ref.shape, NEG_BIG, F32)
    l_ref[...] = jnp.zeros(l_ref.shape, F32)
    acc_ref[...] = jnp.zeros(acc_ref.shape, F32)

    def attn_tile(j, carry):
        kt = k_ref[pl.ds(pl.multiple_of(j * bk, bk), bk), :]
        bias = bias_ref[j]
        for h in range(N_HEADS):
            pair = slice((h // 2) * LANES, (h // 2 + 1) * LANES)
            s = jnp.dot(kt[:, pair], qz_ref[h], preferred_element_type=F32) + bias
            m_prev = m_ref[h:h + 1, :]
            m_new = jnp.maximum(m_prev, jnp.max(s, axis=0, keepdims=True))
            alpha = jnp.exp(m_prev - m_new)
            p = jnp.exp(s - m_new)
            l_ref[h:h + 1, :] = alpha * l_ref[h:h + 1, :] + jnp.sum(p, axis=0, keepdims=True)
            pv = jnp.dot(vt_ref[j, h * HEAD_DIM:(h + 1) * HEAD_DIM, :], p.astype(BF16),
                         preferred_element_type=F32)
            acc_ref[h] = alpha * acc_ref[h] + pv
            m_ref[h:h + 1, :] = m_new
        return carry

    lax.fori_loop(0, n_tiles, attn_tile, 0)

    outs = [acc_ref[h] / l_ref[h:h + 1, :] for h in range(N_HEADS)]
    o_ref[...] = jnp.concatenate(outs, axis=0).T.astype(o_ref.dtype)


def _dsa_call(qt, qit, kw, k, vt, tri, batch, seq):
    nt, aw, bq = qt.shape
    assert bq == BQ and bq == BK
    nq = seq // bq
    n_sel = min(TOPK_ATTN, seq // 4)
    qtile = lambda b, i: (b * nq + i, 0)
    qtile_t = lambda b, i: (b * nq + i, 0, 0)
    whole = lambda b, i: (b, 0)
    kern = functools.partial(_dsa_kernel, n_sel=n_sel)
    return pl.pallas_call(
        kern,
        out_shape=jax.ShapeDtypeStruct((batch * seq, aw), BF16),
        grid=(batch, nq),
        in_specs=[
            pl.BlockSpec((1, aw, bq), qtile_t),
            pl.BlockSpec((1, aw, bq), qtile_t),
            pl.BlockSpec((bq, LANES), qtile),
            pl.BlockSpec((seq, aw), whole),
            pl.BlockSpec((seq // BK, aw, BK), lambda b, i: (b, 0, 0)),
            pl.BlockSpec((seq, LANES), whole),
            pl.BlockSpec(tri.shape, lambda b, i: (0, 0)),
        ],
        out_specs=pl.BlockSpec((bq, aw), qtile),
        scratch_shapes=[
            pltpu.VMEM((seq // BK, BK, bq), I32),
            pltpu.VMEM((seq // BK, BK, bq), F32),
            pltpu.VMEM((N_HEADS, LANES, bq), BF16),
            pltpu.VMEM((IDX_HEADS, LANES, bq), BF16),
            pltpu.VMEM((N_HEADS, bq), F32),
            pltpu.VMEM((N_HEADS, bq), F32),
            pltpu.VMEM((N_HEADS, HEAD_DIM, bq), F32),
        ],
        compiler_params=pltpu.CompilerParams(
            dimension_semantics=("arbitrary", "arbitrary"), vmem_limit_bytes=VMEM_LIMIT),
        name="dsa_attention",
    )(qt, qit, kw, k, vt, kw, tri)


def _post_kernel(x_ref, attn_ref, gyc_ref, sga_ref, mod_ref, wao_ref, wo_ref, o_ref):
    g1 = mod_ref[0, 2:3, :]
    y_attn = jnp.dot(attn_ref[...], wao_ref[...], preferred_element_type=F32)
    mix = gyc_ref[...] + sga_ref[...] * y_attn
    delta = jnp.dot(mix.astype(BF16), wo_ref[...], preferred_element_type=F32)
    o_ref[...] = x_ref[...] + g1 * delta


def _post_call(xf, attn, gyc, sga, mod3, wao, wo, seq):
    t, d = xf.shape
    bt = BT_PROJ
    tiles_per_seq = seq // bt
    aw = attn.shape[1]
    tile = lambda i: (i, 0)
    const = lambda i: (0, 0)
    return pl.pallas_call(
        _post_kernel,
        out_shape=jax.ShapeDtypeStruct((t, d), F32),
        grid=(t // bt,),
        in_specs=[
            pl.BlockSpec((bt, d), tile),
            pl.BlockSpec((bt, aw), tile),
            pl.BlockSpec((bt, d), tile),
            pl.BlockSpec((bt, d), tile),
            pl.BlockSpec((1, N_MOD, d), lambda i: (i // tiles_per_seq, 0, 0)),
            pl.BlockSpec(wao.shape, const),
            pl.BlockSpec(wo.shape, const),
        ],
        out_specs=pl.BlockSpec((bt, d), tile),
        compiler_params=pltpu.CompilerParams(
            dimension_semantics=("arbitrary",), vmem_limit_bytes=VMEM_LIMIT),
        name="post_mix",
    )(xf, attn, gyc, sga, mod3, wao, wo)


def _col_max(a):
    return jnp.max(a, axis=0, keepdims=True)


def _peer_kernel(x_ref, mod_ref, n2_ref, wqt_ref, sk_ref, u_ref, vt_ref, o_ref,
                 h2_ref, s1_ref, s2_ref, ap_ref, bp_ref, thr_ref, w_ref, acc_ref):
    e = pl.program_id(1)
    n_e = pl.num_programs(1)
    tt = x_ref.shape[0]
    nk = PEER_N_KEYS
    n_i = u_ref.shape[0] // nk
    nslab = tt // LANES

    @pl.when(e == 0)
    def _route():
        xv = x_ref[...]
        sh2 = mod_ref[0, 3:4, :]
        sc2 = mod_ref[0, 4:5, :]
        ms = jnp.mean(xv * xv, axis=-1, keepdims=True)
        h2 = ((xv * lax.rsqrt(ms + EPS) * n2_ref[...]) * (1.0 + sc2) + sh2).astype(BF16)
        h2_ref[...] = h2
        qt = _nt_dot(wqt_ref[...], h2).astype(BF16)
        for h in range(PEER_HEADS):
            for p, dst in ((0, s1_ref), (1, s2_ref)):
                r0 = (h * 2 + p) * PEER_HALF
                dst[h] = jnp.dot(sk_ref[h, p], qt[r0:r0 + PEER_HALF, :],
                                 preferred_element_type=F32)

        neg_inf = jnp.float32(-jnp.inf)

        def top_vals(arrs, n):
            vals = []
            for _ in range(n):
                m = functools.reduce(jnp.maximum, [_col_max(a) for a in arrs])
                vals.append(m)
                arrs = [jnp.where(a == m, neg_inf, a) for a in arrs]
            return vals

        def stack_rows(rows):
            rid = lax.broadcasted_iota(I32, (len(rows), LANES), 0)
            out = jnp.zeros((len(rows), LANES), F32)
            for r, v in enumerate(rows):
                out = jnp.where(rid == r, v, out)
            return out

        def route_head(h, carry):
            for c in range(nslab):
                ls = slice(c * LANES, (c + 1) * LANES)
                s1 = s1_ref[h, :, ls]
                s2 = s2_ref[h, :, ls]
                a = top_vals([s1], PEER_TOPK)
                b = top_vals([s2], PEER_TOPK)
                b16 = stack_rows(b)
                b8 = stack_rows(b[:8])
                cands = [a[0] + b16] + [a[r] + b8 for r in range(1, PEER_TOPK)]
                t = top_vals(cands, PEER_TOPK)
                z = functools.reduce(jnp.add, [jnp.exp(tk - t[0]) for tk in t])
                thr_ref[h, :, ls] = jnp.broadcast_to(t[PEER_TOPK - 1], (8, LANES))
                ap_ref[h, :, ls] = jnp.exp(s1 - a[0])
                bp_ref[h, :, ls] = jnp.exp(s2 - b[0]) / z
            return carry

        lax.fori_loop(0, PEER_HEADS, route_head, 0)
        acc_ref[...] = jnp.zeros(acc_ref.shape, F32)

    irow0 = pl.multiple_of(e * n_i, n_i)
    for c in range(nslab):
        ls = slice(c * LANES, (c + 1) * LANES)
        s1_blk = [s1_ref[h, pl.ds(irow0, n_i), ls] for h in range(PEER_HEADS)]
        ap_blk = [ap_ref[h, pl.ds(irow0, n_i), ls] for h in range(PEER_HEADS)]
        for ii in range(n_i):
            acc = jnp.zeros((nk, LANES), F32)
            for h in range(PEER_HEADS):
                s = s1_blk[h][ii:ii + 1] + s2_ref[h, :, ls]
                hit = s >= thr_ref[h, 0:1, ls]
                acc = acc + jnp.where(hit, bp_ref[h, :, ls], 0.0) * ap_blk[h][ii:ii + 1]
            w_ref[ii * nk:(ii + 1) * nk, ls] = acc

    act = _nt_dot(u_ref[...], h2_ref[...])
    gelu = 0.5 * act * (1.0 + lax.erf(act * (2.0 ** -0.5)))
    wa = (w_ref[...] * gelu).astype(BF16)
    acc_ref[...] += jnp.dot(vt_ref[...], wa, preferred_element_type=F32)

    @pl.when(e == n_e - 1)
    def _finish():
        g2 = mod_ref[0, 5:6, :]
        o_ref[...] = x_ref[...] + g2 * acc_ref[...].T


def _peer_call(x1, mod3, n2, wqt, sk, u, vt, seq):
    t, d = x1.shape
    tt = TT_PEER
    ec = EC_PEER
    n_exp = u.shape[0]
    tiles_per_seq = seq // tt
    tile = lambda i, e: (i, 0)
    const2 = lambda i, e: (0, 0)
    return pl.pallas_call(
        _peer_kernel,
        out_shape=jax.ShapeDtypeStruct((t, d), F32),
        grid=(t // tt, n_exp // ec),
        in_specs=[
            pl.BlockSpec((tt, d), tile),
            pl.BlockSpec((1, N_MOD, d), lambda i, e: (i // tiles_per_seq, 0, 0)),
            pl.BlockSpec((1, d), const2),
            pl.BlockSpec(wqt.shape, const2),
            pl.BlockSpec(sk.shape, lambda i, e: (0, 0, 0, 0)),
            pl.BlockSpec((ec, d), lambda i, e: (e, 0)),
            pl.BlockSpec((d, ec), lambda i, e: (0, e)),
        ],
        out_specs=pl.BlockSpec((tt, d), tile),
        scratch_shapes=[
            pltpu.VMEM((tt, d), BF16),
            pltpu.VMEM((PEER_HEADS, PEER_N_KEYS, tt), F32),
            pltpu.VMEM((PEER_HEADS, PEER_N_KEYS, tt), F32),
            pltpu.VMEM((PEER_HEADS, PEER_N_KEYS, tt), F32),
            pltpu.VMEM((PEER_HEADS, PEER_N_KEYS, tt), F32),
            pltpu.VMEM((PEER_HEADS, 8, tt), F32),
            pltpu.VMEM((ec, tt), F32),
            pltpu.VMEM((d, tt), F32),
        ],
        compiler_params=pltpu.CompilerParams(
            dimension_semantics=("arbitrary", "arbitrary"), vmem_limit_bytes=VMEM_LIMIT),
        name="peer",
    )(x1, mod3, n2, wqt, sk, u, vt)


def kernel(x, c, w_ada, b_ada, norm1_w, w_in, conv_w, w_conv_out, q_norm_w, k_norm_w,
           w_attn_out, w_o, norm2_w, w_peer_q, peer_sub_keys, peer_u, peer_v):
    b, s, d = x.shape
    depth = w_ada.shape[0]
    cw = conv_w.shape[2]
    aw = N_HEADS * HEAD_DIM
    iw = IDX_HEADS * IDX_DIM
    assert s % BQ == 0 and s % TT_PEER == 0 and s % BT_PROJ == 0
    assert peer_u.shape[1] == PEER_N_KEYS * PEER_N_KEYS

    gidx = jnp.arange(aw) // HEAD_DIM
    gmat = (gidx[:, None] == gidx[None, :]).astype(BF16)
    tri = (jnp.arange(BK)[:, None] >= jnp.arange(BK)[None, :]).astype(BF16)

    xf = x.reshape(b * s, d)
    for layer in range(depth):
        w = w_in[layer]
        o = 0
        wconv = w[:, o:o + 3 * cw].astype(BF16); o += 3 * cw
        wqkv = w[:, o:o + 3 * aw].astype(BF16); o += 3 * aw
        wqi = w[:, o:o + iw].astype(BF16); o += iw
        wkw = jnp.pad(w[:, o:o + IDX_DIM + IDX_HEADS], ((0, 0), (0, LANES - IDX_DIM - IDX_HEADS))).astype(BF16)
        o += IDX_DIM + IDX_HEADS
        wg = w[:, o:o + 2 * d].astype(BF16)
        qn_t = jnp.tile(q_norm_w[layer], N_HEADS).reshape(1, aw)
        kn_t = jnp.tile(k_norm_w[layer], N_HEADS).reshape(1, aw)

        mod3 = _mod_call(c, w_ada[layer], b_ada[layer]).reshape(b, N_MOD, d)
        gyc, sga, qt, k, vt, qit, kw = _inproj_call(
            xf, mod3, norm1_w[layer].reshape(1, d), wconv, wqkv, wqi, wkw, wg, conv_w[layer],
            w_conv_out[layer].astype(BF16), qn_t, kn_t, gmat, s)
        attn = _dsa_call(qt, qit, kw, k, vt, tri, b, s)
        x1 = _post_call(xf, attn, gyc, sga, mod3, w_attn_out[layer].astype(BF16),
                        w_o[layer].astype(BF16), s)
        xf = _peer_call(
            x1, mod3, norm2_w[layer].reshape(1, d), w_peer_q[layer].T.astype(BF16),
            peer_sub_keys[layer].astype(BF16), peer_u[layer].astype(BF16),
            peer_v[layer].T.astype(BF16), s)
    return xf.reshape(b, s, d)
```

```python
import functools

import jax
import jax.numpy as jnp
from jax import lax
from jax.experimental import pallas as pl
from jax.experimental.pallas import tpu as pltpu

F32 = jnp.float32
BF16 = jnp.bfloat16
I32 = jnp.int32

N_HEADS = 8
HEAD_DIM = 64
IDX_HEADS = 8
IDX_DIM = 64
TOPK_ATTN = 256
PEER_HEADS = 8
PEER_N_KEYS = 128
PEER_HALF = 64
PEER_TOPK = 16
CONV_K = 3
N_MOD = 6
EPS = 1e-6

LANES = 128
INT_MIN = -(2 ** 31)
NEG_BIG = -1e30
VMEM_LIMIT = 56 * 1024 * 1024

BT_PROJ = 256
BQ = 256
BK = 256
TT_PEER = 512
EC_PEER = 1024


def _split_bf16(a):
    hi = a.astype(BF16)
    lo = (a - hi.astype(F32)).astype(BF16)
    return hi, lo


def _nt_dot(a, b):
    return lax.dot_general(a, b, (((1,), (1,)), ((), ())), preferred_element_type=F32)


def _mod_kernel(c_ref, w_ref, b_ref, o_ref):
    c = c_ref[...]
    s = c * jax.nn.sigmoid(c)
    s_hi, s_lo = _split_bf16(s)
    w_hi, w_lo = _split_bf16(w_ref[...])
    acc = jnp.dot(s_hi, w_hi, preferred_element_type=F32)
    acc += jnp.dot(s_hi, w_lo, preferred_element_type=F32)
    acc += jnp.dot(s_lo, w_hi, preferred_element_type=F32)
    o_ref[...] = acc + b_ref[...]


def _mod_call(c, w_ada, b_ada):
    b, d = c.shape
    n = w_ada.shape[1]
    bn = d
    return pl.pallas_call(
        _mod_kernel,
        out_shape=jax.ShapeDtypeStruct((b, n), F32),
        grid=(n // bn,),
        in_specs=[
            pl.BlockSpec((b, d), lambda j: (0, 0)),
            pl.BlockSpec((d, bn), lambda j: (0, j)),
            pl.BlockSpec((1, bn), lambda j: (0, j)),
        ],
        out_specs=pl.BlockSpec((b, bn), lambda j: (0, j)),
        compiler_params=pltpu.CompilerParams(
            dimension_semantics=("arbitrary",), vmem_limit_bytes=VMEM_LIMIT),
        name="adaln_mod",
    )(c, w_ada, b_ada.reshape(1, n))


def _inproj_kernel(x_ref, xh_ref, mod_ref, n1_ref, wconv_ref, wqkv_ref, wqi_ref, wkw_ref,
                   wg_ref, convw_ref, wco_ref, qn_ref, kn_ref, gmat_ref,
                   gyc_ref, sga_ref, q_ref, k_ref, v_ref, qi_ref, kw_ref, *, tiles_per_seq):
    i = pl.program_id(0)
    cw = convw_ref.shape[1]
    aw = q_ref.shape[1]
    d = x_ref.shape[1]
    bt = x_ref.shape[0]
    sh1 = mod_ref[0, 0:1, :]
    sc1 = mod_ref[0, 1:2, :]

    def norm_mod(xv):
        ms = jnp.mean(xv * xv, axis=-1, keepdims=True)
        y = xv * lax.rsqrt(ms + EPS) * n1_ref[...]
        return (y * (1.0 + sc1) + sh1).astype(BF16)

    h = norm_mod(x_ref[...])
    hh = norm_mod(xh_ref[...])

    pc = jnp.dot(h, wconv_ref[...], preferred_element_type=F32)
    cb, cc, cx = pc[:, :cw], pc[:, cw:2 * cw], pc[:, 2 * cw:]
    u = cc * cx
    ph = jnp.dot(hh, wconv_ref[:, cw:], preferred_element_type=F32)
    uh = ph[:, :cw] * ph[:, cw:]
    uh = jnp.where(i % tiles_per_seq == 0, 0.0, uh)
    row = lax.broadcasted_iota(I32, (bt, cw), 0)
    u1 = jnp.where(row == 0, uh[7:8, :], pltpu.roll(u, 1, 0))
    u2 = pltpu.roll(u, 2, 0)
    u2 = jnp.where(row == 0, uh[6:7, :], jnp.where(row == 1, uh[7:8, :], u2))
    conv = convw_ref[0:1, :] * u2 + convw_ref[1:2, :] * u1 + convw_ref[2:3, :] * u
    yc = jnp.dot((cb * conv).astype(BF16), wco_ref[...], preferred_element_type=F32)

    pg = jnp.dot(h, wg_ref[...], preferred_element_type=F32)
    gyc_ref[...] = jax.nn.sigmoid(pg[:, :d]) * yc
    sga_ref[...] = jax.nn.sigmoid(pg[:, d:])

    pq = jnp.dot(h, wqkv_ref[...], preferred_element_type=F32)
    gmat = gmat_ref[...]

    def head_norm(t, w_row):
        hi, lo = _split_bf16(t * t)
        ss = (jnp.dot(hi, gmat, preferred_element_type=F32)
              + jnp.dot(lo, gmat, preferred_element_type=F32))
        return t * lax.rsqrt(ss * (1.0 / HEAD_DIM) + EPS) * w_row

    qn = head_norm(pq[:, :aw], qn_ref[...])
    q_ref[0] = (qn * (HEAD_DIM ** -0.5)).T.astype(BF16)
    k_ref[...] = head_norm(pq[:, aw:2 * aw], kn_ref[...]).astype(BF16)
    v_ref[0] = pq[:, 2 * aw:].T.astype(BF16)

    qi_ref[0] = jnp.dot(h, wqi_ref[...], preferred_element_type=F32).T.astype(BF16)
    kw_ref[...] = jnp.dot(h, wkw_ref[...], preferred_element_type=F32)


def _inproj_call(xf, mod3, n1, wconv, wqkv, wqi, wkw, wg, convw, wco, qn_t, kn_t, gmat, seq):
    t, d = xf.shape
    bt = BT_PROJ
    tiles_per_seq = seq // bt
    cw = convw.shape[1]
    aw = qn_t.shape[1]
    const = lambda i: (0, 0)
    tile = lambda i: (i, 0)
    tile_t = lambda i: (i, 0, 0)
    halo = lambda i: (jnp.maximum(i * (bt // 8) - 1, 0), 0)
    kern = functools.partial(_inproj_kernel, tiles_per_seq=tiles_per_seq)
    return pl.pallas_call(
        kern,
        out_shape=[
            jax.ShapeDtypeStruct((t, d), F32),
            jax.ShapeDtypeStruct((t, d), F32),
            jax.ShapeDtypeStruct((t // bt, aw, bt), BF16),
            jax.ShapeDtypeStruct((t, aw), BF16),
            jax.ShapeDtypeStruct((t // bt, aw, bt), BF16),
            jax.ShapeDtypeStruct((t // bt, aw, bt), BF16),
            jax.ShapeDtypeStruct((t, LANES), F32),
        ],
        grid=(t // bt,),
        in_specs=[
            pl.BlockSpec((bt, d), tile),
            pl.BlockSpec((8, d), halo),
            pl.BlockSpec((1, N_MOD, d), lambda i: (i // tiles_per_seq, 0, 0)),
            pl.BlockSpec((1, d), const),
            pl.BlockSpec(wconv.shape, const),
            pl.BlockSpec(wqkv.shape, const),
            pl.BlockSpec(wqi.shape, const),
            pl.BlockSpec(wkw.shape, const),
            pl.BlockSpec(wg.shape, const),
            pl.BlockSpec(convw.shape, const),
            pl.BlockSpec(wco.shape, const),
            pl.BlockSpec((1, aw), const),
            pl.BlockSpec((1, aw), const),
            pl.BlockSpec(gmat.shape, const),
        ],
        out_specs=[
            pl.BlockSpec((bt, d), tile),
            pl.BlockSpec((bt, d), tile),
            pl.BlockSpec((1, aw, bt), tile_t),
            pl.BlockSpec((bt, aw), tile),
            pl.BlockSpec((1, aw, bt), tile_t),
            pl.BlockSpec((1, aw, bt), tile_t),
            pl.BlockSpec((bt, LANES), tile),
        ],
        compiler_params=pltpu.CompilerParams(
            dimension_semantics=("arbitrary",), vmem_limit_bytes=VMEM_LIMIT),
        name="inproj",
    )(xf, xf, mod3, n1, wconv, wqkv, wqi, wkw, wg, convw, wco, qn_t, kn_t, gmat)


def _tree_sum(parts):
    parts = list(parts)
    while len(parts) > 1:
        nxt = [parts[a] + parts[a + 1] for a in range(0, len(parts) - 1, 2)]
        if len(parts) % 2:
            nxt.append(parts[-1])
        parts = nxt
    return parts[0]


def _dsa_kernel(qt_ref, qit_ref, kwq_ref, k_ref, vt_ref, kwk_ref, tri_ref, o_ref,
                keys_ref, bias_ref, qz_ref, qiz_ref, s_ref, m_ref, l_ref, acc_ref, *, n_sel):
    i = pl.program_id(1)
    bq = qt_ref.shape[2]
    bk = BK
    n_tiles = (i * bq) // bk + bq // bk
    ngrp = bk // 8

    zpad = jnp.zeros((LANES - HEAD_DIM, bq), BF16)
    for h in range(N_HEADS):
        qh = qt_ref[0, h * HEAD_DIM:(h + 1) * HEAD_DIM, :]
        qz_ref[h] = jnp.concatenate([qh, zpad] if h % 2 == 0 else [zpad, qh], axis=0)
        qiz_ref[h] = jnp.concatenate([qit_ref[0, h * IDX_DIM:(h + 1) * IDX_DIM, :], zpad], axis=0)
    wt = kwq_ref[...].T

    q_pos = i * bq + lax.broadcasted_iota(I32, (bk, bq), 1)

    def score_tile(j, carry):
        kw = kwk_ref[pl.ds(pl.multiple_of(j * bk, bk), bk), :].astype(BF16)
        score = jnp.zeros((bk, bq), F32)
        for h in range(IDX_HEADS):
            logits = jnp.dot(kw, qiz_ref[h], preferred_element_type=F32)
            score = score + wt[IDX_DIM + h:IDX_DIM + h + 1, :] * jnp.maximum(logits, 0.0)
        bits = pltpu.bitcast(score, I32)
        key = jnp.where(bits < 0, bits ^ jnp.int32(0x7FFFFFFF), bits)
        key = jnp.where(bits == jnp.int32(INT_MIN), 0, key)
        key_pos = j * bk + lax.broadcasted_iota(I32, (bk, bq), 0)
        keys_ref[j] = jnp.where(key_pos <= q_pos, key, jnp.int32(INT_MIN))
        return carry

    lax.fori_loop(0, n_tiles, score_tile, 0)

    def count_ge(cand, strict):
        cand_b = jnp.broadcast_to(cand, (8, bq))

        def body(j, cnt):
            hits = []
            for g in range(ngrp):
                ks = keys_ref[j, g * 8:(g + 1) * 8, :]
                hit = (ks > cand_b) if strict else (ks >= cand_b)
                hits.append(jnp.where(hit, 1, 0))
            return cnt + _tree_sum(hits)

        cnt = lax.fori_loop(0, n_tiles, body, jnp.zeros((8, bq), I32))
        return jnp.sum(cnt, axis=0, keepdims=True)

    def bit_step(b, t_u):
        cand_u = t_u | lax.shift_left(jnp.int32(1), 31 - b)
        total = count_ge(cand_u ^ jnp.int32(INT_MIN), False)
        return jnp.where(total >= n_sel, cand_u, t_u)

    t_u = lax.fori_loop(0, 32, bit_step, jnp.zeros((1, bq), I32))
    thr = t_u ^ jnp.int32(INT_MIN)
    n_gt = count_ge(thr, True)
    need = jnp.where(thr == jnp.int32(INT_MIN), 0, n_sel - n_gt).astype(F32)

    tri = tri_ref[...]

    def bias_tile(j, carry):
        kt = keys_ref[j]
        eq = kt == thr
        eq_f = jnp.where(eq, 1.0, 0.0)
        prefix = jnp.dot(tri, eq_f.astype(BF16), preferred_element_type=F32) + carry
        sel = (kt > thr) | (eq & (prefix <= need))
        bias_ref[j] = jnp.where(sel, 0.0, NEG_BIG)
        return prefix[bk - 1:bk, :]

    lax.fori_loop(0, n_tiles, bias_tile, jnp.zeros((1, bq), F32))

    m_ref[...] = jnp.full(m_ref.shape, NEG_BIG, F32)
    l_ref[...] = jnp.zeros(l_ref.shape, F32)
    acc_ref[...] = jnp.zeros(acc_ref.shape, F32)

    def attn_tile(j, carry):
        kt = k_ref[pl.ds(pl.multiple_of(j * bk, bk), bk), :]
        bias = bias_ref[j]
        alphas = []
        for h in range(N_HEADS):
            pair = slice((h // 2) * LANES, (h // 2 + 1) * LANES)
            s = jnp.dot(kt[:, pair], qz_ref[h], preferred_element_type=F32) + bias
            s_ref[h] = s
            m_prev = m_ref[h]
            m_new = jnp.maximum(m_prev, jnp.broadcast_to(jnp.max(s, axis=0, keepdims=True), (8, bq)))
            m_ref[h] = m_new
            alphas.append(jnp.exp(m_prev - m_new))
        for h in range(N_HEADS):
            p = jnp.exp(s_ref[h] - m_ref[h, 0:1, :])
            l_ref[h] = alphas[h] * l_ref[h] + _tree_sum([p[g * 8:(g + 1) * 8] for g in range(ngrp)])
            pv = jnp.dot(vt_ref[j, h * HEAD_DIM:(h + 1) * HEAD_DIM, :], p.astype(BF16),
                         preferred_element_type=F32)
            acc_ref[h] = alphas[h][0:1] * acc_ref[h] + pv
        return carry

    lax.fori_loop(0, n_tiles, attn_tile, 0)

    outs = [acc_ref[h] / jnp.sum(l_ref[h], axis=0, keepdims=True) for h in range(N_HEADS)]
    o_ref[...] = jnp.concatenate(outs, axis=0).T.astype(o_ref.dtype)


def _dsa_call(qt, qit, kw, k, vt, tri, batch, seq):
    nt, aw, bq = qt.shape
    assert bq == BQ and bq == BK
    nq = seq // bq
    n_sel = min(TOPK_ATTN, seq // 4)
    qtile = lambda b, i: (b * nq + i, 0)
    qtile_t = lambda b, i: (b * nq + i, 0, 0)
    whole = lambda b, i: (b, 0)
    kern = functools.partial(_dsa_kernel, n_sel=n_sel)
    return pl.pallas_call(
        kern,
        out_shape=jax.ShapeDtypeStruct((batch * seq, aw), BF16),
        grid=(batch, nq),
        in_specs=[
            pl.BlockSpec((1, aw, bq), qtile_t),
            pl.BlockSpec((1, aw, bq), qtile_t),
            pl.BlockSpec((bq, LANES), qtile),
            pl.BlockSpec((seq, aw), whole),
            pl.BlockSpec((seq // BK, aw, BK), lambda b, i: (b, 0, 0)),
            pl.BlockSpec((seq, LANES), whole),
            pl.BlockSpec(tri.shape, lambda b, i: (0, 0)),
        ],
        out_specs=pl.BlockSpec((bq, aw), qtile),
        scratch_shapes=[
            pltpu.VMEM((seq // BK, BK, bq), I32),
            pltpu.VMEM((seq // BK, BK, bq), F32),
            pltpu.VMEM((N_HEADS, LANES, bq), BF16),
            pltpu.VMEM((IDX_HEADS, LANES, bq), BF16),
            pltpu.VMEM((N_HEADS, BK, bq), F32),
            pltpu.VMEM((N_HEADS, 8, bq), F32),
            pltpu.VMEM((N_HEADS, 8, bq), F32),
            pltpu.VMEM((N_HEADS, HEAD_DIM, bq), F32),
        ],
        compiler_params=pltpu.CompilerParams(
            dimension_semantics=("arbitrary", "arbitrary"), vmem_limit_bytes=VMEM_LIMIT),
        name="dsa_attention",
    )(qt, qit, kw, k, vt, kw, tri)


def _post_kernel(x_ref, attn_ref, gyc_ref, sga_ref, mod_ref, wao_ref, wo_ref, o_ref):
    g1 = mod_ref[0, 2:3, :]
    y_attn = jnp.dot(attn_ref[...], wao_ref[...], preferred_element_type=F32)
    mix = gyc_ref[...] + sga_ref[...] * y_attn
    delta = jnp.dot(mix.astype(BF16), wo_ref[...], preferred_element_type=F32)
    o_ref[...] = x_ref[...] + g1 * delta


def _post_call(xf, attn, gyc, sga, mod3, wao, wo, seq):
    t, d = xf.shape
    bt = BT_PROJ
    tiles_per_seq = seq // bt
    aw = attn.shape[1]
    tile = lambda i: (i, 0)
    const = lambda i: (0, 0)
    return pl.pallas_call(
        _post_kernel,
        out_shape=jax.ShapeDtypeStruct((t, d), F32),
        grid=(t // bt,),
        in_specs=[
            pl.BlockSpec((bt, d), tile),
            pl.BlockSpec((bt, aw), tile),
            pl.BlockSpec((bt, d), tile),
            pl.BlockSpec((bt, d), tile),
            pl.BlockSpec((1, N_MOD, d), lambda i: (i // tiles_per_seq, 0, 0)),
            pl.BlockSpec(wao.shape, const),
            pl.BlockSpec(wo.shape, const),
        ],
        out_specs=pl.BlockSpec((bt, d), tile),
        compiler_params=pltpu.CompilerParams(
            dimension_semantics=("arbitrary",), vmem_limit_bytes=VMEM_LIMIT),
        name="post_mix",
    )(xf, attn, gyc, sga, mod3, wao, wo)


def _col_max(a):
    return jnp.max(a, axis=0, keepdims=True)


def _batcher_pairs(n):
    pairs = []
    p = 1
    while p < n:
        k = p
        while k >= 1:
            for j in range(k % p, n - k, 2 * k):
                for i in range(min(k, n - j - k)):
                    if (i + j) // (2 * p) == (i + j + k) // (2 * p):
                        pairs.append((i + j, i + j + k))
            k //= 2
        p *= 2
    return pairs


_SORT16 = _batcher_pairs(PEER_TOPK)
_BITONIC16 = [(k, k + d) for d in (8, 4, 2, 1) for k in range(PEER_TOPK) if not k & d]


def _sort_desc(v, pairs):
    v = list(v)
    for i, j in pairs:
        v[i], v[j] = jnp.maximum(v[i], v[j]), jnp.minimum(v[i], v[j])
    return v


def _peer_kernel(x_ref, mod_ref, n2_ref, wqt_ref, sk_ref, u_ref, vt_ref, o_ref,
                 h2_ref, s1_ref, s2_ref, ap_ref, bp_ref, nsel_ref, rk_ref, w_ref, acc_ref):
    e = pl.program_id(1)
    n_e = pl.num_programs(1)
    tt = x_ref.shape[0]
    nk = PEER_N_KEYS
    n_i = u_ref.shape[0] // nk
    nslab = tt // LANES

    @pl.when(e == 0)
    def _route():
        xv = x_ref[...]
        sh2 = mod_ref[0, 3:4, :]
        sc2 = mod_ref[0, 4:5, :]
        ms = jnp.mean(xv * xv, axis=-1, keepdims=True)
        h2 = ((xv * lax.rsqrt(ms + EPS) * n2_ref[...]) * (1.0 + sc2) + sh2).astype(BF16)
        h2_ref[...] = h2
        qt = _nt_dot(wqt_ref[...], h2).astype(BF16)
        for h in range(PEER_HEADS):
            for p, dst in ((0, s1_ref), (1, s2_ref)):
                r0 = (h * 2 + p) * PEER_HALF
                dst[h] = jnp.dot(sk_ref[h, p], qt[r0:r0 + PEER_HALF, :],
                                 preferred_element_type=F32)

        neg_inf = jnp.float32(-jnp.inf)
        sub_id = lax.broadcasted_iota(I32, (8, LANES), 0)

        def top16(s):
            v = _sort_desc([s[g * 8:(g + 1) * 8] for g in range(nk // 8)], _SORT16)
            for shift in (4, 2, 1):
                v = [jnp.maximum(v[r], pltpu.roll(v[PEER_TOPK - 1 - r], shift, 0)) for r in range(PEER_TOPK)]
                v = _sort_desc(v, _BITONIC16)
            return v

        def stack8(vals):
            out = vals[0]
            for r in range(1, 8):
                out = jnp.where(sub_id == r, vals[r], out)
            return out

        def route_head(h, carry):
            for c in range(nslab):
                ls = slice(c * LANES, (c + 1) * LANES)
                s1 = s1_ref[h, :, ls]
                s2 = s2_ref[h, :, ls]
                a = top16(s1)
                b = top16(s2)
                b_lo, b_hi = stack8(b[:8]), stack8(b[8:])
                cands = [a[0] + b_lo, a[0] + b_hi] + [a[r] + b_lo for r in range(1, PEER_TOPK)]
                work, t = cands, []
                for _ in range(PEER_TOPK):
                    m = _col_max(functools.reduce(jnp.maximum, work))
                    t.append(m)
                    work = [jnp.where(w == m, neg_inf, w) for w in work]
                thr = t[PEER_TOPK - 1]
                z = functools.reduce(jnp.add, [jnp.exp(tk - t[0]) for tk in t])
                hits = [jnp.where(cd >= thr, 1.0, 0.0) for cd in cands]
                n_sel = [jnp.sum(hits[0] + hits[1], axis=0, keepdims=True)]
                n_sel += [jnp.sum(hits[r + 1], axis=0, keepdims=True) for r in range(1, PEER_TOPK)]
                n_b = [jnp.broadcast_to(n, (8, LANES)) for n in n_sel]
                cnts, ranks = [], []
                for g in range(nk // 8):
                    rows = slice(g * 8, (g + 1) * 8)
                    s1g, s2g = s1[rows], s2[rows]
                    cnt = jnp.zeros((8, LANES), F32)
                    rank = jnp.ones((8, LANES), F32)
                    for r in range(PEER_TOPK):
                        cnt = jnp.where(s1g == a[r], n_b[r], cnt)
                        rank = rank + jnp.where(b[r] > s2g, 1.0, 0.0)
                    cnts.append(cnt)
                    ranks.append(rank)
                nsel_ref[h, :, ls] = jnp.concatenate(cnts, axis=0)
                rk_ref[h, :, ls] = jnp.concatenate(ranks, axis=0).astype(BF16)
                ap_ref[h, :, ls] = jnp.exp(s1 - a[0][0:1])
                bp_ref[h, :, ls] = (jnp.exp(s2 - b[0][0:1]) / z).astype(BF16)
            return carry

        lax.fori_loop(0, PEER_HEADS, route_head, 0)
        acc_ref[...] = jnp.zeros(acc_ref.shape, F32)

    irow0 = pl.multiple_of(e * n_i, n_i)
    pk = 16
    for c in range(nslab):
        ls = slice(c * LANES, (c + 1) * LANES)
        n_blk = [nsel_ref[h, pl.ds(irow0, n_i), ls] for h in range(PEER_HEADS)]
        ap_blk = [ap_ref[h, pl.ds(irow0, n_i), ls] for h in range(PEER_HEADS)]
        for ii in range(n_i):
            accs = [jnp.zeros((pk, LANES), BF16) for _ in range(nk // pk)]
            for h in range(PEER_HEADS):
                n16 = jnp.broadcast_to(n_blk[h][ii:ii + 1], (pk, LANES)).astype(BF16)
                a16 = jnp.broadcast_to(ap_blk[h][ii:ii + 1], (pk, LANES)).astype(BF16)
                for g in range(nk // pk):
                    rows = slice(g * pk, (g + 1) * pk)
                    hit = rk_ref[h, rows, ls] <= n16
                    accs[g] = accs[g] + jnp.where(hit, bp_ref[h, rows, ls], jnp.zeros((), BF16)) * a16
            for g in range(nk // pk):
                w_ref[ii * nk + g * pk:ii * nk + (g + 1) * pk, ls] = accs[g]

    act = _nt_dot(u_ref[...], h2_ref[...])
    gelu = 0.5 * act * (1.0 + lax.erf(act * (2.0 ** -0.5)))
    wa = (w_ref[...].astype(F32) * gelu).astype(BF16)
    acc_ref[...] += jnp.dot(vt_ref[...], wa, preferred_element_type=F32)

    @pl.when(e == n_e - 1)
    def _finish():
        g2 = mod_ref[0, 5:6, :]
        o_ref[...] = x_ref[...] + g2 * acc_ref[...].T


def _peer_call(x1, mod3, n2, wqt, sk, u, vt, seq):
    t, d = x1.shape
    tt = TT_PEER
    ec = EC_PEER
    n_exp = u.shape[0]
    tiles_per_seq = seq // tt
    tile = lambda i, e: (i, 0)
    const2 = lambda i, e: (0, 0)
    return pl.pallas_call(
        _peer_kernel,
        out_shape=jax.ShapeDtypeStruct((t, d), F32),
        grid=(t // tt, n_exp // ec),
        in_specs=[
            pl.BlockSpec((tt, d), tile),
            pl.BlockSpec((1, N_MOD, d), lambda i, e: (i // tiles_per_seq, 0, 0)),
            pl.BlockSpec((1, d), const2),
            pl.BlockSpec(wqt.shape, const2),
            pl.BlockSpec(sk.shape, lambda i, e: (0, 0, 0, 0)),
            pl.BlockSpec((ec, d), lambda i, e: (e, 0)),
            pl.BlockSpec((d, ec), lambda i, e: (0, e)),
        ],
        out_specs=pl.BlockSpec((tt, d), tile),
        scratch_shapes=[
            pltpu.VMEM((tt, d), BF16),
            pltpu.VMEM((PEER_HEADS, PEER_N_KEYS, tt), F32),
            pltpu.VMEM((PEER_HEADS, PEER_N_KEYS, tt), F32),
            pltpu.VMEM((PEER_HEADS, PEER_N_KEYS, tt), F32),
            pltpu.VMEM((PEER_HEADS, PEER_N_KEYS, tt), BF16),
            pltpu.VMEM((PEER_HEADS, PEER_N_KEYS, tt), F32),
            pltpu.VMEM((PEER_HEADS, PEER_N_KEYS, tt), BF16),
            pltpu.VMEM((ec, tt), BF16),
            pltpu.VMEM((d, tt), F32),
        ],
        compiler_params=pltpu.CompilerParams(
            dimension_semantics=("arbitrary", "arbitrary"), vmem_limit_bytes=VMEM_LIMIT),
        name="peer",
    )(x1, mod3, n2, wqt, sk, u, vt)


def kernel(x, c, w_ada, b_ada, norm1_w, w_in, conv_w, w_conv_out, q_norm_w, k_norm_w,
           w_attn_out, w_o, norm2_w, w_peer_q, peer_sub_keys, peer_u, peer_v):
    b, s, d = x.shape
    depth = w_ada.shape[0]
    cw = conv_w.shape[2]
    aw = N_HEADS * HEAD_DIM
    iw = IDX_HEADS * IDX_DIM
    assert s % BQ == 0 and s % TT_PEER == 0 and s % BT_PROJ == 0
    assert peer_u.shape[1] == PEER_N_KEYS * PEER_N_KEYS

    gidx = jnp.arange(aw) // HEAD_DIM
    gmat = (gidx[:, None] == gidx[None, :]).astype(BF16)
    tri = (jnp.arange(BK)[:, None] >= jnp.arange(BK)[None, :]).astype(BF16)

    xf = x.reshape(b * s, d)
    for layer in range(depth):
        w = w_in[layer]
        o = 0
        wconv = w[:, o:o + 3 * cw].astype(BF16); o += 3 * cw
        wqkv = w[:, o:o + 3 * aw].astype(BF16); o += 3 * aw
        wqi = w[:, o:o + iw].astype(BF16); o += iw
        wkw = jnp.pad(w[:, o:o + IDX_DIM + IDX_HEADS], ((0, 0), (0, LANES - IDX_DIM - IDX_HEADS))).astype(BF16)
        o += IDX_DIM + IDX_HEADS
        wg = w[:, o:o + 2 * d].astype(BF16)
        qn_t = jnp.tile(q_norm_w[layer], N_HEADS).reshape(1, aw)
        kn_t = jnp.tile(k_norm_w[layer], N_HEADS).reshape(1, aw)

        mod3 = _mod_call(c, w_ada[layer], b_ada[layer]).reshape(b, N_MOD, d)
        gyc, sga, qt, k, vt, qit, kw = _inproj_call(
            xf, mod3, norm1_w[layer].reshape(1, d), wconv, wqkv, wqi, wkw, wg, conv_w[layer],
            w_conv_out[layer].astype(BF16), qn_t, kn_t, gmat, s)
        attn = _dsa_call(qt, qit, kw, k, vt, tri, b, s)
        x1 = _post_call(xf, attn, gyc, sga, mod3, w_attn_out[layer].astype(BF16),
                        w_o[layer].astype(BF16), s)
        xf = _peer_call(
            x1, mod3, norm2_w[layer].reshape(1, d), w_peer_q[layer].T.astype(BF16),
            peer_sub_keys[layer].astype(BF16), peer_u[layer].astype(BF16),
            peer_v[layer].T.astype(BF16), s)
    return xf.reshape(b, s, d)
```

```python
import functools

import jax
import jax.numpy as jnp
from jax import lax
from jax.experimental import pallas as pl
from jax.experimental.pallas import tpu as pltpu

F32 = jnp.float32
BF16 = jnp.bfloat16
I32 = jnp.int32

N_HEADS = 8
HEAD_DIM = 64
IDX_HEADS = 8
IDX_DIM = 64
TOPK_ATTN = 256
PEER_HEADS = 8
PEER_N_KEYS = 128
PEER_HALF = 64
PEER_TOPK = 16
CONV_K = 3
N_MOD = 6
EPS = 1e-6

LANES = 128
INT_MIN = -(2 ** 31)
NEG_BIG = -1e30
VMEM_LIMIT = 56 * 1024 * 1024

BT_PROJ = 256
BQ = 256
BK = 256
TT_PEER = 512
EC_PEER = 1024
PEER_SUB = 1


def _split_bf16(a):
    hi = a.astype(BF16)
    lo = (a - hi.astype(F32)).astype(BF16)
    return hi, lo


def _nt_dot(a, b):
    return lax.dot_general(a, b, (((1,), (1,)), ((), ())), preferred_element_type=F32)


def _mod_kernel(c_ref, w_ref, b_ref, o_ref):
    c = c_ref[...]
    s = c * jax.nn.sigmoid(c)
    s_hi, s_lo = _split_bf16(s)
    w_hi, w_lo = _split_bf16(w_ref[...])
    acc = jnp.dot(s_hi, w_hi, preferred_element_type=F32)
    acc += jnp.dot(s_hi, w_lo, preferred_element_type=F32)
    acc += jnp.dot(s_lo, w_hi, preferred_element_type=F32)
    o_ref[...] = acc + b_ref[...]


def _mod_call(c, w_ada, b_ada):
    b, d = c.shape
    n = w_ada.shape[1]
    bn = d
    return pl.pallas_call(
        _mod_kernel,
        out_shape=jax.ShapeDtypeStruct((b, n), F32),
        grid=(n // bn,),
        in_specs=[
            pl.BlockSpec((b, d), lambda j: (0, 0)),
            pl.BlockSpec((d, bn), lambda j: (0, j)),
            pl.BlockSpec((1, bn), lambda j: (0, j)),
        ],
        out_specs=pl.BlockSpec((b, bn), lambda j: (0, j)),
        compiler_params=pltpu.CompilerParams(
            dimension_semantics=("arbitrary",), vmem_limit_bytes=VMEM_LIMIT),
        name="adaln_mod",
    )(c, w_ada, b_ada.reshape(1, n))


def _inproj_kernel(x_ref, xh_ref, mod_ref, n1_ref, wconv_ref, wqkv_ref, wqi_ref, wkw_ref,
                   wg_ref, convw_ref, wco_ref, qn_ref, kn_ref, gmat_ref,
                   gyc_ref, sga_ref, q_ref, k_ref, v_ref, qi_ref, kw_ref, *, tiles_per_seq):
    i = pl.program_id(0)
    cw = convw_ref.shape[1]
    aw = q_ref.shape[1]
    d = x_ref.shape[1]
    bt = x_ref.shape[0]
    sh1 = mod_ref[0, 0:1, :]
    sc1 = mod_ref[0, 1:2, :]

    def norm_mod(xv):
        ms = jnp.mean(xv * xv, axis=-1, keepdims=True)
        y = xv * lax.rsqrt(ms + EPS) * n1_ref[...]
        return (y * (1.0 + sc1) + sh1).astype(BF16)

    h = norm_mod(x_ref[...])
    hh = norm_mod(xh_ref[...])

    pc = jnp.dot(h, wconv_ref[...], preferred_element_type=F32)
    cb, cc, cx = pc[:, :cw], pc[:, cw:2 * cw], pc[:, 2 * cw:]
    u = cc * cx
    ph = jnp.dot(hh, wconv_ref[:, cw:], preferred_element_type=F32)
    uh = ph[:, :cw] * ph[:, cw:]
    uh = jnp.where(i % tiles_per_seq == 0, 0.0, uh)
    row = lax.broadcasted_iota(I32, (bt, cw), 0)
    u1 = jnp.where(row == 0, uh[7:8, :], pltpu.roll(u, 1, 0))
    u2 = pltpu.roll(u, 2, 0)
    u2 = jnp.where(row == 0, uh[6:7, :], jnp.where(row == 1, uh[7:8, :], u2))
    conv = convw_ref[0:1, :] * u2 + convw_ref[1:2, :] * u1 + convw_ref[2:3, :] * u
    yc = jnp.dot((cb * conv).astype(BF16), wco_ref[...], preferred_element_type=F32)

    pg = jnp.dot(h, wg_ref[...], preferred_element_type=F32)
    gyc_ref[...] = jax.nn.sigmoid(pg[:, :d]) * yc
    sga_ref[...] = jax.nn.sigmoid(pg[:, d:])

    pq = jnp.dot(h, wqkv_ref[...], preferred_element_type=F32)
    gmat = gmat_ref[...]

    def head_norm(t, w_row):
        hi, lo = _split_bf16(t * t)
        ss = (jnp.dot(hi, gmat, preferred_element_type=F32)
              + jnp.dot(lo, gmat, preferred_element_type=F32))
        return t * lax.rsqrt(ss * (1.0 / HEAD_DIM) + EPS) * w_row

    qn = head_norm(pq[:, :aw], qn_ref[...])
    q_ref[0] = (qn * (HEAD_DIM ** -0.5)).T.astype(BF16)
    k_ref[...] = head_norm(pq[:, aw:2 * aw], kn_ref[...]).astype(BF16)
    v_ref[0] = pq[:, 2 * aw:].T.astype(BF16)

    qi_ref[0] = jnp.dot(h, wqi_ref[...], preferred_element_type=F32).T.astype(BF16)
    kw_ref[...] = jnp.dot(h, wkw_ref[...], preferred_element_type=F32)


def _inproj_call(xf, mod3, n1, wconv, wqkv, wqi, wkw, wg, convw, wco, qn_t, kn_t, gmat, seq):
    t, d = xf.shape
    bt = BT_PROJ
    tiles_per_seq = seq // bt
    cw = convw.shape[1]
    aw = qn_t.shape[1]
    const = lambda i: (0, 0)
    tile = lambda i: (i, 0)
    tile_t = lambda i: (i, 0, 0)
    halo = lambda i: (jnp.maximum(i * (bt // 8) - 1, 0), 0)
    kern = functools.partial(_inproj_kernel, tiles_per_seq=tiles_per_seq)
    return pl.pallas_call(
        kern,
        out_shape=[
            jax.ShapeDtypeStruct((t, d), F32),
            jax.ShapeDtypeStruct((t, d), F32),
            jax.ShapeDtypeStruct((t // bt, aw, bt), BF16),
            jax.ShapeDtypeStruct((t, aw), BF16),
            jax.ShapeDtypeStruct((t // bt, aw, bt), BF16),
            jax.ShapeDtypeStruct((t // bt, aw, bt), BF16),
            jax.ShapeDtypeStruct((t, LANES), F32),
        ],
        grid=(t // bt,),
        in_specs=[
            pl.BlockSpec((bt, d), tile),
            pl.BlockSpec((8, d), halo),
            pl.BlockSpec((1, N_MOD, d), lambda i: (i // tiles_per_seq, 0, 0)),
            pl.BlockSpec((1, d), const),
            pl.BlockSpec(wconv.shape, const),
            pl.BlockSpec(wqkv.shape, const),
            pl.BlockSpec(wqi.shape, const),
            pl.BlockSpec(wkw.shape, const),
            pl.BlockSpec(wg.shape, const),
            pl.BlockSpec(convw.shape, const),
            pl.BlockSpec(wco.shape, const),
            pl.BlockSpec((1, aw), const),
            pl.BlockSpec((1, aw), const),
            pl.BlockSpec(gmat.shape, const),
        ],
        out_specs=[
            pl.BlockSpec((bt, d), tile),
            pl.BlockSpec((bt, d), tile),
            pl.BlockSpec((1, aw, bt), tile_t),
            pl.BlockSpec((bt, aw), tile),
            pl.BlockSpec((1, aw, bt), tile_t),
            pl.BlockSpec((1, aw, bt), tile_t),
            pl.BlockSpec((bt, LANES), tile),
        ],
        compiler_params=pltpu.CompilerParams(
            dimension_semantics=("arbitrary",), vmem_limit_bytes=VMEM_LIMIT),
        name="inproj",
    )(xf, xf, mod3, n1, wconv, wqkv, wqi, wkw, wg, convw, wco, qn_t, kn_t, gmat)


def _tree_sum(parts):
    parts = list(parts)
    while len(parts) > 1:
        nxt = [parts[a] + parts[a + 1] for a in range(0, len(parts) - 1, 2)]
        if len(parts) % 2:
            nxt.append(parts[-1])
        parts = nxt
    return parts[0]


def _dsa_kernel(qt_ref, qit_ref, kwq_ref, k_ref, vt_ref, kwk_ref, tri_ref, o_ref,
                keys_ref, bias_ref, qz_ref, qiz_ref, s_ref, m_ref, l_ref, acc_ref, *, n_sel):
    i = pl.program_id(1)
    bq = qt_ref.shape[2]
    bk = BK
    n_tiles = (i * bq) // bk + bq // bk
    ngrp = bk // 8

    zpad = jnp.zeros((LANES - HEAD_DIM, bq), BF16)
    for h in range(N_HEADS):
        qh = qt_ref[0, h * HEAD_DIM:(h + 1) * HEAD_DIM, :]
        qz_ref[h] = jnp.concatenate([qh, zpad] if h % 2 == 0 else [zpad, qh], axis=0)
        qiz_ref[h] = jnp.concatenate([qit_ref[0, h * IDX_DIM:(h + 1) * IDX_DIM, :], zpad], axis=0)
    wt = kwq_ref[...].T

    q_pos = i * bq + lax.broadcasted_iota(I32, (bk, bq), 1)

    def score_tile(j, carry):
        kw = kwk_ref[pl.ds(pl.multiple_of(j * bk, bk), bk), :].astype(BF16)
        score = jnp.zeros((bk, bq), F32)
        for h in range(IDX_HEADS):
            logits = jnp.dot(kw, qiz_ref[h], preferred_element_type=F32)
            score = score + wt[IDX_DIM + h:IDX_DIM + h + 1, :] * jnp.maximum(logits, 0.0)
        bits = pltpu.bitcast(score, I32)
        key = jnp.where(bits < 0, bits ^ jnp.int32(0x7FFFFFFF), bits)
        key = jnp.where(bits == jnp.int32(INT_MIN), 0, key)
        key_pos = j * bk + lax.broadcasted_iota(I32, (bk, bq), 0)
        keys_ref[j] = jnp.where(key_pos <= q_pos, key, jnp.int32(INT_MIN))
        return carry

    lax.fori_loop(0, n_tiles, score_tile, 0)

    def count_ge(cand, strict):
        cand_b = jnp.broadcast_to(cand, (8, bq))

        def body(j, cnt):
            hits = []
            for g in range(ngrp):
                ks = keys_ref[j, g * 8:(g + 1) * 8, :]
                hit = (ks > cand_b) if strict else (ks >= cand_b)
                hits.append(jnp.where(hit, 1, 0))
            return cnt + _tree_sum(hits)

        cnt = lax.fori_loop(0, n_tiles, body, jnp.zeros((8, bq), I32))
        return jnp.sum(cnt, axis=0, keepdims=True)

    def bit_step(b, t_u):
        cand_u = t_u | lax.shift_left(jnp.int32(1), 31 - b)
        total = count_ge(cand_u ^ jnp.int32(INT_MIN), False)
        return jnp.where(total >= n_sel, cand_u, t_u)

    t_u = lax.fori_loop(0, 32, bit_step, jnp.zeros((1, bq), I32))
    thr = t_u ^ jnp.int32(INT_MIN)
    n_gt = count_ge(thr, True)
    need = jnp.where(thr == jnp.int32(INT_MIN), 0, n_sel - n_gt).astype(F32)

    tri = tri_ref[...]

    def bias_tile(j, carry):
        kt = keys_ref[j]
        eq = kt == thr
        eq_f = jnp.where(eq, 1.0, 0.0)
        prefix = jnp.dot(tri, eq_f.astype(BF16), preferred_element_type=F32) + carry
        sel = (kt > thr) | (eq & (prefix <= need))
        bias_ref[j] = jnp.where(sel, 0.0, NEG_BIG)
        return prefix[bk - 1:bk, :]

    lax.fori_loop(0, n_tiles, bias_tile, jnp.zeros((1, bq), F32))

    m_ref[...] = jnp.full(m_ref.shape, NEG_BIG, F32)
    l_ref[...] = jnp.zeros(l_ref.shape, F32)
    acc_ref[...] = jnp.zeros(acc_ref.shape, F32)

    def attn_tile(j, carry):
        kt = k_ref[pl.ds(pl.multiple_of(j * bk, bk), bk), :]
        bias = bias_ref[j]
        alphas = []
        for h in range(N_HEADS):
            pair = slice((h // 2) * LANES, (h // 2 + 1) * LANES)
            s = jnp.dot(kt[:, pair], qz_ref[h], preferred_element_type=F32) + bias
            s_ref[h] = s
            m_prev = m_ref[h]
            m_new = jnp.maximum(m_prev, jnp.broadcast_to(jnp.max(s, axis=0, keepdims=True), (8, bq)))
            m_ref[h] = m_new
            alphas.append(jnp.exp(m_prev - m_new))
        for h in range(N_HEADS):
            p = jnp.exp(s_ref[h] - m_ref[h, 0:1, :])
            l_ref[h] = alphas[h] * l_ref[h] + _tree_sum([p[g * 8:(g + 1) * 8] for g in range(ngrp)])
            pv = jnp.dot(vt_ref[j, h * HEAD_DIM:(h + 1) * HEAD_DIM, :], p.astype(BF16),
                         preferred_element_type=F32)
            acc_ref[h] = alphas[h][0:1] * acc_ref[h] + pv
        return carry

    lax.fori_loop(0, n_tiles, attn_tile, 0)

    outs = [acc_ref[h] / jnp.sum(l_ref[h], axis=0, keepdims=True) for h in range(N_HEADS)]
    o_ref[...] = jnp.concatenate(outs, axis=0).T.astype(o_ref.dtype)


def _dsa_call(qt, qit, kw, k, vt, tri, batch, seq):
    nt, aw, bq = qt.shape
    assert bq == BQ and bq == BK
    nq = seq // bq
    n_sel = min(TOPK_ATTN, seq // 4)
    qtile = lambda b, i: (b * nq + i, 0)
    qtile_t = lambda b, i: (b * nq + i, 0, 0)
    whole = lambda b, i: (b, 0)
    kern = functools.partial(_dsa_kernel, n_sel=n_sel)
    return pl.pallas_call(
        kern,
        out_shape=jax.ShapeDtypeStruct((batch * seq, aw), BF16),
        grid=(batch, nq),
        in_specs=[
            pl.BlockSpec((1, aw, bq), qtile_t),
            pl.BlockSpec((1, aw, bq), qtile_t),
            pl.BlockSpec((bq, LANES), qtile),
            pl.BlockSpec((seq, aw), whole),
            pl.BlockSpec((seq // BK, aw, BK), lambda b, i: (b, 0, 0)),
            pl.BlockSpec((seq, LANES), whole),
            pl.BlockSpec(tri.shape, lambda b, i: (0, 0)),
        ],
        out_specs=pl.BlockSpec((bq, aw), qtile),
        scratch_shapes=[
            pltpu.VMEM((seq // BK, BK, bq), I32),
            pltpu.VMEM((seq // BK, BK, bq), F32),
            pltpu.VMEM((N_HEADS, LANES, bq), BF16),
            pltpu.VMEM((IDX_HEADS, LANES, bq), BF16),
            pltpu.VMEM((N_HEADS, BK, bq), F32),
            pltpu.VMEM((N_HEADS, 8, bq), F32),
            pltpu.VMEM((N_HEADS, 8, bq), F32),
            pltpu.VMEM((N_HEADS, HEAD_DIM, bq), F32),
        ],
        compiler_params=pltpu.CompilerParams(
            dimension_semantics=("arbitrary", "arbitrary"), vmem_limit_bytes=VMEM_LIMIT),
        name="dsa_attention",
    )(qt, qit, kw, k, vt, kw, tri)


def _post_kernel(x_ref, attn_ref, gyc_ref, sga_ref, mod_ref, wao_ref, wo_ref, o_ref):
    g1 = mod_ref[0, 2:3, :]
    y_attn = jnp.dot(attn_ref[...], wao_ref[...], preferred_element_type=F32)
    mix = gyc_ref[...] + sga_ref[...] * y_attn
    delta = jnp.dot(mix.astype(BF16), wo_ref[...], preferred_element_type=F32)
    o_ref[...] = x_ref[...] + g1 * delta


def _post_call(xf, attn, gyc, sga, mod3, wao, wo, seq):
    t, d = xf.shape
    bt = BT_PROJ
    tiles_per_seq = seq // bt
    aw = attn.shape[1]
    tile = lambda i: (i, 0)
    const = lambda i: (0, 0)
    return pl.pallas_call(
        _post_kernel,
        out_shape=jax.ShapeDtypeStruct((t, d), F32),
        grid=(t // bt,),
        in_specs=[
            pl.BlockSpec((bt, d), tile),
            pl.BlockSpec((bt, aw), tile),
            pl.BlockSpec((bt, d), tile),
            pl.BlockSpec((bt, d), tile),
            pl.BlockSpec((1, N_MOD, d), lambda i: (i // tiles_per_seq, 0, 0)),
            pl.BlockSpec(wao.shape, const),
            pl.BlockSpec(wo.shape, const),
        ],
        out_specs=pl.BlockSpec((bt, d), tile),
        compiler_params=pltpu.CompilerParams(
            dimension_semantics=("arbitrary",), vmem_limit_bytes=VMEM_LIMIT),
        name="post_mix",
    )(xf, attn, gyc, sga, mod3, wao, wo)


def _col_max(a):
    return jnp.max(a, axis=0, keepdims=True)


def _batcher_pairs(n):
    pairs = []
    p = 1
    while p < n:
        k = p
        while k >= 1:
            for j in range(k % p, n - k, 2 * k):
                for i in range(min(k, n - j - k)):
                    if (i + j) // (2 * p) == (i + j + k) // (2 * p):
                        pairs.append((i + j, i + j + k))
            k //= 2
        p *= 2
    return pairs


_SORT16 = _batcher_pairs(PEER_TOPK)
_BITONIC16 = [(k, k + d) for d in (8, 4, 2, 1) for k in range(PEER_TOPK) if not k & d]


def _sort_desc(v, pairs):
    v = list(v)
    for i, j in pairs:
        v[i], v[j] = jnp.maximum(v[i], v[j]), jnp.minimum(v[i], v[j])
    return v


def _peer_kernel(x_ref, mod_ref, n2_ref, wqt_ref, sk_ref, u_ref, vt_ref, o_ref,
                 h2_ref, s1_ref, s2_ref, ap_ref, bp_ref, nsel_ref, rk_ref, w_ref, acc_ref):
    e = pl.program_id(1)
    n_e = pl.num_programs(1)
    tt = x_ref.shape[0]
    nk = PEER_N_KEYS
    n_i = u_ref.shape[0] // nk
    nslab = tt // LANES

    @pl.when(e == 0)
    def _route():
        xv = x_ref[...]
        sh2 = mod_ref[0, 3:4, :]
        sc2 = mod_ref[0, 4:5, :]
        ms = jnp.mean(xv * xv, axis=-1, keepdims=True)
        h2 = ((xv * lax.rsqrt(ms + EPS) * n2_ref[...]) * (1.0 + sc2) + sh2).astype(BF16)
        h2_ref[...] = h2
        qt = _nt_dot(wqt_ref[...], h2).astype(BF16)
        for h in range(PEER_HEADS):
            for p, dst in ((0, s1_ref), (1, s2_ref)):
                r0 = (h * 2 + p) * PEER_HALF
                dst[h] = jnp.dot(sk_ref[h, p], qt[r0:r0 + PEER_HALF, :],
                                 preferred_element_type=F32)

        neg_inf = jnp.float32(-jnp.inf)
        sub_id = lax.broadcasted_iota(I32, (8, LANES), 0)

        def top16(s):
            v = _sort_desc([s[g * 8:(g + 1) * 8] for g in range(nk // 8)], _SORT16)
            for shift in (4, 2, 1):
                v = [jnp.maximum(v[r], pltpu.roll(v[PEER_TOPK - 1 - r], shift, 0)) for r in range(PEER_TOPK)]
                v = _sort_desc(v, _BITONIC16)
            return v

        def stack8(vals):
            out = vals[0]
            for r in range(1, 8):
                out = jnp.where(sub_id == r, vals[r], out)
            return out

        def route_head(h, carry):
            for c in range(nslab):
                ls = slice(c * LANES, (c + 1) * LANES)
                s1 = s1_ref[h, :, ls]
                s2 = s2_ref[h, :, ls]
                a = top16(s1)
                b = top16(s2)
                b_lo, b_hi = stack8(b[:8]), stack8(b[8:])
                cands = [a[0] + b_lo, a[0] + b_hi] + [a[r] + b_lo for r in range(1, PEER_TOPK)]
                work, t = cands, []
                for _ in range(PEER_TOPK):
                    m = _col_max(functools.reduce(jnp.maximum, work))
                    t.append(m)
                    work = [jnp.where(w == m, neg_inf, w) for w in work]
                thr = t[PEER_TOPK - 1]
                z = functools.reduce(jnp.add, [jnp.exp(tk - t[0]) for tk in t])
                hits = [jnp.where(cd >= thr, 1.0, 0.0) for cd in cands]
                n_sel = [jnp.sum(hits[0] + hits[1], axis=0, keepdims=True)]
                n_sel += [jnp.sum(hits[r + 1], axis=0, keepdims=True) for r in range(1, PEER_TOPK)]
                n_b = [jnp.broadcast_to(n, (8, LANES)) for n in n_sel]
                cnts, ranks = [], []
                for g in range(nk // 8):
                    rows = slice(g * 8, (g + 1) * 8)
                    s1g, s2g = s1[rows], s2[rows]
                    cnt = jnp.zeros((8, LANES), F32)
                    rank = jnp.ones((8, LANES), F32)
                    for r in range(PEER_TOPK):
                        cnt = jnp.where(s1g == a[r], n_b[r], cnt)
                        rank = rank + jnp.where(b[r] > s2g, 1.0, 0.0)
                    cnts.append(cnt)
                    ranks.append(rank)
                nsel_ref[h, :, ls] = jnp.concatenate(cnts, axis=0)
                rk_ref[h, :, ls] = jnp.concatenate(ranks, axis=0).astype(BF16)
                ap_ref[h, :, ls] = jnp.exp(s1 - a[0][0:1])
                bp_ref[h, :, ls] = (jnp.exp(s2 - b[0][0:1]) / z).astype(BF16)
            return carry

        lax.fori_loop(0, PEER_HEADS, route_head, 0)
        acc_ref[...] = jnp.zeros(acc_ref.shape, F32)

    irow0 = pl.multiple_of(e * n_i, n_i)
    pk = 16
    n_blk = [[nsel_ref[h, pl.ds(irow0, n_i), c * LANES:(c + 1) * LANES] for h in range(PEER_HEADS)]
             for c in range(nslab)]
    ap_blk = [[ap_ref[h, pl.ds(irow0, n_i), c * LANES:(c + 1) * LANES] for h in range(PEER_HEADS)]
              for c in range(nslab)]
    sub_i = n_i // PEER_SUB
    partial = None
    for sc in range(PEER_SUB):
        for ii in range(sc * sub_i, (sc + 1) * sub_i):
            for c in range(nslab):
                ls = slice(c * LANES, (c + 1) * LANES)
                accs = [None] * (nk // pk)
                for h in range(PEER_HEADS):
                    n16 = jnp.broadcast_to(n_blk[c][h][ii:ii + 1], (pk, LANES)).astype(BF16)
                    a16 = jnp.broadcast_to(ap_blk[c][h][ii:ii + 1], (pk, LANES)).astype(BF16)
                    for g in range(nk // pk):
                        rows = slice(g * pk, (g + 1) * pk)
                        hit = rk_ref[h, rows, ls] <= n16
                        term = jnp.where(hit, bp_ref[h, rows, ls], jnp.zeros((), BF16)) * a16
                        accs[g] = term if h == 0 else accs[g] + term
                for g in range(nk // pk):
                    w_ref[ii * nk + g * pk:ii * nk + (g + 1) * pk, ls] = accs[g]
        rows = slice(sc * sub_i * nk, (sc + 1) * sub_i * nk)
        act = _nt_dot(u_ref[rows, :], h2_ref[...])
        gelu = 0.5 * act * (1.0 + lax.erf(act * (2.0 ** -0.5)))
        wa = (w_ref[rows, :].astype(F32) * gelu).astype(BF16)
        part = jnp.dot(vt_ref[0, :, rows], wa, preferred_element_type=F32)
        partial = part if partial is None else partial + part
    acc_ref[...] += partial

    @pl.when(e == n_e - 1)
    def _finish():
        g2 = mod_ref[0, 5:6, :]
        o_ref[...] = x_ref[...] + g2 * acc_ref[...].T


def _peer_call(x1, mod3, n2, wqt, sk, u, vt, seq):
    t, d = x1.shape
    tt = TT_PEER
    ec = EC_PEER
    n_exp = u.shape[0]
    tiles_per_seq = seq // tt
    tile = lambda i, e: (i, 0)
    const2 = lambda i, e: (0, 0)
    return pl.pallas_call(
        _peer_kernel,
        out_shape=jax.ShapeDtypeStruct((t, d), F32),
        grid=(t // tt, n_exp // ec),
        in_specs=[
            pl.BlockSpec((tt, d), tile),
            pl.BlockSpec((1, N_MOD, d), lambda i, e: (i // tiles_per_seq, 0, 0)),
            pl.BlockSpec((1, d), const2),
            pl.BlockSpec(wqt.shape, const2),
            pl.BlockSpec(sk.shape, lambda i, e: (0, 0, 0, 0)),
            pl.BlockSpec((ec, d), lambda i, e: (e, 0)),
            pl.BlockSpec((1, d, ec), lambda i, e: (e, 0, 0)),
        ],
        out_specs=pl.BlockSpec((tt, d), tile),
        scratch_shapes=[
            pltpu.VMEM((tt, d), BF16),
            pltpu.VMEM((PEER_HEADS, PEER_N_KEYS, tt), F32),
            pltpu.VMEM((PEER_HEADS, PEER_N_KEYS, tt), F32),
            pltpu.VMEM((PEER_HEADS, PEER_N_KEYS, tt), F32),
            pltpu.VMEM((PEER_HEADS, PEER_N_KEYS, tt), BF16),
            pltpu.VMEM((PEER_HEADS, PEER_N_KEYS, tt), F32),
            pltpu.VMEM((PEER_HEADS, PEER_N_KEYS, tt), BF16),
            pltpu.VMEM((ec, tt), BF16),
            pltpu.VMEM((d, tt), F32),
        ],
        compiler_params=pltpu.CompilerParams(
            dimension_semantics=("arbitrary", "arbitrary"), vmem_limit_bytes=VMEM_LIMIT),
        name="peer",
    )(x1, mod3, n2, wqt, sk, u, vt)


def kernel(x, c, w_ada, b_ada, norm1_w, w_in, conv_w, w_conv_out, q_norm_w, k_norm_w,
           w_attn_out, w_o, norm2_w, w_peer_q, peer_sub_keys, peer_u, peer_v):
    b, s, d = x.shape
    depth = w_ada.shape[0]
    cw = conv_w.shape[2]
    aw = N_HEADS * HEAD_DIM
    iw = IDX_HEADS * IDX_DIM
    assert s % BQ == 0 and s % TT_PEER == 0 and s % BT_PROJ == 0
    assert peer_u.shape[1] == PEER_N_KEYS * PEER_N_KEYS

    gidx = jnp.arange(aw) // HEAD_DIM
    gmat = (gidx[:, None] == gidx[None, :]).astype(BF16)
    tri = (jnp.arange(BK)[:, None] >= jnp.arange(BK)[None, :]).astype(BF16)

    xf = x.reshape(b * s, d)
    for layer in range(depth):
        w = w_in[layer]
        o = 0
        wconv = w[:, o:o + 3 * cw].astype(BF16); o += 3 * cw
        wqkv = w[:, o:o + 3 * aw].astype(BF16); o += 3 * aw
        wqi = w[:, o:o + iw].astype(BF16); o += iw
        wkw = jnp.pad(w[:, o:o + IDX_DIM + IDX_HEADS], ((0, 0), (0, LANES - IDX_DIM - IDX_HEADS))).astype(BF16)
        o += IDX_DIM + IDX_HEADS
        wg = w[:, o:o + 2 * d].astype(BF16)
        qn_t = jnp.tile(q_norm_w[layer], N_HEADS).reshape(1, aw)
        kn_t = jnp.tile(k_norm_w[layer], N_HEADS).reshape(1, aw)

        mod3 = _mod_call(c, w_ada[layer], b_ada[layer]).reshape(b, N_MOD, d)
        gyc, sga, qt, k, vt, qit, kw = _inproj_call(
            xf, mod3, norm1_w[layer].reshape(1, d), wconv, wqkv, wqi, wkw, wg, conv_w[layer],
            w_conv_out[layer].astype(BF16), qn_t, kn_t, gmat, s)
        attn = _dsa_call(qt, qit, kw, k, vt, tri, b, s)
        x1 = _post_call(xf, attn, gyc, sga, mod3, w_attn_out[layer].astype(BF16),
                        w_o[layer].astype(BF16), s)
        xf = _peer_call(
            x1, mod3, norm2_w[layer].reshape(1, d), w_peer_q[layer].T.astype(BF16),
            peer_sub_keys[layer].astype(BF16), peer_u[layer].astype(BF16),
            peer_v[layer].astype(BF16).reshape(-1, EC_PEER, d).transpose(0, 2, 1), s)
    return xf.reshape(b, s, d)
```

```python
import functools

import jax
import jax.numpy as jnp
from jax import lax
from jax.experimental import pallas as pl
from jax.experimental.pallas import tpu as pltpu

F32 = jnp.float32
BF16 = jnp.bfloat16
I32 = jnp.int32

N_HEADS = 8
HEAD_DIM = 64
IDX_HEADS = 8
IDX_DIM = 64
TOPK_ATTN = 256
PEER_HEADS = 8
PEER_N_KEYS = 128
PEER_HALF = 64
PEER_TOPK = 16
CONV_K = 3
N_MOD = 6
EPS = 1e-6

LANES = 128
INT_MIN = -(2 ** 31)
NEG_BIG = -1e30
VMEM_LIMIT = 56 * 1024 * 1024

BT_PROJ = 256
BQ = 256
BK = 256
TT_PEER = 512
EC_PEER = 1024
PEER_SUB = 1


def _split_bf16(a):
    hi = a.astype(BF16)
    lo = (a - hi.astype(F32)).astype(BF16)
    return hi, lo


def _nt_dot(a, b):
    return lax.dot_general(a, b, (((1,), (1,)), ((), ())), preferred_element_type=F32)


def _mod_kernel(c_ref, w_ref, b_ref, o_ref):
    c = c_ref[...]
    s = c * jax.nn.sigmoid(c)
    s_hi, s_lo = _split_bf16(s)
    w_hi, w_lo = _split_bf16(w_ref[...])
    acc = jnp.dot(s_hi, w_hi, preferred_element_type=F32)
    acc += jnp.dot(s_hi, w_lo, preferred_element_type=F32)
    acc += jnp.dot(s_lo, w_hi, preferred_element_type=F32)
    o_ref[...] = acc + b_ref[...]


def _mod_call(c, w_ada, b_ada):
    b, d = c.shape
    n = w_ada.shape[1]
    bn = d
    return pl.pallas_call(
        _mod_kernel,
        out_shape=jax.ShapeDtypeStruct((b, n), F32),
        grid=(n // bn,),
        in_specs=[
            pl.BlockSpec((b, d), lambda j: (0, 0)),
            pl.BlockSpec((d, bn), lambda j: (0, j)),
            pl.BlockSpec((1, bn), lambda j: (0, j)),
        ],
        out_specs=pl.BlockSpec((b, bn), lambda j: (0, j)),
        compiler_params=pltpu.CompilerParams(
            dimension_semantics=("arbitrary",), vmem_limit_bytes=VMEM_LIMIT),
        name="adaln_mod",
    )(c, w_ada, b_ada.reshape(1, n))


def _inproj_kernel(x_ref, xh_ref, mod_ref, n1_ref, wconv_ref, wqkv_ref, wqi_ref, wkw_ref,
                   wg_ref, convw_ref, wco_ref, qn_ref, kn_ref, gmat_ref,
                   gyc_ref, sga_ref, q_ref, k_ref, v_ref, qi_ref, kw_ref, *, tiles_per_seq):
    i = pl.program_id(0)
    cw = convw_ref.shape[1]
    aw = q_ref.shape[1]
    d = x_ref.shape[1]
    bt = x_ref.shape[0]
    sh1 = mod_ref[0, 0:1, :]
    sc1 = mod_ref[0, 1:2, :]

    def norm_mod(xv):
        ms = jnp.mean(xv * xv, axis=-1, keepdims=True)
        y = xv * lax.rsqrt(ms + EPS) * n1_ref[...]
        return (y * (1.0 + sc1) + sh1).astype(BF16)

    h = norm_mod(x_ref[...])
    hh = norm_mod(xh_ref[...])

    pc = jnp.dot(h, wconv_ref[...], preferred_element_type=F32)
    cb, cc, cx = pc[:, :cw], pc[:, cw:2 * cw], pc[:, 2 * cw:]
    u = cc * cx
    ph = jnp.dot(hh, wconv_ref[:, cw:], preferred_element_type=F32)
    uh = ph[:, :cw] * ph[:, cw:]
    uh = jnp.where(i % tiles_per_seq == 0, 0.0, uh)
    row = lax.broadcasted_iota(I32, (bt, cw), 0)
    u1 = jnp.where(row == 0, uh[7:8, :], pltpu.roll(u, 1, 0))
    u2 = pltpu.roll(u, 2, 0)
    u2 = jnp.where(row == 0, uh[6:7, :], jnp.where(row == 1, uh[7:8, :], u2))
    conv = convw_ref[0:1, :] * u2 + convw_ref[1:2, :] * u1 + convw_ref[2:3, :] * u
    yc = jnp.dot((cb * conv).astype(BF16), wco_ref[...], preferred_element_type=F32)

    pg = jnp.dot(h, wg_ref[...], preferred_element_type=F32)
    gyc_ref[...] = jax.nn.sigmoid(pg[:, :d]) * yc
    sga_ref[...] = jax.nn.sigmoid(pg[:, d:])

    pq = jnp.dot(h, wqkv_ref[...], preferred_element_type=F32)
    gmat = gmat_ref[...]

    def head_norm(t, w_row):
        hi, lo = _split_bf16(t * t)
        ss = (jnp.dot(hi, gmat, preferred_element_type=F32)
              + jnp.dot(lo, gmat, preferred_element_type=F32))
        return t * lax.rsqrt(ss * (1.0 / HEAD_DIM) + EPS) * w_row

    qn = head_norm(pq[:, :aw], qn_ref[...])
    q_ref[0] = (qn * (HEAD_DIM ** -0.5)).T.astype(BF16)
    k_ref[...] = head_norm(pq[:, aw:2 * aw], kn_ref[...]).astype(BF16)
    v_ref[0] = pq[:, 2 * aw:].T.astype(BF16)

    qi_ref[0] = jnp.dot(h, wqi_ref[...], preferred_element_type=F32).T.astype(BF16)
    kw_ref[...] = jnp.dot(h, wkw_ref[...], preferred_element_type=F32)


def _inproj_call(xf, mod3, n1, wconv, wqkv, wqi, wkw, wg, convw, wco, qn_t, kn_t, gmat, seq):
    t, d = xf.shape
    bt = BT_PROJ
    tiles_per_seq = seq // bt
    cw = convw.shape[1]
    aw = qn_t.shape[1]
    const = lambda i: (0, 0)
    tile = lambda i: (i, 0)
    tile_t = lambda i: (i, 0, 0)
    halo = lambda i: (jnp.maximum(i * (bt // 8) - 1, 0), 0)
    kern = functools.partial(_inproj_kernel, tiles_per_seq=tiles_per_seq)
    return pl.pallas_call(
        kern,
        out_shape=[
            jax.ShapeDtypeStruct((t, d), F32),
            jax.ShapeDtypeStruct((t, d), F32),
            jax.ShapeDtypeStruct((t // bt, aw, bt), BF16),
            jax.ShapeDtypeStruct((t, aw), BF16),
            jax.ShapeDtypeStruct((t // bt, aw, bt), BF16),
            jax.ShapeDtypeStruct((t // bt, aw, bt), BF16),
            jax.ShapeDtypeStruct((t, LANES), F32),
        ],
        grid=(t // bt,),
        in_specs=[
            pl.BlockSpec((bt, d), tile),
            pl.BlockSpec((8, d), halo),
            pl.BlockSpec((1, N_MOD, d), lambda i: (i // tiles_per_seq, 0, 0)),
            pl.BlockSpec((1, d), const),
            pl.BlockSpec(wconv.shape, const),
            pl.BlockSpec(wqkv.shape, const),
            pl.BlockSpec(wqi.shape, const),
            pl.BlockSpec(wkw.shape, const),
            pl.BlockSpec(wg.shape, const),
            pl.BlockSpec(convw.shape, const),
            pl.BlockSpec(wco.shape, const),
            pl.BlockSpec((1, aw), const),
            pl.BlockSpec((1, aw), const),
            pl.BlockSpec(gmat.shape, const),
        ],
        out_specs=[
            pl.BlockSpec((bt, d), tile),
            pl.BlockSpec((bt, d), tile),
            pl.BlockSpec((1, aw, bt), tile_t),
            pl.BlockSpec((bt, aw), tile),
            pl.BlockSpec((1, aw, bt), tile_t),
            pl.BlockSpec((1, aw, bt), tile_t),
            pl.BlockSpec((bt, LANES), tile),
        ],
        compiler_params=pltpu.CompilerParams(
            dimension_semantics=("arbitrary",), vmem_limit_bytes=VMEM_LIMIT),
        name="inproj",
    )(xf, xf, mod3, n1, wconv, wqkv, wqi, wkw, wg, convw, wco, qn_t, kn_t, gmat)


def _tree_sum(parts):
    parts = list(parts)
    while len(parts) > 1:
        nxt = [parts[a] + parts[a + 1] for a in range(0, len(parts) - 1, 2)]
        if len(parts) % 2:
            nxt.append(parts[-1])
        parts = nxt
    return parts[0]


def _dsa_kernel(qt_ref, qit_ref, kwq_ref, k_ref, vt_ref, kwk_ref, tri_ref, o_ref,
                keys_ref, bias_ref, qz_ref, qiz_ref, s_ref, m_ref, l_ref, acc_ref, *, n_sel):
    i = pl.program_id(1)
    bq = qt_ref.shape[2]
    bk = BK
    n_tiles = (i * bq) // bk + bq // bk
    ngrp = bk // 8

    zpad = jnp.zeros((LANES - HEAD_DIM, bq), BF16)
    for h in range(N_HEADS):
        qh = qt_ref[0, h * HEAD_DIM:(h + 1) * HEAD_DIM, :]
        qz_ref[h] = jnp.concatenate([qh, zpad] if h % 2 == 0 else [zpad, qh], axis=0)
        qiz_ref[h] = jnp.concatenate([qit_ref[0, h * IDX_DIM:(h + 1) * IDX_DIM, :], zpad], axis=0)
    wt = kwq_ref[...].T

    q_pos = i * bq + lax.broadcasted_iota(I32, (bk, bq), 1)

    def score_tile(j, carry):
        kw = kwk_ref[pl.ds(pl.multiple_of(j * bk, bk), bk), :].astype(BF16)
        score = jnp.zeros((bk, bq), F32)
        for h in range(IDX_HEADS):
            logits = jnp.dot(kw, qiz_ref[h], preferred_element_type=F32)
            score = score + wt[IDX_DIM + h:IDX_DIM + h + 1, :] * jnp.maximum(logits, 0.0)
        bits = pltpu.bitcast(score, I32)
        key = jnp.where(bits < 0, bits ^ jnp.int32(0x7FFFFFFF), bits)
        key = jnp.where(bits == jnp.int32(INT_MIN), 0, key)
        key_pos = j * bk + lax.broadcasted_iota(I32, (bk, bq), 0)
        keys_ref[j] = jnp.where(key_pos <= q_pos, key, jnp.int32(INT_MIN))
        return carry

    lax.fori_loop(0, n_tiles, score_tile, 0)

    def count_ge(cand, strict):
        cand_b = jnp.broadcast_to(cand, (8, bq))

        def body(j, cnt):
            hits = []
            for g in range(ngrp):
                ks = keys_ref[j, g * 8:(g + 1) * 8, :]
                hit = (ks > cand_b) if strict else (ks >= cand_b)
                hits.append(jnp.where(hit, 1, 0))
            return cnt + _tree_sum(hits)

        cnt = lax.fori_loop(0, n_tiles, body, jnp.zeros((8, bq), I32))
        return jnp.sum(cnt, axis=0, keepdims=True)

    def bit_step(b, t_u):
        cand_u = t_u | lax.shift_left(jnp.int32(1), 31 - b)
        total = count_ge(cand_u ^ jnp.int32(INT_MIN), False)
        return jnp.where(total >= n_sel, cand_u, t_u)

    t_u = lax.fori_loop(0, 32, bit_step, jnp.zeros((1, bq), I32))
    thr = t_u ^ jnp.int32(INT_MIN)
    n_gt = count_ge(thr, True)
    need = jnp.where(thr == jnp.int32(INT_MIN), 0, n_sel - n_gt).astype(F32)

    tri = tri_ref[...]

    def bias_tile(j, carry):
        kt = keys_ref[j]
        eq = kt == thr
        eq_f = jnp.where(eq, 1.0, 0.0)
        prefix = jnp.dot(tri, eq_f.astype(BF16), preferred_element_type=F32) + carry
        sel = (kt > thr) | (eq & (prefix <= need))
        bias_ref[j] = jnp.where(sel, 0.0, NEG_BIG)
        return prefix[bk - 1:bk, :]

    lax.fori_loop(0, n_tiles, bias_tile, jnp.zeros((1, bq), F32))

    m_ref[...] = jnp.full(m_ref.shape, NEG_BIG, F32)
    l_ref[...] = jnp.zeros(l_ref.shape, F32)
    acc_ref[...] = jnp.zeros(acc_ref.shape, F32)

    def attn_tile(j, carry):
        kt = k_ref[pl.ds(pl.multiple_of(j * bk, bk), bk), :]
        bias = bias_ref[j]
        alphas = []
        for h in range(N_HEADS):
            pair = slice((h // 2) * LANES, (h // 2 + 1) * LANES)
            s = jnp.dot(kt[:, pair], qz_ref[h], preferred_element_type=F32) + bias
            s_ref[h] = s
            m_prev = m_ref[h]
            m_new = jnp.maximum(m_prev, jnp.broadcast_to(jnp.max(s, axis=0, keepdims=True), (8, bq)))
            m_ref[h] = m_new
            alphas.append(jnp.exp(m_prev - m_new))
        for h in range(N_HEADS):
            p = jnp.exp(s_ref[h] - m_ref[h, 0:1, :])
            l_ref[h] = alphas[h] * l_ref[h] + _tree_sum([p[g * 8:(g + 1) * 8] for g in range(ngrp)])
            pv = jnp.dot(vt_ref[j, h * HEAD_DIM:(h + 1) * HEAD_DIM, :], p.astype(BF16),
                         preferred_element_type=F32)
            acc_ref[h] = alphas[h][0:1] * acc_ref[h] + pv
        return carry

    lax.fori_loop(0, n_tiles, attn_tile, 0)

    outs = [acc_ref[h] / jnp.sum(l_ref[h], axis=0, keepdims=True) for h in range(N_HEADS)]
    o_ref[...] = jnp.concatenate(outs, axis=0).T.astype(o_ref.dtype)


def _dsa_call(qt, qit, kw, k, vt, tri, batch, seq):
    nt, aw, bq = qt.shape
    assert bq == BQ and bq == BK
    nq = seq // bq
    n_sel = min(TOPK_ATTN, seq // 4)
    qtile = lambda b, i: (b * nq + i, 0)
    qtile_t = lambda b, i: (b * nq + i, 0, 0)
    whole = lambda b, i: (b, 0)
    kern = functools.partial(_dsa_kernel, n_sel=n_sel)
    return pl.pallas_call(
        kern,
        out_shape=jax.ShapeDtypeStruct((batch * seq, aw), BF16),
        grid=(batch, nq),
        in_specs=[
            pl.BlockSpec((1, aw, bq), qtile_t),
            pl.BlockSpec((1, aw, bq), qtile_t),
            pl.BlockSpec((bq, LANES), qtile),
            pl.BlockSpec((seq, aw), whole),
            pl.BlockSpec((seq // BK, aw, BK), lambda b, i: (b, 0, 0)),
            pl.BlockSpec((seq, LANES), whole),
            pl.BlockSpec(tri.shape, lambda b, i: (0, 0)),
        ],
        out_specs=pl.BlockSpec((bq, aw), qtile),
        scratch_shapes=[
            pltpu.VMEM((seq // BK, BK, bq), I32),
            pltpu.VMEM((seq // BK, BK, bq), F32),
            pltpu.VMEM((N_HEADS, LANES, bq), BF16),
            pltpu.VMEM((IDX_HEADS, LANES, bq), BF16),
            pltpu.VMEM((N_HEADS, BK, bq), F32),
            pltpu.VMEM((N_HEADS, 8, bq), F32),
            pltpu.VMEM((N_HEADS, 8, bq), F32),
            pltpu.VMEM((N_HEADS, HEAD_DIM, bq), F32),
        ],
        compiler_params=pltpu.CompilerParams(
            dimension_semantics=("arbitrary", "arbitrary"), vmem_limit_bytes=VMEM_LIMIT),
        name="dsa_attention",
    )(qt, qit, kw, k, vt, kw, tri)


def _post_kernel(x_ref, attn_ref, gyc_ref, sga_ref, mod_ref, wao_ref, wo_ref, o_ref):
    g1 = mod_ref[0, 2:3, :]
    y_attn = jnp.dot(attn_ref[...], wao_ref[...], preferred_element_type=F32)
    mix = gyc_ref[...] + sga_ref[...] * y_attn
    delta = jnp.dot(mix.astype(BF16), wo_ref[...], preferred_element_type=F32)
    o_ref[...] = x_ref[...] + g1 * delta


def _post_call(xf, attn, gyc, sga, mod3, wao, wo, seq):
    t, d = xf.shape
    bt = BT_PROJ
    tiles_per_seq = seq // bt
    aw = attn.shape[1]
    tile = lambda i: (i, 0)
    const = lambda i: (0, 0)
    return pl.pallas_call(
        _post_kernel,
        out_shape=jax.ShapeDtypeStruct((t, d), F32),
        grid=(t // bt,),
        in_specs=[
            pl.BlockSpec((bt, d), tile),
            pl.BlockSpec((bt, aw), tile),
            pl.BlockSpec((bt, d), tile),
            pl.BlockSpec((bt, d), tile),
            pl.BlockSpec((1, N_MOD, d), lambda i: (i // tiles_per_seq, 0, 0)),
            pl.BlockSpec(wao.shape, const),
            pl.BlockSpec(wo.shape, const),
        ],
        out_specs=pl.BlockSpec((bt, d), tile),
        compiler_params=pltpu.CompilerParams(
            dimension_semantics=("arbitrary",), vmem_limit_bytes=VMEM_LIMIT),
        name="post_mix",
    )(xf, attn, gyc, sga, mod3, wao, wo)


def _col_max(a):
    return jnp.max(a, axis=0, keepdims=True)


def _batcher_pairs(n):
    pairs = []
    p = 1
    while p < n:
        k = p
        while k >= 1:
            for j in range(k % p, n - k, 2 * k):
                for i in range(min(k, n - j - k)):
                    if (i + j) // (2 * p) == (i + j + k) // (2 * p):
                        pairs.append((i + j, i + j + k))
            k //= 2
        p *= 2
    return pairs


_SORT16 = _batcher_pairs(PEER_TOPK)
_BITONIC16 = [(k, k + d) for d in (8, 4, 2, 1) for k in range(PEER_TOPK) if not k & d]


def _sort_desc(v, pairs):
    v = list(v)
    for i, j in pairs:
        v[i], v[j] = jnp.maximum(v[i], v[j]), jnp.minimum(v[i], v[j])
    return v


def _peer_kernel(x_ref, mod_ref, n2_ref, wqt_ref, sk_ref, u_ref, vt_ref, o_ref,
                 h2t_ref, s1_ref, s2_ref, ap_ref, bp_ref, nsel_ref, rk_ref, w_ref, g_ref, acc_ref):
    e = pl.program_id(1)
    n_e = pl.num_programs(1)
    tt = x_ref.shape[0]
    nk = PEER_N_KEYS
    n_i = u_ref.shape[0] // nk
    nslab = tt // LANES
    cur = e % 2
    prev = 1 - cur

    @pl.when(e == 0)
    def _route():
        xv = x_ref[...]
        sh2 = mod_ref[0, 3:4, :]
        sc2 = mod_ref[0, 4:5, :]
        ms = jnp.mean(xv * xv, axis=-1, keepdims=True)
        h2t = ((xv * lax.rsqrt(ms + EPS) * n2_ref[...]) * (1.0 + sc2) + sh2).T.astype(BF16)
        h2t_ref[...] = h2t
        qt = jnp.dot(wqt_ref[...], h2t, preferred_element_type=F32).astype(BF16)
        for h in range(PEER_HEADS):
            for p, dst in ((0, s1_ref), (1, s2_ref)):
                r0 = (h * 2 + p) * PEER_HALF
                dst[h] = jnp.dot(sk_ref[h, p], qt[r0:r0 + PEER_HALF, :],
                                 preferred_element_type=F32)

        neg_inf = jnp.float32(-jnp.inf)
        sub_id = lax.broadcasted_iota(I32, (8, LANES), 0)

        def top16(s):
            v = _sort_desc([s[g * 8:(g + 1) * 8] for g in range(nk // 8)], _SORT16)
            for shift in (4, 2, 1):
                v = [jnp.maximum(v[r], pltpu.roll(v[PEER_TOPK - 1 - r], shift, 0)) for r in range(PEER_TOPK)]
                v = _sort_desc(v, _BITONIC16)
            return v

        def stack8(vals):
            out = vals[0]
            for r in range(1, 8):
                out = jnp.where(sub_id == r, vals[r], out)
            return out

        def route_head(h, carry):
            for c in range(nslab):
                ls = slice(c * LANES, (c + 1) * LANES)
                s1 = s1_ref[h, :, ls]
                s2 = s2_ref[h, :, ls]
                a = top16(s1)
                b = top16(s2)
                b_lo, b_hi = stack8(b[:8]), stack8(b[8:])
                cands = [a[0] + b_lo, a[0] + b_hi] + [a[r] + b_lo for r in range(1, PEER_TOPK)]
                work, t = cands, []
                for _ in range(PEER_TOPK):
                    m = _col_max(functools.reduce(jnp.maximum, work))
                    t.append(m)
                    work = [jnp.where(w == m, neg_inf, w) for w in work]
                thr = t[PEER_TOPK - 1]
                z = functools.reduce(jnp.add, [jnp.exp(tk - t[0]) for tk in t])
                hits = [jnp.where(cd >= thr, 1.0, 0.0) for cd in cands]
                n_sel = [jnp.sum(hits[0] + hits[1], axis=0, keepdims=True)]
                n_sel += [jnp.sum(hits[r + 1], axis=0, keepdims=True) for r in range(1, PEER_TOPK)]
                n_b = [jnp.broadcast_to(n, (8, LANES)) for n in n_sel]
                cnts, ranks = [], []
                for g in range(nk // 8):
                    rows = slice(g * 8, (g + 1) * 8)
                    s1g, s2g = s1[rows], s2[rows]
                    cnt = jnp.zeros((8, LANES), F32)
                    rank = jnp.ones((8, LANES), F32)
                    for r in range(PEER_TOPK):
                        cnt = jnp.where(s1g == a[r], n_b[r], cnt)
                        rank = rank + jnp.where(b[r] > s2g, 1.0, 0.0)
                    cnts.append(cnt)
                    ranks.append(rank)
                nsel_ref[h, :, ls] = jnp.concatenate(cnts, axis=0)
                rk_ref[h, :, ls] = jnp.concatenate(ranks, axis=0).astype(BF16)
                ap_ref[h, :, ls] = jnp.exp(s1 - a[0][0:1])
                bp_ref[h, :, ls] = (jnp.exp(s2 - b[0][0:1]) / z).astype(BF16)
            return carry

        lax.fori_loop(0, PEER_HEADS, route_head, 0)
        acc_ref[...] = jnp.zeros(acc_ref.shape, F32)
        w_ref[1] = jnp.zeros(w_ref.shape[1:], BF16)
        g_ref[1] = jnp.zeros(g_ref.shape[1:], F32)

    wa = (w_ref[prev].astype(F32) * g_ref[prev]).astype(BF16)
    acc_ref[...] += jnp.dot(vt_ref[0], wa, preferred_element_type=F32)

    act = jnp.dot(u_ref[...], h2t_ref[...], preferred_element_type=F32)
    g_ref[cur] = 0.5 * act * (1.0 + lax.erf(act * (2.0 ** -0.5)))

    irow0 = pl.multiple_of(jnp.minimum(e, n_e - 2) * n_i, n_i)
    pk = 16
    for c in range(nslab):
        ls = slice(c * LANES, (c + 1) * LANES)
        n_blk = [nsel_ref[h, pl.ds(irow0, n_i), ls] for h in range(PEER_HEADS)]
        ap_blk = [ap_ref[h, pl.ds(irow0, n_i), ls] for h in range(PEER_HEADS)]
        for ii in range(n_i):
            accs = [None] * (nk // pk)
            for h in range(PEER_HEADS):
                n16 = jnp.broadcast_to(n_blk[h][ii:ii + 1], (pk, LANES)).astype(BF16)
                a16 = jnp.broadcast_to(ap_blk[h][ii:ii + 1], (pk, LANES)).astype(BF16)
                for g in range(nk // pk):
                    rows = slice(g * pk, (g + 1) * pk)
                    hit = rk_ref[h, rows, ls] <= n16
                    term = jnp.where(hit, bp_ref[h, rows, ls], jnp.zeros((), BF16)) * a16
                    accs[g] = term if h == 0 else accs[g] + term
            for g in range(nk // pk):
                w_ref[cur, ii * nk + g * pk:ii * nk + (g + 1) * pk, ls] = accs[g]

    @pl.when(e == n_e - 1)
    def _finish():
        g2 = mod_ref[0, 5:6, :]
        o_ref[...] = x_ref[...] + g2 * acc_ref[...].T


def _peer_call(x1, mod3, n2, wqt, sk, u, vt, seq):
    t, d = x1.shape
    tt = TT_PEER
    ec = EC_PEER
    n_chunks = u.shape[0] // ec
    assert ec == 8 * PEER_N_KEYS
    tiles_per_seq = seq // tt
    tile = lambda i, e: (i, 0)
    const2 = lambda i, e: (0, 0)
    return pl.pallas_call(
        _peer_kernel,
        out_shape=jax.ShapeDtypeStruct((t, d), F32),
        grid=(t // tt, n_chunks + 1),
        in_specs=[
            pl.BlockSpec((tt, d), tile),
            pl.BlockSpec((1, N_MOD, d), lambda i, e: (i // tiles_per_seq, 0, 0)),
            pl.BlockSpec((1, d), const2),
            pl.BlockSpec(wqt.shape, const2),
            pl.BlockSpec(sk.shape, lambda i, e: (0, 0, 0, 0)),
            pl.BlockSpec((ec, d), lambda i, e: (jnp.minimum(e, n_chunks - 1), 0)),
            pl.BlockSpec((1, d, ec), lambda i, e: (jnp.maximum(e - 1, 0), 0, 0)),
        ],
        out_specs=pl.BlockSpec((tt, d), tile),
        scratch_shapes=[
            pltpu.VMEM((d, tt), BF16),
            pltpu.VMEM((PEER_HEADS, PEER_N_KEYS, tt), F32),
            pltpu.VMEM((PEER_HEADS, PEER_N_KEYS, tt), F32),
            pltpu.VMEM((PEER_HEADS, PEER_N_KEYS, tt), F32),
            pltpu.VMEM((PEER_HEADS, PEER_N_KEYS, tt), BF16),
            pltpu.VMEM((PEER_HEADS, PEER_N_KEYS, tt), F32),
            pltpu.VMEM((PEER_HEADS, PEER_N_KEYS, tt), BF16),
            pltpu.VMEM((2, ec, tt), BF16),
            pltpu.VMEM((2, ec, tt), F32),
            pltpu.VMEM((d, tt), F32),
        ],
        compiler_params=pltpu.CompilerParams(
            dimension_semantics=("arbitrary", "arbitrary"), vmem_limit_bytes=VMEM_LIMIT),
        name="peer",
    )(x1, mod3, n2, wqt, sk, u, vt)


def kernel(x, c, w_ada, b_ada, norm1_w, w_in, conv_w, w_conv_out, q_norm_w, k_norm_w,
           w_attn_out, w_o, norm2_w, w_peer_q, peer_sub_keys, peer_u, peer_v):
    b, s, d = x.shape
    depth = w_ada.shape[0]
    cw = conv_w.shape[2]
    aw = N_HEADS * HEAD_DIM
    iw = IDX_HEADS * IDX_DIM
    assert s % BQ == 0 and s % TT_PEER == 0 and s % BT_PROJ == 0
    assert peer_u.shape[1] == PEER_N_KEYS * PEER_N_KEYS

    gidx = jnp.arange(aw) // HEAD_DIM
    gmat = (gidx[:, None] == gidx[None, :]).astype(BF16)
    tri = (jnp.arange(BK)[:, None] >= jnp.arange(BK)[None, :]).astype(BF16)

    xf = x.reshape(b * s, d)
    for layer in range(depth):
        w = w_in[layer]
        o = 0
        wconv = w[:, o:o + 3 * cw].astype(BF16); o += 3 * cw
        wqkv = w[:, o:o + 3 * aw].astype(BF16); o += 3 * aw
        wqi = w[:, o:o + iw].astype(BF16); o += iw
        wkw = jnp.pad(w[:, o:o + IDX_DIM + IDX_HEADS], ((0, 0), (0, LANES - IDX_DIM - IDX_HEADS))).astype(BF16)
        o += IDX_DIM + IDX_HEADS
        wg = w[:, o:o + 2 * d].astype(BF16)
        qn_t = jnp.tile(q_norm_w[layer], N_HEADS).reshape(1, aw)
        kn_t = jnp.tile(k_norm_w[layer], N_HEADS).reshape(1, aw)

        mod3 = _mod_call(c, w_ada[layer], b_ada[layer]).reshape(b, N_MOD, d)
        gyc, sga, qt, k, vt, qit, kw = _inproj_call(
            xf, mod3, norm1_w[layer].reshape(1, d), wconv, wqkv, wqi, wkw, wg, conv_w[layer],
            w_conv_out[layer].astype(BF16), qn_t, kn_t, gmat, s)
        attn = _dsa_call(qt, qit, kw, k, vt, tri, b, s)
        x1 = _post_call(xf, attn, gyc, sga, mod3, w_attn_out[layer].astype(BF16),
                        w_o[layer].astype(BF16), s)
        xf = _peer_call(
            x1, mod3, norm2_w[layer].reshape(1, d), w_peer_q[layer].T.astype(BF16),
            peer_sub_keys[layer].astype(BF16), peer_u[layer].astype(BF16),
            peer_v[layer].astype(BF16).reshape(-1, EC_PEER, d).transpose(0, 2, 1), s)
    return xf.reshape(b, s, d)
```

```python
import functools

import jax
import jax.numpy as jnp
from jax import lax
from jax.experimental import pallas as pl
from jax.experimental.pallas import tpu as pltpu

F32 = jnp.float32
BF16 = jnp.bfloat16
I32 = jnp.int32

N_HEADS = 8
HEAD_DIM = 64
IDX_HEADS = 8
IDX_DIM = 64
TOPK_ATTN = 256
PEER_HEADS = 8
PEER_N_KEYS = 128
PEER_HALF = 64
PEER_TOPK = 16
CONV_K = 3
N_MOD = 6
EPS = 1e-6

LANES = 128
INT_MIN = -(2 ** 31)
NEG_BIG = -1e30
VMEM_LIMIT = 56 * 1024 * 1024

BT_PROJ = 256
BQ = 256
BK = 256
TT_PEER = 512
EC_PEER = 2048


def _split_bf16(a):
    hi = a.astype(BF16)
    lo = (a - hi.astype(F32)).astype(BF16)
    return hi, lo


def _nt_dot(a, b):
    return lax.dot_general(a, b, (((1,), (1,)), ((), ())), preferred_element_type=F32)


def _mod_kernel(c_ref, w_ref, b_ref, o_ref):
    c = c_ref[...]
    s = c * jax.nn.sigmoid(c)
    s_hi, s_lo = _split_bf16(s)
    w_hi, w_lo = _split_bf16(w_ref[...])
    acc = jnp.dot(s_hi, w_hi, preferred_element_type=F32)
    acc += jnp.dot(s_hi, w_lo, preferred_element_type=F32)
    acc += jnp.dot(s_lo, w_hi, preferred_element_type=F32)
    o_ref[...] = acc + b_ref[...]


def _mod_call(c, w_ada, b_ada):
    b, d = c.shape
    n = w_ada.shape[1]
    bn = d
    return pl.pallas_call(
        _mod_kernel,
        out_shape=jax.ShapeDtypeStruct((b, n), F32),
        grid=(n // bn,),
        in_specs=[
            pl.BlockSpec((b, d), lambda j: (0, 0)),
            pl.BlockSpec((d, bn), lambda j: (0, j)),
            pl.BlockSpec((1, bn), lambda j: (0, j)),
        ],
        out_specs=pl.BlockSpec((b, bn), lambda j: (0, j)),
        compiler_params=pltpu.CompilerParams(
            dimension_semantics=("arbitrary",), vmem_limit_bytes=VMEM_LIMIT),
        name="adaln_mod",
    )(c, w_ada, b_ada.reshape(1, n))


def _inproj_kernel(x_ref, xh_ref, mod_ref, n1_ref, wconv_ref, wqkv_ref, wqi_ref, wkw_ref,
                   wg_ref, convw_ref, wco_ref, qn_ref, kn_ref, gmat_ref,
                   gyc_ref, sga_ref, q_ref, k_ref, v_ref, qi_ref, kw_ref, *, tiles_per_seq):
    i = pl.program_id(0)
    cw = convw_ref.shape[1]
    aw = q_ref.shape[1]
    d = x_ref.shape[1]
    bt = x_ref.shape[0]
    sh1 = mod_ref[0, 0:1, :]
    sc1 = mod_ref[0, 1:2, :]

    def norm_mod(xv):
        ms = jnp.mean(xv * xv, axis=-1, keepdims=True)
        y = xv * lax.rsqrt(ms + EPS) * n1_ref[...]
        return (y * (1.0 + sc1) + sh1).astype(BF16)

    h = norm_mod(x_ref[...])
    hh = norm_mod(xh_ref[...])

    pc = jnp.dot(h, wconv_ref[...], preferred_element_type=F32)
    cb, cc, cx = pc[:, :cw], pc[:, cw:2 * cw], pc[:, 2 * cw:]
    u = cc * cx
    ph = jnp.dot(hh, wconv_ref[:, cw:], preferred_element_type=F32)
    uh = ph[:, :cw] * ph[:, cw:]
    uh = jnp.where(i % tiles_per_seq == 0, 0.0, uh)
    row = lax.broadcasted_iota(I32, (bt, cw), 0)
    u1 = jnp.where(row == 0, uh[7:8, :], pltpu.roll(u, 1, 0))
    u2 = pltpu.roll(u, 2, 0)
    u2 = jnp.where(row == 0, uh[6:7, :], jnp.where(row == 1, uh[7:8, :], u2))
    conv = convw_ref[0:1, :] * u2 + convw_ref[1:2, :] * u1 + convw_ref[2:3, :] * u
    yc = jnp.dot((cb * conv).astype(BF16), wco_ref[...], preferred_element_type=F32)

    pg = jnp.dot(h, wg_ref[...], preferred_element_type=F32)
    gyc_ref[...] = (jax.nn.sigmoid(pg[:, :d]) * yc).astype(gyc_ref.dtype)
    sga_ref[...] = jax.nn.sigmoid(pg[:, d:]).astype(sga_ref.dtype)

    pq = jnp.dot(h, wqkv_ref[...], preferred_element_type=F32)
    gmat = gmat_ref[...]

    def head_norm(t, w_row):
        hi, lo = _split_bf16(t * t)
        ss = (jnp.dot(hi, gmat, preferred_element_type=F32)
              + jnp.dot(lo, gmat, preferred_element_type=F32))
        return t * lax.rsqrt(ss * (1.0 / HEAD_DIM) + EPS) * w_row

    qn = head_norm(pq[:, :aw], qn_ref[...])
    q_ref[0] = (qn * (HEAD_DIM ** -0.5)).T.astype(BF16)
    k_ref[...] = head_norm(pq[:, aw:2 * aw], kn_ref[...]).astype(BF16)
    v_ref[0] = pq[:, 2 * aw:].T.astype(BF16)

    qi_ref[0] = jnp.dot(h, wqi_ref[...], preferred_element_type=F32).T.astype(BF16)
    kw_ref[...] = jnp.dot(h, wkw_ref[...], preferred_element_type=F32)


def _inproj_call(xf, mod3, n1, wconv, wqkv, wqi, wkw, wg, convw, wco, qn_t, kn_t, gmat, seq):
    t, d = xf.shape
    bt = BT_PROJ
    tiles_per_seq = seq // bt
    cw = convw.shape[1]
    aw = qn_t.shape[1]
    const = lambda i: (0, 0)
    tile = lambda i: (i, 0)
    tile_t = lambda i: (i, 0, 0)
    halo = lambda i: (jnp.maximum(i * (bt // 8) - 1, 0), 0)
    kern = functools.partial(_inproj_kernel, tiles_per_seq=tiles_per_seq)
    return pl.pallas_call(
        kern,
        out_shape=[
            jax.ShapeDtypeStruct((t, d), BF16),
            jax.ShapeDtypeStruct((t, d), BF16),
            jax.ShapeDtypeStruct((t // bt, aw, bt), BF16),
            jax.ShapeDtypeStruct((t, aw), BF16),
            jax.ShapeDtypeStruct((t // bt, aw, bt), BF16),
            jax.ShapeDtypeStruct((t // bt, aw, bt), BF16),
            jax.ShapeDtypeStruct((t, LANES), F32),
        ],
        grid=(t // bt,),
        in_specs=[
            pl.BlockSpec((bt, d), tile),
            pl.BlockSpec((8, d), halo),
            pl.BlockSpec((1, N_MOD, d), lambda i: (i // tiles_per_seq, 0, 0)),
            pl.BlockSpec((1, d), const),
            pl.BlockSpec(wconv.shape, const),
            pl.BlockSpec(wqkv.shape, const),
            pl.BlockSpec(wqi.shape, const),
            pl.BlockSpec(wkw.shape, const),
            pl.BlockSpec(wg.shape, const),
            pl.BlockSpec(convw.shape, const),
            pl.BlockSpec(wco.shape, const),
            pl.BlockSpec((1, aw), const),
            pl.BlockSpec((1, aw), const),
            pl.BlockSpec(gmat.shape, const),
        ],
        out_specs=[
            pl.BlockSpec((bt, d), tile),
            pl.BlockSpec((bt, d), tile),
            pl.BlockSpec((1, aw, bt), tile_t),
            pl.BlockSpec((bt, aw), tile),
            pl.BlockSpec((1, aw, bt), tile_t),
            pl.BlockSpec((1, aw, bt), tile_t),
            pl.BlockSpec((bt, LANES), tile),
        ],
        compiler_params=pltpu.CompilerParams(
            dimension_semantics=("arbitrary",), vmem_limit_bytes=VMEM_LIMIT),
        name="inproj",
    )(xf, xf, mod3, n1, wconv, wqkv, wqi, wkw, wg, convw, wco, qn_t, kn_t, gmat)


def _tree_sum(parts):
    parts = list(parts)
    while len(parts) > 1:
        nxt = [parts[a] + parts[a + 1] for a in range(0, len(parts) - 1, 2)]
        if len(parts) % 2:
            nxt.append(parts[-1])
        parts = nxt
    return parts[0]


def _dsa_kernel(qt_ref, qit_ref, kwq_ref, k_ref, vt_ref, kwk_ref, tri_ref, o_ref,
                keys_ref, bias_ref, qz_ref, qiz_ref, s_ref, m_ref, l_ref, acc_ref, *, n_sel):
    i = pl.program_id(1)
    bq = qt_ref.shape[2]
    bk = BK
    n_tiles = (i * bq) // bk + bq // bk
    ngrp = bk // 8

    zpad = jnp.zeros((LANES - HEAD_DIM, bq), BF16)
    for h in range(N_HEADS):
        qh = qt_ref[0, h * HEAD_DIM:(h + 1) * HEAD_DIM, :]
        qz_ref[h] = jnp.concatenate([qh, zpad] if h % 2 == 0 else [zpad, qh], axis=0)
        qiz_ref[h] = jnp.concatenate([qit_ref[0, h * IDX_DIM:(h + 1) * IDX_DIM, :], zpad], axis=0)
    wt = kwq_ref[...].T

    q_pos = i * bq + lax.broadcasted_iota(I32, (bk, bq), 1)

    def score_tile(j, carry):
        kw = kwk_ref[pl.ds(pl.multiple_of(j * bk, bk), bk), :].astype(BF16)
        score = jnp.zeros((bk, bq), F32)
        for h in range(IDX_HEADS):
            logits = jnp.dot(kw, qiz_ref[h], preferred_element_type=F32)
            score = score + wt[IDX_DIM + h:IDX_DIM + h + 1, :] * jnp.maximum(logits, 0.0)
        bits = pltpu.bitcast(score, I32)
        key = jnp.where(bits < 0, bits ^ jnp.int32(0x7FFFFFFF), bits)
        key = jnp.where(bits == jnp.int32(INT_MIN), 0, key)
        key_pos = j * bk + lax.broadcasted_iota(I32, (bk, bq), 0)
        keys_ref[j] = jnp.where(key_pos <= q_pos, key, jnp.int32(INT_MIN))
        return carry

    lax.fori_loop(0, n_tiles, score_tile, 0)

    def count_ge(cand, strict):
        cand_b = jnp.broadcast_to(cand, (8, bq))

        def body(j, cnt):
            hits = []
            for g in range(ngrp):
                ks = keys_ref[j, g * 8:(g + 1) * 8, :]
                hit = (ks > cand_b) if strict else (ks >= cand_b)
                hits.append(jnp.where(hit, 1, 0))
            return cnt + _tree_sum(hits)

        cnt = lax.fori_loop(0, n_tiles, body, jnp.zeros((8, bq), I32))
        return jnp.sum(cnt, axis=0, keepdims=True)

    def bit_step(b, t_u):
        cand_u = t_u | lax.shift_left(jnp.int32(1), 31 - b)
        total = count_ge(cand_u ^ jnp.int32(INT_MIN), False)
        return jnp.where(total >= n_sel, cand_u, t_u)

    t_u = lax.fori_loop(0, 32, bit_step, jnp.zeros((1, bq), I32))
    thr = t_u ^ jnp.int32(INT_MIN)
    n_gt = count_ge(thr, True)
    need = jnp.where(thr == jnp.int32(INT_MIN), 0, n_sel - n_gt).astype(F32)

    tri = tri_ref[...]

    def bias_tile(j, carry):
        kt = keys_ref[j]
        eq = kt == thr
        eq_f = jnp.where(eq, 1.0, 0.0)
        prefix = jnp.dot(tri, eq_f.astype(BF16), preferred_element_type=F32) + carry
        sel = (kt > thr) | (eq & (prefix <= need))
        bias_ref[j] = jnp.where(sel, 0.0, NEG_BIG)
        return prefix[bk - 1:bk, :]

    lax.fori_loop(0, n_tiles, bias_tile, jnp.zeros((1, bq), F32))

    m_ref[...] = jnp.full(m_ref.shape, NEG_BIG, F32)
    l_ref[...] = jnp.zeros(l_ref.shape, F32)
    acc_ref[...] = jnp.zeros(acc_ref.shape, F32)

    def attn_tile(j, carry):
        kt = k_ref[pl.ds(pl.multiple_of(j * bk, bk), bk), :]
        bias = bias_ref[j]
        alphas = []
        for h in range(N_HEADS):
            pair = slice((h // 2) * LANES, (h // 2 + 1) * LANES)
            s = jnp.dot(kt[:, pair], qz_ref[h], preferred_element_type=F32) + bias
            s_ref[h] = s
            m_prev = m_ref[h]
            m_new = jnp.maximum(m_prev, jnp.broadcast_to(jnp.max(s, axis=0, keepdims=True), (8, bq)))
            m_ref[h] = m_new
            alphas.append(jnp.exp(m_prev - m_new))
        for h in range(N_HEADS):
            p = jnp.exp(s_ref[h] - m_ref[h, 0:1, :])
            l_ref[h] = alphas[h] * l_ref[h] + _tree_sum([p[g * 8:(g + 1) * 8] for g in range(ngrp)])
            pv = jnp.dot(vt_ref[j, h * HEAD_DIM:(h + 1) * HEAD_DIM, :], p.astype(BF16),
                         preferred_element_type=F32)
            acc_ref[h] = alphas[h][0:1] * acc_ref[h] + pv
        return carry

    lax.fori_loop(0, n_tiles, attn_tile, 0)

    outs = [acc_ref[h] / jnp.sum(l_ref[h], axis=0, keepdims=True) for h in range(N_HEADS)]
    o_ref[...] = jnp.concatenate(outs, axis=0).T.astype(o_ref.dtype)


def _dsa_call(qt, qit, kw, k, vt, tri, batch, seq):
    nt, aw, bq = qt.shape
    assert bq == BQ and bq == BK
    nq = seq // bq
    n_sel = min(TOPK_ATTN, seq // 4)
    qtile = lambda b, i: (b * nq + i, 0)
    qtile_t = lambda b, i: (b * nq + i, 0, 0)
    whole = lambda b, i: (b, 0)
    kern = functools.partial(_dsa_kernel, n_sel=n_sel)
    return pl.pallas_call(
        kern,
        out_shape=jax.ShapeDtypeStruct((batch * seq, aw), BF16),
        grid=(batch, nq),
        in_specs=[
            pl.BlockSpec((1, aw, bq), qtile_t),
            pl.BlockSpec((1, aw, bq), qtile_t),
            pl.BlockSpec((bq, LANES), qtile),
            pl.BlockSpec((seq, aw), whole),
            pl.BlockSpec((seq // BK, aw, BK), lambda b, i: (b, 0, 0)),
            pl.BlockSpec((seq, LANES), whole),
            pl.BlockSpec(tri.shape, lambda b, i: (0, 0)),
        ],
        out_specs=pl.BlockSpec((bq, aw), qtile),
        scratch_shapes=[
            pltpu.VMEM((seq // BK, BK, bq), I32),
            pltpu.VMEM((seq // BK, BK, bq), F32),
            pltpu.VMEM((N_HEADS, LANES, bq), BF16),
            pltpu.VMEM((IDX_HEADS, LANES, bq), BF16),
            pltpu.VMEM((N_HEADS, BK, bq), F32),
            pltpu.VMEM((N_HEADS, 8, bq), F32),
            pltpu.VMEM((N_HEADS, 8, bq), F32),
            pltpu.VMEM((N_HEADS, HEAD_DIM, bq), F32),
        ],
        compiler_params=pltpu.CompilerParams(
            dimension_semantics=("arbitrary", "arbitrary"), vmem_limit_bytes=VMEM_LIMIT),
        name="dsa_attention",
    )(qt, qit, kw, k, vt, kw, tri)


def _post_kernel(x_ref, attn_ref, gyc_ref, sga_ref, mod_ref, wao_ref, wo_ref, o_ref):
    g1 = mod_ref[0, 2:3, :]
    y_attn = jnp.dot(attn_ref[...], wao_ref[...], preferred_element_type=F32)
    mix = gyc_ref[...] + sga_ref[...] * y_attn
    delta = jnp.dot(mix.astype(BF16), wo_ref[...], preferred_element_type=F32)
    o_ref[...] = x_ref[...] + g1 * delta


def _post_call(xf, attn, gyc, sga, mod3, wao, wo, seq):
    t, d = xf.shape
    bt = BT_PROJ
    tiles_per_seq = seq // bt
    aw = attn.shape[1]
    tile = lambda i: (i, 0)
    const = lambda i: (0, 0)
    return pl.pallas_call(
        _post_kernel,
        out_shape=jax.ShapeDtypeStruct((t, d), F32),
        grid=(t // bt,),
        in_specs=[
            pl.BlockSpec((bt, d), tile),
            pl.BlockSpec((bt, aw), tile),
            pl.BlockSpec((bt, d), tile),
            pl.BlockSpec((bt, d), tile),
            pl.BlockSpec((1, N_MOD, d), lambda i: (i // tiles_per_seq, 0, 0)),
            pl.BlockSpec(wao.shape, const),
            pl.BlockSpec(wo.shape, const),
        ],
        out_specs=pl.BlockSpec((bt, d), tile),
        compiler_params=pltpu.CompilerParams(
            dimension_semantics=("arbitrary",), vmem_limit_bytes=VMEM_LIMIT),
        name="post_mix",
    )(xf, attn, gyc, sga, mod3, wao, wo)


def _col_max(a):
    return jnp.max(a, axis=0, keepdims=True)


def _batcher_pairs(n):
    pairs = []
    p = 1
    while p < n:
        k = p
        while k >= 1:
            for j in range(k % p, n - k, 2 * k):
                for i in range(min(k, n - j - k)):
                    if (i + j) // (2 * p) == (i + j + k) // (2 * p):
                        pairs.append((i + j, i + j + k))
            k //= 2
        p *= 2
    return pairs


_SORT16 = _batcher_pairs(PEER_TOPK)
_BITONIC16 = [(k, k + d) for d in (8, 4, 2, 1) for k in range(PEER_TOPK) if not k & d]


def _sort_desc(v, pairs):
    v = list(v)
    for i, j in pairs:
        v[i], v[j] = jnp.maximum(v[i], v[j]), jnp.minimum(v[i], v[j])
    return v


def _peer_kernel(x_ref, mod_ref, n2_ref, wqt_ref, sk_ref, u_ref, vt_ref, o_ref,
                 h2t_ref, s1_ref, s2_ref, ap_ref, bp_ref, nsel_ref, rk_ref, w_ref, acc_ref):
    e = pl.program_id(1)
    n_e = pl.num_programs(1)
    tt = x_ref.shape[0]
    nk = PEER_N_KEYS
    n_i = u_ref.shape[0] // nk
    nslab = tt // LANES

    @pl.when(e == 0)
    def _route():
        xv = x_ref[...]
        sh2 = mod_ref[0, 3:4, :]
        sc2 = mod_ref[0, 4:5, :]
        ms = jnp.mean(xv * xv, axis=-1, keepdims=True)
        h2t = ((xv * lax.rsqrt(ms + EPS) * n2_ref[...]) * (1.0 + sc2) + sh2).T.astype(BF16)
        h2t_ref[...] = h2t
        qt = jnp.dot(wqt_ref[...], h2t, preferred_element_type=F32).astype(BF16)
        for h in range(PEER_HEADS):
            for p, dst in ((0, s1_ref), (1, s2_ref)):
                r0 = (h * 2 + p) * PEER_HALF
                dst[h] = jnp.dot(sk_ref[h, p], qt[r0:r0 + PEER_HALF, :],
                                 preferred_element_type=F32)

        neg_inf = jnp.float32(-jnp.inf)
        sub_id = lax.broadcasted_iota(I32, (8, LANES), 0)

        def top16(s):
            v = _sort_desc([s[g * 8:(g + 1) * 8] for g in range(nk // 8)], _SORT16)
            for shift in (4, 2, 1):
                v = [jnp.maximum(v[r], pltpu.roll(v[PEER_TOPK - 1 - r], shift, 0)) for r in range(PEER_TOPK)]
                v = _sort_desc(v, _BITONIC16)
            return v

        def stack8(vals):
            out = vals[0]
            for r in range(1, 8):
                out = jnp.where(sub_id == r, vals[r], out)
            return out

        def route_head(h, carry):
            for c in range(nslab):
                ls = slice(c * LANES, (c + 1) * LANES)
                s1 = s1_ref[h, :, ls]
                s2 = s2_ref[h, :, ls]
                a = top16(s1)
                b = top16(s2)
                b_lo, b_hi = stack8(b[:8]), stack8(b[8:])
                cands = [a[0] + b_lo, a[0] + b_hi] + [a[r] + b_lo for r in range(1, PEER_TOPK)]
                work, t = cands, []
                for _ in range(PEER_TOPK):
                    m = _col_max(functools.reduce(jnp.maximum, work))
                    t.append(m)
                    work = [jnp.where(w == m, neg_inf, w) for w in work]
                thr = t[PEER_TOPK - 1]
                z = functools.reduce(jnp.add, [jnp.exp(tk - t[0]) for tk in t])
                hits = [jnp.where(cd >= thr, 1.0, 0.0) for cd in cands]
                n_sel = [jnp.sum(hits[0] + hits[1], axis=0, keepdims=True)]
                n_sel += [jnp.sum(hits[r + 1], axis=0, keepdims=True) for r in range(1, PEER_TOPK)]
                n_b = [jnp.broadcast_to(n, (8, LANES)) for n in n_sel]
                cnts, ranks = [], []
                for g in range(nk // 8):
                    rows = slice(g * 8, (g + 1) * 8)
                    s1g, s2g = s1[rows], s2[rows]
                    cnt = jnp.zeros((8, LANES), F32)
                    rank = jnp.ones((8, LANES), F32)
                    for r in range(PEER_TOPK):
                        cnt = jnp.where(s1g == a[r], n_b[r], cnt)
                        rank = rank + jnp.where(b[r] > s2g, 1.0, 0.0)
                    cnts.append(cnt)
                    ranks.append(rank)
                nsel_ref[h, :, ls] = jnp.concatenate(cnts, axis=0)
                rk_ref[h, :, ls] = jnp.concatenate(ranks, axis=0).astype(BF16)
                ap_ref[h, :, ls] = jnp.exp(s1 - a[0][0:1])
                bp_ref[h, :, ls] = (jnp.exp(s2 - b[0][0:1]) / z).astype(BF16)
            return carry

        lax.fori_loop(0, PEER_HEADS, route_head, 0)
        acc_ref[...] = jnp.zeros(acc_ref.shape, F32)

    irow0 = pl.multiple_of(e * n_i, 8)
    pk = 16
    for c in range(nslab):
        ls = slice(c * LANES, (c + 1) * LANES)
        n_blk = [nsel_ref[h, pl.ds(irow0, n_i), ls] for h in range(PEER_HEADS)]
        ap_blk = [ap_ref[h, pl.ds(irow0, n_i), ls] for h in range(PEER_HEADS)]
        for ii in range(n_i):
            accs = [None] * (nk // pk)
            for h in range(PEER_HEADS):
                n16 = jnp.broadcast_to(n_blk[h][ii:ii + 1], (pk, LANES)).astype(BF16)
                a16 = jnp.broadcast_to(ap_blk[h][ii:ii + 1], (pk, LANES)).astype(BF16)
                for g in range(nk // pk):
                    rows = slice(g * pk, (g + 1) * pk)
                    hit = rk_ref[h, rows, ls] <= n16
                    term = jnp.where(hit, bp_ref[h, rows, ls], jnp.zeros((), BF16)) * a16
                    accs[g] = term if h == 0 else accs[g] + term
            for g in range(nk // pk):
                w_ref[ii * nk + g * pk:ii * nk + (g + 1) * pk, ls] = accs[g]

    act = jnp.dot(u_ref[...], h2t_ref[...], preferred_element_type=F32)
    gelu = 0.5 * act * (1.0 + lax.erf(act * (2.0 ** -0.5)))
    wa = (w_ref[...].astype(F32) * gelu).astype(BF16)
    acc_ref[...] += jnp.dot(vt_ref[0], wa, preferred_element_type=F32)

    @pl.when(e == n_e - 1)
    def _finish():
        g2 = mod_ref[0, 5:6, :]
        o_ref[...] = x_ref[...] + g2 * acc_ref[...].T


def _peer_call(x1, mod3, n2, wqt, sk, u, vt, seq):
    t, d = x1.shape
    tt = TT_PEER
    ec = EC_PEER
    n_chunks = u.shape[0] // ec
    assert ec % (8 * PEER_N_KEYS) == 0
    tiles_per_seq = seq // tt
    tile = lambda i, e: (i, 0)
    const2 = lambda i, e: (0, 0)
    return pl.pallas_call(
        _peer_kernel,
        out_shape=jax.ShapeDtypeStruct((t, d), F32),
        grid=(t // tt, n_chunks),
        in_specs=[
            pl.BlockSpec((tt, d), tile),
            pl.BlockSpec((1, N_MOD, d), lambda i, e: (i // tiles_per_seq, 0, 0)),
            pl.BlockSpec((1, d), const2),
            pl.BlockSpec(wqt.shape, const2),
            pl.BlockSpec(sk.shape, lambda i, e: (0, 0, 0, 0)),
            pl.BlockSpec((ec, d), lambda i, e: (e, 0)),
            pl.BlockSpec((1, d, ec), lambda i, e: (e, 0, 0)),
        ],
        out_specs=pl.BlockSpec((tt, d), tile),
        scratch_shapes=[
            pltpu.VMEM((d, tt), BF16),
            pltpu.VMEM((PEER_HEADS, PEER_N_KEYS, tt), F32),
            pltpu.VMEM((PEER_HEADS, PEER_N_KEYS, tt), F32),
            pltpu.VMEM((PEER_HEADS, PEER_N_KEYS, tt), F32),
            pltpu.VMEM((PEER_HEADS, PEER_N_KEYS, tt), BF16),
            pltpu.VMEM((PEER_HEADS, PEER_N_KEYS, tt), F32),
            pltpu.VMEM((PEER_HEADS, PEER_N_KEYS, tt), BF16),
            pltpu.VMEM((ec, tt), BF16),
            pltpu.VMEM((d, tt), F32),
        ],
        compiler_params=pltpu.CompilerParams(
            dimension_semantics=("arbitrary", "arbitrary"), vmem_limit_bytes=VMEM_LIMIT),
        name="peer",
    )(x1, mod3, n2, wqt, sk, u, vt)


def kernel(x, c, w_ada, b_ada, norm1_w, w_in, conv_w, w_conv_out, q_norm_w, k_norm_w,
           w_attn_out, w_o, norm2_w, w_peer_q, peer_sub_keys, peer_u, peer_v):
    b, s, d = x.shape
    depth = w_ada.shape[0]
    cw = conv_w.shape[2]
    aw = N_HEADS * HEAD_DIM
    iw = IDX_HEADS * IDX_DIM
    assert s % BQ == 0 and s % TT_PEER == 0 and s % BT_PROJ == 0
    assert peer_u.shape[1] == PEER_N_KEYS * PEER_N_KEYS

    gidx = jnp.arange(aw) // HEAD_DIM
    gmat = (gidx[:, None] == gidx[None, :]).astype(BF16)
    tri = (jnp.arange(BK)[:, None] >= jnp.arange(BK)[None, :]).astype(BF16)

    xf = x.reshape(b * s, d)
    for layer in range(depth):
        w = w_in[layer]
        o = 0
        wconv = w[:, o:o + 3 * cw].astype(BF16); o += 3 * cw
        wqkv = w[:, o:o + 3 * aw].astype(BF16); o += 3 * aw
        wqi = w[:, o:o + iw].astype(BF16); o += iw
        wkw = jnp.pad(w[:, o:o + IDX_DIM + IDX_HEADS], ((0, 0), (0, LANES - IDX_DIM - IDX_HEADS))).astype(BF16)
        o += IDX_DIM + IDX_HEADS
        wg = w[:, o:o + 2 * d].astype(BF16)
        qn_t = jnp.tile(q_norm_w[layer], N_HEADS).reshape(1, aw)
        kn_t = jnp.tile(k_norm_w[layer], N_HEADS).reshape(1, aw)

        mod3 = _mod_call(c, w_ada[layer], b_ada[layer]).reshape(b, N_MOD, d)
        gyc, sga, qt, k, vt, qit, kw = _inproj_call(
            xf, mod3, norm1_w[layer].reshape(1, d), wconv, wqkv, wqi, wkw, wg, conv_w[layer],
            w_conv_out[layer].astype(BF16), qn_t, kn_t, gmat, s)
        attn = _dsa_call(qt, qit, kw, k, vt, tri, b, s)
        x1 = _post_call(xf, attn, gyc, sga, mod3, w_attn_out[layer].astype(BF16),
                        w_o[layer].astype(BF16), s)
        xf = _peer_call(
            x1, mod3, norm2_w[layer].reshape(1, d), w_peer_q[layer].T.astype(BF16),
            peer_sub_keys[layer].astype(BF16), peer_u[layer].astype(BF16),
            peer_v[layer].astype(BF16).reshape(-1, EC_PEER, d).transpose(0, 2, 1), s)
    return xf.reshape(b, s, d)
```

```python
import functools

import jax
import jax.numpy as jnp
from jax import lax
from jax.experimental import pallas as pl
from jax.experimental.pallas import tpu as pltpu

F32 = jnp.float32
BF16 = jnp.bfloat16
I32 = jnp.int32

N_HEADS = 8
HEAD_DIM = 64
IDX_HEADS = 8
IDX_DIM = 64
TOPK_ATTN = 256
PEER_HEADS = 8
PEER_N_KEYS = 128
PEER_HALF = 64
PEER_TOPK = 16
CONV_K = 3
N_MOD = 6
EPS = 1e-6

LANES = 128
INT_MIN = -(2 ** 31)
NEG_BIG = -1e30
VMEM_LIMIT = 56 * 1024 * 1024

BT_PROJ = 256
BQ = 256
BK = 256
SEARCH_BITS = 8
TT_PEER = 512
EC_PEER = 2048


def _split_bf16(a):
    hi = a.astype(BF16)
    lo = (a - hi.astype(F32)).astype(BF16)
    return hi, lo


def _nt_dot(a, b):
    return lax.dot_general(a, b, (((1,), (1,)), ((), ())), preferred_element_type=F32)


def _mod_kernel(c_ref, w_ref, b_ref, o_ref):
    c = c_ref[...]
    s = c * jax.nn.sigmoid(c)
    s_hi, s_lo = _split_bf16(s)
    w_hi, w_lo = _split_bf16(w_ref[...])
    acc = jnp.dot(s_hi, w_hi, preferred_element_type=F32)
    acc += jnp.dot(s_hi, w_lo, preferred_element_type=F32)
    acc += jnp.dot(s_lo, w_hi, preferred_element_type=F32)
    o_ref[...] = acc + b_ref[...]


def _mod_call(c, w_ada, b_ada):
    b, d = c.shape
    n = w_ada.shape[1]
    bn = d
    return pl.pallas_call(
        _mod_kernel,
        out_shape=jax.ShapeDtypeStruct((b, n), F32),
        grid=(n // bn,),
        in_specs=[
            pl.BlockSpec((b, d), lambda j: (0, 0)),
            pl.BlockSpec((d, bn), lambda j: (0, j)),
            pl.BlockSpec((1, bn), lambda j: (0, j)),
        ],
        out_specs=pl.BlockSpec((b, bn), lambda j: (0, j)),
        compiler_params=pltpu.CompilerParams(
            dimension_semantics=("arbitrary",), vmem_limit_bytes=VMEM_LIMIT),
        name="adaln_mod",
    )(c, w_ada, b_ada.reshape(1, n))


def _inproj_kernel(x_ref, xh_ref, mod_ref, n1_ref, wconv_ref, wqkv_ref, wqi_ref, wkw_ref,
                   wg_ref, convw_ref, wco_ref, qn_ref, kn_ref, gmat_ref,
                   gyc_ref, sga_ref, q_ref, k_ref, v_ref, qi_ref, kw_ref, *, tiles_per_seq):
    i = pl.program_id(0)
    cw = convw_ref.shape[1]
    aw = q_ref.shape[1]
    d = x_ref.shape[1]
    bt = x_ref.shape[0]
    sh1 = mod_ref[0, 0:1, :]
    sc1 = mod_ref[0, 1:2, :]

    def norm_mod(xv):
        ms = jnp.mean(xv * xv, axis=-1, keepdims=True)
        y = xv * lax.rsqrt(ms + EPS) * n1_ref[...]
        return (y * (1.0 + sc1) + sh1).astype(BF16)

    h = norm_mod(x_ref[...])
    hh = norm_mod(xh_ref[...])

    pc = jnp.dot(h, wconv_ref[...], preferred_element_type=F32)
    cb, cc, cx = pc[:, :cw], pc[:, cw:2 * cw], pc[:, 2 * cw:]
    u = cc * cx
    ph = jnp.dot(hh, wconv_ref[:, cw:], preferred_element_type=F32)
    uh = ph[:, :cw] * ph[:, cw:]
    uh = jnp.where(i % tiles_per_seq == 0, 0.0, uh)
    row = lax.broadcasted_iota(I32, (bt, cw), 0)
    u1 = jnp.where(row == 0, uh[7:8, :], pltpu.roll(u, 1, 0))
    u2 = pltpu.roll(u, 2, 0)
    u2 = jnp.where(row == 0, uh[6:7, :], jnp.where(row == 1, uh[7:8, :], u2))
    conv = convw_ref[0:1, :] * u2 + convw_ref[1:2, :] * u1 + convw_ref[2:3, :] * u
    yc = jnp.dot((cb * conv).astype(BF16), wco_ref[...], preferred_element_type=F32)

    pg = jnp.dot(h, wg_ref[...], preferred_element_type=F32)
    gyc_ref[...] = (jax.nn.sigmoid(pg[:, :d]) * yc).astype(gyc_ref.dtype)
    sga_ref[...] = jax.nn.sigmoid(pg[:, d:]).astype(sga_ref.dtype)

    pq = jnp.dot(h, wqkv_ref[...], preferred_element_type=F32)
    gmat = gmat_ref[...]

    def head_norm(t, w_row):
        hi, lo = _split_bf16(t * t)
        ss = (jnp.dot(hi, gmat, preferred_element_type=F32)
              + jnp.dot(lo, gmat, preferred_element_type=F32))
        return t * lax.rsqrt(ss * (1.0 / HEAD_DIM) + EPS) * w_row

    qn = head_norm(pq[:, :aw], qn_ref[...])
    q_ref[0] = (qn * (HEAD_DIM ** -0.5)).T.astype(BF16)
    k_ref[...] = head_norm(pq[:, aw:2 * aw], kn_ref[...]).astype(BF16)
    v_ref[0] = pq[:, 2 * aw:].T.astype(BF16)

    qi_ref[0] = jnp.dot(h, wqi_ref[...], preferred_element_type=F32).T.astype(BF16)
    kw_ref[...] = jnp.dot(h, wkw_ref[...], preferred_element_type=F32)


def _inproj_call(xf, mod3, n1, wconv, wqkv, wqi, wkw, wg, convw, wco, qn_t, kn_t, gmat, seq):
    t, d = xf.shape
    bt = BT_PROJ
    tiles_per_seq = seq // bt
    cw = convw.shape[1]
    aw = qn_t.shape[1]
    const = lambda i: (0, 0)
    tile = lambda i: (i, 0)
    tile_t = lambda i: (i, 0, 0)
    halo = lambda i: (jnp.maximum(i * (bt // 8) - 1, 0), 0)
    kern = functools.partial(_inproj_kernel, tiles_per_seq=tiles_per_seq)
    return pl.pallas_call(
        kern,
        out_shape=[
            jax.ShapeDtypeStruct((t, d), BF16),
            jax.ShapeDtypeStruct((t, d), BF16),
            jax.ShapeDtypeStruct((t // bt, aw, bt), BF16),
            jax.ShapeDtypeStruct((t, aw), BF16),
            jax.ShapeDtypeStruct((t // bt, aw, bt), BF16),
            jax.ShapeDtypeStruct((t // bt, aw, bt), BF16),
            jax.ShapeDtypeStruct((t, LANES), F32),
        ],
        grid=(t // bt,),
        in_specs=[
            pl.BlockSpec((bt, d), tile),
            pl.BlockSpec((8, d), halo),
            pl.BlockSpec((1, N_MOD, d), lambda i: (i // tiles_per_seq, 0, 0)),
            pl.BlockSpec((1, d), const),
            pl.BlockSpec(wconv.shape, const),
            pl.BlockSpec(wqkv.shape, const),
            pl.BlockSpec(wqi.shape, const),
            pl.BlockSpec(wkw.shape, const),
            pl.BlockSpec(wg.shape, const),
            pl.BlockSpec(convw.shape, const),
            pl.BlockSpec(wco.shape, const),
            pl.BlockSpec((1, aw), const),
            pl.BlockSpec((1, aw), const),
            pl.BlockSpec(gmat.shape, const),
        ],
        out_specs=[
            pl.BlockSpec((bt, d), tile),
            pl.BlockSpec((bt, d), tile),
            pl.BlockSpec((1, aw, bt), tile_t),
            pl.BlockSpec((bt, aw), tile),
            pl.BlockSpec((1, aw, bt), tile_t),
            pl.BlockSpec((1, aw, bt), tile_t),
            pl.BlockSpec((bt, LANES), tile),
        ],
        compiler_params=pltpu.CompilerParams(
            dimension_semantics=("arbitrary",), vmem_limit_bytes=VMEM_LIMIT),
        name="inproj",
    )(xf, xf, mod3, n1, wconv, wqkv, wqi, wkw, wg, convw, wco, qn_t, kn_t, gmat)


def _tree_sum(parts):
    parts = list(parts)
    while len(parts) > 1:
        nxt = [parts[a] + parts[a + 1] for a in range(0, len(parts) - 1, 2)]
        if len(parts) % 2:
            nxt.append(parts[-1])
        parts = nxt
    return parts[0]


def _dsa_kernel(qt_ref, qit_ref, kwq_ref, k_ref, vt_ref, kwk_ref, tri_ref, o_ref,
                keys_ref, bias_ref, qz_ref, qiz_ref, s_ref, m_ref, l_ref, acc_ref, *, n_sel):
    i = pl.program_id(1)
    bq = qt_ref.shape[2]
    bk = BK
    n_tiles = (i * bq) // bk + bq // bk
    ngrp = bk // 8

    zpad = jnp.zeros((LANES - HEAD_DIM, bq), BF16)
    for h in range(N_HEADS):
        qh = qt_ref[0, h * HEAD_DIM:(h + 1) * HEAD_DIM, :]
        qz_ref[h] = jnp.concatenate([qh, zpad] if h % 2 == 0 else [zpad, qh], axis=0)
        qiz_ref[h] = jnp.concatenate([qit_ref[0, h * IDX_DIM:(h + 1) * IDX_DIM, :], zpad], axis=0)
    wt = kwq_ref[...].T

    q_pos = i * bq + lax.broadcasted_iota(I32, (bk, bq), 1)

    def score_tile(j, carry):
        kw = kwk_ref[pl.ds(pl.multiple_of(j * bk, bk), bk), :].astype(BF16)
        score = jnp.zeros((bk, bq), F32)
        for h in range(IDX_HEADS):
            logits = jnp.dot(kw, qiz_ref[h], preferred_element_type=F32)
            score = score + wt[IDX_DIM + h:IDX_DIM + h + 1, :] * jnp.maximum(logits, 0.0)
        bits = pltpu.bitcast(score, I32)
        key = jnp.where(bits < 0, bits ^ jnp.int32(0x7FFFFFFF), bits)
        key = jnp.where(bits == jnp.int32(INT_MIN), 0, key)
        key_pos = j * bk + lax.broadcasted_iota(I32, (bk, bq), 0)
        keys_ref[j] = jnp.where(key_pos <= q_pos, key, jnp.int32(INT_MIN))
        return carry

    lax.fori_loop(0, n_tiles, score_tile, 0)

    def count_ge(cand, strict):
        cand_b = jnp.broadcast_to(cand, (8, bq))

        def body(j, cnt):
            hits = []
            for g in range(ngrp):
                ks = keys_ref[j, g * 8:(g + 1) * 8, :]
                hit = (ks > cand_b) if strict else (ks >= cand_b)
                hits.append(jnp.where(hit, 1, 0))
            return cnt + _tree_sum(hits)

        cnt = lax.fori_loop(0, n_tiles, body, jnp.zeros((8, bq), I32))
        return jnp.sum(cnt, axis=0, keepdims=True)

    def bit_step(b, st):
        lo_u, c_lo, c_hi = st
        cand_u = lo_u | lax.shift_left(jnp.int32(1), 31 - b)
        total = count_ge(cand_u ^ jnp.int32(INT_MIN), False)
        up = total >= n_sel
        return jnp.where(up, cand_u, lo_u), jnp.where(up, total, c_lo), jnp.where(up, c_hi, total)

    st0 = (jnp.zeros((1, bq), I32), jnp.full((1, bq), n_tiles * bk, I32), jnp.zeros((1, bq), I32))
    lo_u, c_lo, c_hi = lax.fori_loop(0, SEARCH_BITS, bit_step, st0)
    w0 = jnp.full((1, bq), 1 << (32 - SEARCH_BITS), I32)

    def unfinished(st):
        _, c_lo, _, w, _ = st
        return jnp.max(jnp.where((c_lo == n_sel) | (w == 1), 0, 1)) > 0

    def refine(st):
        lo_u, c_lo, c_hi, w, it = st
        live = (c_lo != n_sel) & (w != 1)
        frac = (c_lo - n_sel).astype(F32) / jnp.maximum(c_lo - c_hi, 1).astype(F32)
        off = jnp.where(it % 2 == 0, (w.astype(F32) * frac).astype(I32), lax.shift_right_logical(w, 1))
        off = jnp.clip(off, 1, jnp.maximum(w - 1, 1))
        cand_u = lo_u + off
        total = count_ge(cand_u ^ jnp.int32(INT_MIN), False)
        up = live & (total >= n_sel)
        down = live & (total < n_sel)
        return (jnp.where(up, cand_u, lo_u), jnp.where(up, total, c_lo), jnp.where(down, total, c_hi),
                jnp.where(up, w - off, jnp.where(down, off, w)), it + 1)

    lo_u = lax.while_loop(unfinished, refine, (lo_u, c_lo, c_hi, w0, jnp.int32(0)))[0]
    thr = lo_u ^ jnp.int32(INT_MIN)
    n_gt = count_ge(thr, True)
    need = jnp.where(thr == jnp.int32(INT_MIN), 0, n_sel - n_gt).astype(F32)

    tri = tri_ref[...]

    def bias_tile(j, carry):
        kt = keys_ref[j]
        eq = kt == thr
        eq_f = jnp.where(eq, 1.0, 0.0)
        prefix = jnp.dot(tri, eq_f.astype(BF16), preferred_element_type=F32) + carry
        sel = (kt > thr) | (eq & (prefix <= need))
        bias_ref[j] = jnp.where(sel, 0.0, NEG_BIG)
        return prefix[bk - 1:bk, :]

    lax.fori_loop(0, n_tiles, bias_tile, jnp.zeros((1, bq), F32))

    m_ref[...] = jnp.full(m_ref.shape, NEG_BIG, F32)
    l_ref[...] = jnp.zeros(l_ref.shape, F32)
    acc_ref[...] = jnp.zeros(acc_ref.shape, F32)

    def attn_tile(j, carry):
        kt = k_ref[pl.ds(pl.multiple_of(j * bk, bk), bk), :]
        bias = bias_ref[j]
        alphas = []
        for h in range(N_HEADS):
            pair = slice((h // 2) * LANES, (h // 2 + 1) * LANES)
            s = jnp.dot(kt[:, pair], qz_ref[h], preferred_element_type=F32) + bias
            s_ref[h] = s
            m_prev = m_ref[h]
            m_new = jnp.maximum(m_prev, jnp.broadcast_to(jnp.max(s, axis=0, keepdims=True), (8, bq)))
            m_ref[h] = m_new
            alphas.append(jnp.exp(m_prev - m_new))
        for h in range(N_HEADS):
            p = jnp.exp(s_ref[h] - m_ref[h, 0:1, :])
            l_ref[h] = alphas[h] * l_ref[h] + _tree_sum([p[g * 8:(g + 1) * 8] for g in range(ngrp)])
            pv = jnp.dot(vt_ref[j, h * HEAD_DIM:(h + 1) * HEAD_DIM, :], p.astype(BF16),
                         preferred_element_type=F32)
            acc_ref[h] = alphas[h][0:1] * acc_ref[h] + pv
        return carry

    lax.fori_loop(0, n_tiles, attn_tile, 0)

    outs = [acc_ref[h] / jnp.sum(l_ref[h], axis=0, keepdims=True) for h in range(N_HEADS)]
    o_ref[...] = jnp.concatenate(outs, axis=0).T.astype(o_ref.dtype)


def _dsa_call(qt, qit, kw, k, vt, tri, batch, seq):
    nt, aw, bq = qt.shape
    assert bq == BQ and bq == BK
    nq = seq // bq
    n_sel = min(TOPK_ATTN, seq // 4)
    qtile = lambda b, i: (b * nq + i, 0)
    qtile_t = lambda b, i: (b * nq + i, 0, 0)
    whole = lambda b, i: (b, 0)
    kern = functools.partial(_dsa_kernel, n_sel=n_sel)
    return pl.pallas_call(
        kern,
        out_shape=jax.ShapeDtypeStruct((batch * seq, aw), BF16),
        grid=(batch, nq),
        in_specs=[
            pl.BlockSpec((1, aw, bq), qtile_t),
            pl.BlockSpec((1, aw, bq), qtile_t),
            pl.BlockSpec((bq, LANES), qtile),
            pl.BlockSpec((seq, aw), whole),
            pl.BlockSpec((seq // BK, aw, BK), lambda b, i: (b, 0, 0)),
            pl.BlockSpec((seq, LANES), whole),
            pl.BlockSpec(tri.shape, lambda b, i: (0, 0)),
        ],
        out_specs=pl.BlockSpec((bq, aw), qtile),
        scratch_shapes=[
            pltpu.VMEM((seq // BK, BK, bq), I32),
            pltpu.VMEM((seq // BK, BK, bq), F32),
            pltpu.VMEM((N_HEADS, LANES, bq), BF16),
            pltpu.VMEM((IDX_HEADS, LANES, bq), BF16),
            pltpu.VMEM((N_HEADS, BK, bq), F32),
            pltpu.VMEM((N_HEADS, 8, bq), F32),
            pltpu.VMEM((N_HEADS, 8, bq), F32),
            pltpu.VMEM((N_HEADS, HEAD_DIM, bq), F32),
        ],
        compiler_params=pltpu.CompilerParams(
            dimension_semantics=("arbitrary", "arbitrary"), vmem_limit_bytes=VMEM_LIMIT),
        name="dsa_attention",
    )(qt, qit, kw, k, vt, kw, tri)


def _post_kernel(x_ref, attn_ref, gyc_ref, sga_ref, mod_ref, wao_ref, wo_ref, o_ref):
    g1 = mod_ref[0, 2:3, :]
    y_attn = jnp.dot(attn_ref[...], wao_ref[...], preferred_element_type=F32)
    mix = gyc_ref[...] + sga_ref[...] * y_attn
    delta = jnp.dot(mix.astype(BF16), wo_ref[...], preferred_element_type=F32)
    o_ref[...] = x_ref[...] + g1 * delta


def _post_call(xf, attn, gyc, sga, mod3, wao, wo, seq):
    t, d = xf.shape
    bt = BT_PROJ
    tiles_per_seq = seq // bt
    aw = attn.shape[1]
    tile = lambda i: (i, 0)
    const = lambda i: (0, 0)
    return pl.pallas_call(
        _post_kernel,
        out_shape=jax.ShapeDtypeStruct((t, d), F32),
        grid=(t // bt,),
        in_specs=[
            pl.BlockSpec((bt, d), tile),
            pl.BlockSpec((bt, aw), tile),
            pl.BlockSpec((bt, d), tile),
            pl.BlockSpec((bt, d), tile),
            pl.BlockSpec((1, N_MOD, d), lambda i: (i // tiles_per_seq, 0, 0)),
            pl.BlockSpec(wao.shape, const),
            pl.BlockSpec(wo.shape, const),
        ],
        out_specs=pl.BlockSpec((bt, d), tile),
        compiler_params=pltpu.CompilerParams(
            dimension_semantics=("arbitrary",), vmem_limit_bytes=VMEM_LIMIT),
        name="post_mix",
    )(xf, attn, gyc, sga, mod3, wao, wo)


def _col_max(a):
    return jnp.max(a, axis=0, keepdims=True)


def _batcher_pairs(n):
    pairs = []
    p = 1
    while p < n:
        k = p
        while k >= 1:
            for j in range(k % p, n - k, 2 * k):
                for i in range(min(k, n - j - k)):
                    if (i + j) // (2 * p) == (i + j + k) // (2 * p):
                        pairs.append((i + j, i + j + k))
            k //= 2
        p *= 2
    return pairs


_SORT16 = _batcher_pairs(PEER_TOPK)
_BITONIC16 = [(k, k + d) for d in (8, 4, 2, 1) for k in range(PEER_TOPK) if not k & d]


def _sort_desc(v, pairs):
    v = list(v)
    for i, j in pairs:
        v[i], v[j] = jnp.maximum(v[i], v[j]), jnp.minimum(v[i], v[j])
    return v


def _peer_kernel(x_ref, mod_ref, n2_ref, wqt_ref, sk_ref, u_ref, vt_ref, o_ref,
                 h2t_ref, s1_ref, s2_ref, ap_ref, bp_ref, nsel_ref, rk_ref, w_ref, acc_ref):
    e = pl.program_id(1)
    n_e = pl.num_programs(1)
    tt = x_ref.shape[0]
    nk = PEER_N_KEYS
    n_i = u_ref.shape[0] // nk
    nslab = tt // LANES

    @pl.when(e == 0)
    def _route():
        xv = x_ref[...]
        sh2 = mod_ref[0, 3:4, :]
        sc2 = mod_ref[0, 4:5, :]
        ms = jnp.mean(xv * xv, axis=-1, keepdims=True)
        h2t = ((xv * lax.rsqrt(ms + EPS) * n2_ref[...]) * (1.0 + sc2) + sh2).T.astype(BF16)
        h2t_ref[...] = h2t
        qt = jnp.dot(wqt_ref[...], h2t, preferred_element_type=F32).astype(BF16)
        for h in range(PEER_HEADS):
            for p, dst in ((0, s1_ref), (1, s2_ref)):
                r0 = (h * 2 + p) * PEER_HALF
                dst[h] = jnp.dot(sk_ref[h, p], qt[r0:r0 + PEER_HALF, :],
                                 preferred_element_type=F32)

        neg_inf = jnp.float32(-jnp.inf)
        sub_id = lax.broadcasted_iota(I32, (8, LANES), 0)

        def top16(s):
            v = _sort_desc([s[g * 8:(g + 1) * 8] for g in range(nk // 8)], _SORT16)
            for shift in (4, 2, 1):
                v = [jnp.maximum(v[r], pltpu.roll(v[PEER_TOPK - 1 - r], shift, 0)) for r in range(PEER_TOPK)]
                v = _sort_desc(v, _BITONIC16)
            return v

        def stack8(vals):
            out = vals[0]
            for r in range(1, 8):
                out = jnp.where(sub_id == r, vals[r], out)
            return out

        def route_head(h, carry):
            for c in range(nslab):
                ls = slice(c * LANES, (c + 1) * LANES)
                s1 = s1_ref[h, :, ls]
                s2 = s2_ref[h, :, ls]
                a = top16(s1)
                b = top16(s2)
                b_lo, b_hi = stack8(b[:8]), stack8(b[8:])
                cands = [a[0] + b_lo, a[0] + b_hi] + [a[r] + b_lo for r in range(1, PEER_TOPK)]
                work, t = cands, []
                for _ in range(PEER_TOPK):
                    m = _col_max(functools.reduce(jnp.maximum, work))
                    t.append(m)
                    work = [jnp.where(w == m, neg_inf, w) for w in work]
                thr = t[PEER_TOPK - 1]
                z = functools.reduce(jnp.add, [jnp.exp(tk - t[0]) for tk in t])
                hits = [jnp.where(cd >= thr, 1.0, 0.0) for cd in cands]
                n_sel = [jnp.sum(hits[0] + hits[1], axis=0, keepdims=True)]
                n_sel += [jnp.sum(hits[r + 1], axis=0, keepdims=True) for r in range(1, PEER_TOPK)]
                n_b = [jnp.broadcast_to(n, (8, LANES)) for n in n_sel]
                cnts, ranks = [], []
                for g in range(nk // 8):
                    rows = slice(g * 8, (g + 1) * 8)
                    s1g, s2g = s1[rows], s2[rows]
                    cnt = jnp.zeros((8, LANES), F32)
                    rank = jnp.ones((8, LANES), F32)
                    for r in range(PEER_TOPK):
                        cnt = jnp.where(s1g == a[r], n_b[r], cnt)
                        rank = rank + jnp.where(b[r] > s2g, 1.0, 0.0)
                    cnts.append(cnt)
                    ranks.append(rank)
                nsel_ref[h, :, ls] = jnp.concatenate(cnts, axis=0)
                rk_ref[h, :, ls] = jnp.concatenate(ranks, axis=0).astype(BF16)
                ap_ref[h, :, ls] = jnp.exp(s1 - a[0][0:1])
                bp_ref[h, :, ls] = (jnp.exp(s2 - b[0][0:1]) / z).astype(BF16)
            return carry

        lax.fori_loop(0, PEER_HEADS, route_head, 0)
        acc_ref[...] = jnp.zeros(acc_ref.shape, F32)

    irow0 = pl.multiple_of(e * n_i, 8)
    pk = 16
    for c in range(nslab):
        ls = slice(c * LANES, (c + 1) * LANES)
        n_blk = [nsel_ref[h, pl.ds(irow0, n_i), ls] for h in range(PEER_HEADS)]
        ap_blk = [ap_ref[h, pl.ds(irow0, n_i), ls] for h in range(PEER_HEADS)]
        for ii in range(n_i):
            accs = [None] * (nk // pk)
            for h in range(PEER_HEADS):
                n16 = jnp.broadcast_to(n_blk[h][ii:ii + 1], (pk, LANES)).astype(BF16)
                a16 = jnp.broadcast_to(ap_blk[h][ii:ii + 1], (pk, LANES)).astype(BF16)
                for g in range(nk // pk):
                    rows = slice(g * pk, (g + 1) * pk)
                    hit = rk_ref[h, rows, ls] <= n16
                    term = jnp.where(hit, bp_ref[h, rows, ls], jnp.zeros((), BF16)) * a16
                    accs[g] = term if h == 0 else accs[g] + term
            for g in range(nk // pk):
                w_ref[ii * nk + g * pk:ii * nk + (g + 1) * pk, ls] = accs[g]

    act = jnp.dot(u_ref[...], h2t_ref[...], preferred_element_type=F32)
    gelu = 0.5 * act * (1.0 + lax.erf(act * (2.0 ** -0.5)))
    wa = (w_ref[...].astype(F32) * gelu).astype(BF16)
    acc_ref[...] += jnp.dot(vt_ref[0], wa, preferred_element_type=F32)

    @pl.when(e == n_e - 1)
    def _finish():
        g2 = mod_ref[0, 5:6, :]
        o_ref[...] = x_ref[...] + g2 * acc_ref[...].T


def _peer_call(x1, mod3, n2, wqt, sk, u, vt, seq):
    t, d = x1.shape
    tt = TT_PEER
    ec = EC_PEER
    n_chunks = u.shape[0] // ec
    assert ec % (8 * PEER_N_KEYS) == 0
    tiles_per_seq = seq // tt
    tile = lambda i, e: (i, 0)
    const2 = lambda i, e: (0, 0)
    return pl.pallas_call(
        _peer_kernel,
        out_shape=jax.ShapeDtypeStruct((t, d), F32),
        grid=(t // tt, n_chunks),
        in_specs=[
            pl.BlockSpec((tt, d), tile),
            pl.BlockSpec((1, N_MOD, d), lambda i, e: (i // tiles_per_seq, 0, 0)),
            pl.BlockSpec((1, d), const2),
            pl.BlockSpec(wqt.shape, const2),
            pl.BlockSpec(sk.shape, lambda i, e: (0, 0, 0, 0)),
            pl.BlockSpec((ec, d), lambda i, e: (e, 0)),
            pl.BlockSpec((1, d, ec), lambda i, e: (e, 0, 0)),
        ],
        out_specs=pl.BlockSpec((tt, d), tile),
        scratch_shapes=[
            pltpu.VMEM((d, tt), BF16),
            pltpu.VMEM((PEER_HEADS, PEER_N_KEYS, tt), F32),
            pltpu.VMEM((PEER_HEADS, PEER_N_KEYS, tt), F32),
            pltpu.VMEM((PEER_HEADS, PEER_N_KEYS, tt), F32),
            pltpu.VMEM((PEER_HEADS, PEER_N_KEYS, tt), BF16),
            pltpu.VMEM((PEER_HEADS, PEER_N_KEYS, tt), F32),
            pltpu.VMEM((PEER_HEADS, PEER_N_KEYS, tt), BF16),
            pltpu.VMEM((ec, tt), BF16),
            pltpu.VMEM((d, tt), F32),
        ],
        compiler_params=pltpu.CompilerParams(
            dimension_semantics=("arbitrary", "arbitrary"), vmem_limit_bytes=VMEM_LIMIT),
        name="peer",
    )(x1, mod3, n2, wqt, sk, u, vt)


def kernel(x, c, w_ada, b_ada, norm1_w, w_in, conv_w, w_conv_out, q_norm_w, k_norm_w,
           w_attn_out, w_o, norm2_w, w_peer_q, peer_sub_keys, peer_u, peer_v):
    b, s, d = x.shape
    depth = w_ada.shape[0]
    cw = conv_w.shape[2]
    aw = N_HEADS * HEAD_DIM
    iw = IDX_HEADS * IDX_DIM
    assert s % BQ == 0 and s % TT_PEER == 0 and s % BT_PROJ == 0
    assert peer_u.shape[1] == PEER_N_KEYS * PEER_N_KEYS

    gidx = jnp.arange(aw) // HEAD_DIM
    gmat = (gidx[:, None] == gidx[None, :]).astype(BF16)
    tri = (jnp.arange(BK)[:, None] >= jnp.arange(BK)[None, :]).astype(BF16)

    xf = x.reshape(b * s, d)
    for layer in range(depth):
        w = w_in[layer]
        o = 0
        wconv = w[:, o:o + 3 * cw].astype(BF16); o += 3 * cw
        wqkv = w[:, o:o + 3 * aw].astype(BF16); o += 3 * aw
        wqi = w[:, o:o + iw].astype(BF16); o += iw
        wkw = jnp.pad(w[:, o:o + IDX_DIM + IDX_HEADS], ((0, 0), (0, LANES - IDX_DIM - IDX_HEADS))).astype(BF16)
        o += IDX_DIM + IDX_HEADS
        wg = w[:, o:o + 2 * d].astype(BF16)
        qn_t = jnp.tile(q_norm_w[layer], N_HEADS).reshape(1, aw)
        kn_t = jnp.tile(k_norm_w[layer], N_HEADS).reshape(1, aw)

        mod3 = _mod_call(c, w_ada[layer], b_ada[layer]).reshape(b, N_MOD, d)
        gyc, sga, qt, k, vt, qit, kw = _inproj_call(
            xf, mod3, norm1_w[layer].reshape(1, d), wconv, wqkv, wqi, wkw, wg, conv_w[layer],
            w_conv_out[layer].astype(BF16), qn_t, kn_t, gmat, s)
        attn = _dsa_call(qt, qit, kw, k, vt, tri, b, s)
        x1 = _post_call(xf, attn, gyc, sga, mod3, w_attn_out[layer].astype(BF16),
                        w_o[layer].astype(BF16), s)
        xf = _peer_call(
            x1, mod3, norm2_w[layer].reshape(1, d), w_peer_q[layer].T.astype(BF16),
            peer_sub_keys[layer].astype(BF16), peer_u[layer].astype(BF16),
            peer_v[layer].astype(BF16).reshape(-1, EC_PEER, d).transpose(0, 2, 1), s)
    return xf.reshape(b, s, d)
```

```python
import functools

import jax
import jax.numpy as jnp
from jax import lax
from jax.experimental import pallas as pl
from jax.experimental.pallas import tpu as pltpu

F32 = jnp.float32
BF16 = jnp.bfloat16
I32 = jnp.int32

N_HEADS = 8
HEAD_DIM = 64
IDX_HEADS = 8
IDX_DIM = 64
TOPK_ATTN = 256
PEER_HEADS = 8
PEER_N_KEYS = 128
PEER_HALF = 64
PEER_TOPK = 16
CONV_K = 3
N_MOD = 6
EPS = 1e-6

LANES = 128
INT_MIN = -(2 ** 31)
NEG_BIG = -1e30
LOG2E = 1.4426950408889634
VMEM_LIMIT = 56 * 1024 * 1024

BT_PROJ = 512
BQ = 512
BK = 256
TT_PEER = 512
EC_PEER = 2048


def _split_bf16(a):
    hi = a.astype(BF16)
    lo = (a - hi.astype(F32)).astype(BF16)
    return hi, lo


def _nt_dot(a, b):
    return lax.dot_general(a, b, (((1,), (1,)), ((), ())), preferred_element_type=F32)


def _mod_kernel(c_ref, w_ref, b_ref, o_ref):
    c = c_ref[...]
    s = c * jax.nn.sigmoid(c)
    s_hi, s_lo = _split_bf16(s)
    w_hi, w_lo = _split_bf16(w_ref[...])
    acc = jnp.dot(s_hi, w_hi, preferred_element_type=F32)
    acc += jnp.dot(s_hi, w_lo, preferred_element_type=F32)
    acc += jnp.dot(s_lo, w_hi, preferred_element_type=F32)
    o_ref[...] = acc + b_ref[...]


def _mod_call(c, w_ada, b_ada):
    b, d = c.shape
    n = w_ada.shape[1]
    bn = d
    return pl.pallas_call(
        _mod_kernel,
        out_shape=jax.ShapeDtypeStruct((b, n), F32),
        grid=(n // bn,),
        in_specs=[
            pl.BlockSpec((b, d), lambda j: (0, 0)),
            pl.BlockSpec((d, bn), lambda j: (0, j)),
            pl.BlockSpec((1, bn), lambda j: (0, j)),
        ],
        out_specs=pl.BlockSpec((b, bn), lambda j: (0, j)),
        compiler_params=pltpu.CompilerParams(
            dimension_semantics=("arbitrary",), vmem_limit_bytes=VMEM_LIMIT),
        name="adaln_mod",
    )(c, w_ada, b_ada.reshape(1, n))


def _inproj_kernel(x_ref, xh_ref, mod_ref, n1_ref, wconv_ref, wqkv_ref, wqi_ref, wkw_ref,
                   wg_ref, convw_ref, wco_ref, qn_ref, kn_ref, gmat_ref,
                   gyc_ref, sga_ref, q_ref, k_ref, v_ref, qi_ref, kw_ref, *, tiles_per_seq):
    i = pl.program_id(0)
    cw = convw_ref.shape[1]
    aw = q_ref.shape[1]
    d = x_ref.shape[1]
    bt = x_ref.shape[0]
    sh1 = mod_ref[0, 0:1, :]
    sc1 = mod_ref[0, 1:2, :]

    def norm_mod(xv):
        ms = jnp.mean(xv * xv, axis=-1, keepdims=True)
        y = xv * lax.rsqrt(ms + EPS) * n1_ref[...]
        return (y * (1.0 + sc1) + sh1).astype(BF16)

    h = norm_mod(x_ref[...])
    hh = norm_mod(xh_ref[...])

    pc = jnp.dot(h, wconv_ref[...], preferred_element_type=F32)
    cb, cc, cx = pc[:, :cw], pc[:, cw:2 * cw], pc[:, 2 * cw:]
    u = cc * cx
    ph = jnp.dot(hh, wconv_ref[:, cw:], preferred_element_type=F32)
    uh = ph[:, :cw] * ph[:, cw:]
    uh = jnp.where(i % tiles_per_seq == 0, 0.0, uh)
    row = lax.broadcasted_iota(I32, (bt, cw), 0)
    u1 = jnp.where(row == 0, uh[7:8, :], pltpu.roll(u, 1, 0))
    u2 = pltpu.roll(u, 2, 0)
    u2 = jnp.where(row == 0, uh[6:7, :], jnp.where(row == 1, uh[7:8, :], u2))
    conv = convw_ref[0:1, :] * u2 + convw_ref[1:2, :] * u1 + convw_ref[2:3, :] * u
    yc = jnp.dot((cb * conv).astype(BF16), wco_ref[...], preferred_element_type=F32)

    pg = jnp.dot(h, wg_ref[...], preferred_element_type=F32)
    gyc_ref[...] = (jax.nn.sigmoid(pg[:, :d]) * yc).astype(gyc_ref.dtype)
    sga_ref[...] = jax.nn.sigmoid(pg[:, d:]).astype(sga_ref.dtype)

    pq = jnp.dot(h, wqkv_ref[...], preferred_element_type=F32)
    gmat = gmat_ref[...]

    def head_norm(t, w_row):
        hi, lo = _split_bf16(t * t)
        ss = (jnp.dot(hi, gmat, preferred_element_type=F32)
              + jnp.dot(lo, gmat, preferred_element_type=F32))
        return t * lax.rsqrt(ss * (1.0 / HEAD_DIM) + EPS) * w_row

    qn = head_norm(pq[:, :aw], qn_ref[...])
    qs = qn * (HEAD_DIM ** -0.5 * LOG2E)
    k_ref[...] = head_norm(pq[:, aw:2 * aw], kn_ref[...]).astype(BF16)
    vv = pq[:, 2 * aw:]
    qi = jnp.dot(h, wqi_ref[...], preferred_element_type=F32)
    tw = q_ref.shape[2]
    for s in range(bt // tw):
        rows = slice(s * tw, (s + 1) * tw)
        q_ref[s] = qs[rows].T.astype(BF16)
        v_ref[s] = vv[rows].T.astype(BF16)
        qi_ref[s] = qi[rows].T.astype(BF16)
    kw_ref[...] = jnp.dot(h, wkw_ref[...], preferred_element_type=F32)


def _inproj_call(xf, mod3, n1, wconv, wqkv, wqi, wkw, wg, convw, wco, qn_t, kn_t, gmat, seq):
    t, d = xf.shape
    bt = BT_PROJ
    tw = BK
    tiles_per_seq = seq // bt
    cw = convw.shape[1]
    aw = qn_t.shape[1]
    const = lambda i: (0, 0)
    tile = lambda i: (i, 0)
    tile_t = lambda i: (i, 0, 0)
    resident = lambda a: pl.BlockSpec(a.shape, const, pipeline_mode=pl.Buffered(1))
    halo = lambda i: (jnp.maximum(i * (bt // 8) - 1, 0), 0)
    kern = functools.partial(_inproj_kernel, tiles_per_seq=tiles_per_seq)
    return pl.pallas_call(
        kern,
        out_shape=[
            jax.ShapeDtypeStruct((t, d), BF16),
            jax.ShapeDtypeStruct((t, d), BF16),
            jax.ShapeDtypeStruct((t // tw, aw, tw), BF16),
            jax.ShapeDtypeStruct((t, aw), BF16),
            jax.ShapeDtypeStruct((t // tw, aw, tw), BF16),
            jax.ShapeDtypeStruct((t // tw, aw, tw), BF16),
            jax.ShapeDtypeStruct((t, LANES), F32),
        ],
        grid=(t // bt,),
        in_specs=[
            pl.BlockSpec((bt, d), tile),
            pl.BlockSpec((8, d), halo),
            pl.BlockSpec((1, N_MOD, d), lambda i: (i // tiles_per_seq, 0, 0)),
            pl.BlockSpec((1, d), const),
            resident(wconv),
            resident(wqkv),
            resident(wqi),
            resident(wkw),
            resident(wg),
            pl.BlockSpec(convw.shape, const),
            resident(wco),
            pl.BlockSpec((1, aw), const),
            pl.BlockSpec((1, aw), const),
            resident(gmat),
        ],
        out_specs=[
            pl.BlockSpec((bt, d), tile),
            pl.BlockSpec((bt, d), tile),
            pl.BlockSpec((bt // tw, aw, tw), tile_t),
            pl.BlockSpec((bt, aw), tile),
            pl.BlockSpec((bt // tw, aw, tw), tile_t),
            pl.BlockSpec((bt // tw, aw, tw), tile_t),
            pl.BlockSpec((bt, LANES), tile),
        ],
        compiler_params=pltpu.CompilerParams(
            dimension_semantics=("arbitrary",), vmem_limit_bytes=VMEM_LIMIT),
        name="inproj",
    )(xf, xf, mod3, n1, wconv, wqkv, wqi, wkw, wg, convw, wco, qn_t, kn_t, gmat)


def _tree_sum(parts):
    parts = list(parts)
    while len(parts) > 1:
        nxt = [parts[a] + parts[a + 1] for a in range(0, len(parts) - 1, 2)]
        if len(parts) % 2:
            nxt.append(parts[-1])
        parts = nxt
    return parts[0]


def _dsa_kernel(qt_ref, qit_ref, kwq_ref, k_ref, vt_ref, kwk_ref, tri_ref, o_ref,
                keys_ref, bias_ref, qz_ref, qiz_ref, s_ref, m_ref, l_ref, acc_ref, *, n_sel):
    i = pl.program_id(1)
    n_qt, _, tw = qt_ref.shape
    bq = n_qt * tw
    bk = BK
    n_tiles = (i * bq) // bk + bq // bk
    ngrp = bk // 8

    zpad = jnp.zeros((LANES - HEAD_DIM, bq), BF16)
    for h in range(N_HEADS):
        qh = jnp.concatenate([qt_ref[s, h * HEAD_DIM:(h + 1) * HEAD_DIM, :] for s in range(n_qt)], axis=1)
        qih = jnp.concatenate([qit_ref[s, h * IDX_DIM:(h + 1) * IDX_DIM, :] for s in range(n_qt)], axis=1)
        qz_ref[h] = jnp.concatenate([qh, zpad] if h % 2 == 0 else [zpad, qh], axis=0)
        qiz_ref[h] = jnp.concatenate([qih, zpad], axis=0)
    wt = kwq_ref[...].T

    q_pos = i * bq + lax.broadcasted_iota(I32, (bk, bq), 1)

    def score_tile(j, carry):
        kw = kwk_ref[pl.ds(pl.multiple_of(j * bk, bk), bk), :].astype(BF16)
        score = jnp.zeros((bk, bq), F32)
        for h in range(IDX_HEADS):
            logits = jnp.dot(kw, qiz_ref[h], preferred_element_type=F32)
            score = score + wt[IDX_DIM + h:IDX_DIM + h + 1, :] * jnp.maximum(logits, 0.0)
        bits = pltpu.bitcast(score, I32)
        key = jnp.where(bits < 0, bits ^ jnp.int32(0x7FFFFFFF), bits)
        key = jnp.where(bits == jnp.int32(INT_MIN), 0, key)
        key_pos = j * bk + lax.broadcasted_iota(I32, (bk, bq), 0)
        keys_ref[j] = jnp.where(key_pos <= q_pos, key, jnp.int32(INT_MIN))
        return carry

    lax.fori_loop(0, n_tiles, score_tile, 0)

    def count_ge(cand, strict):
        cand_b = jnp.broadcast_to(cand, (8, bq))

        def body(j, cnt):
            hits = []
            for g in range(ngrp):
                ks = keys_ref[j, g * 8:(g + 1) * 8, :]
                hit = (ks > cand_b) if strict else (ks >= cand_b)
                hits.append(jnp.where(hit, 1, 0))
            return cnt + _tree_sum(hits)

        cnt = lax.fori_loop(0, n_tiles, body, jnp.zeros((8, bq), I32))
        return jnp.sum(cnt, axis=0, keepdims=True)

    def bit_step(b, t_u):
        cand_u = t_u | lax.shift_left(jnp.int32(1), 31 - b)
        total = count_ge(cand_u ^ jnp.int32(INT_MIN), False)
        return jnp.where(total >= n_sel, cand_u, t_u)

    t_u = lax.fori_loop(0, 32, bit_step, jnp.zeros((1, bq), I32))
    thr = t_u ^ jnp.int32(INT_MIN)
    n_gt = count_ge(thr, True)
    need = jnp.where(thr == jnp.int32(INT_MIN), 0, n_sel - n_gt).astype(F32)

    tri = tri_ref[...]

    def bias_tile(j, carry):
        kt = keys_ref[j]
        eq = kt == thr
        eq_f = jnp.where(eq, 1.0, 0.0)
        prefix = jnp.dot(tri, eq_f.astype(BF16), preferred_element_type=F32) + carry
        sel = (kt > thr) | (eq & (prefix <= need))
        bias_ref[j] = jnp.where(sel, 0.0, NEG_BIG)
        return prefix[bk - 1:bk, :]

    lax.fori_loop(0, n_tiles, bias_tile, jnp.zeros((1, bq), F32))

    m_ref[...] = jnp.full(m_ref.shape, NEG_BIG, F32)
    l_ref[...] = jnp.zeros(l_ref.shape, F32)
    acc_ref[...] = jnp.zeros(acc_ref.shape, F32)

    def attn_tile(j, carry):
        kt = k_ref[pl.ds(pl.multiple_of(j * bk, bk), bk), :]
        bias = bias_ref[j]
        alphas = []
        for h in range(N_HEADS):
            pair = slice((h // 2) * LANES, (h // 2 + 1) * LANES)
            s = jnp.dot(kt[:, pair], qz_ref[h], preferred_element_type=F32) + bias
            s_ref[h] = s
            m_prev = m_ref[h]
            m_new = jnp.maximum(m_prev, jnp.broadcast_to(jnp.max(s, axis=0, keepdims=True), (8, bq)))
            m_ref[h] = m_new
            alphas.append(jnp.exp2(m_prev - m_new))
        for h in range(N_HEADS):
            p = jnp.exp2(s_ref[h] - m_ref[h, 0:1, :])
            l_ref[h] = alphas[h] * l_ref[h] + _tree_sum([p[g * 8:(g + 1) * 8] for g in range(ngrp)])
            pv = jnp.dot(vt_ref[j, h * HEAD_DIM:(h + 1) * HEAD_DIM, :], p.astype(BF16),
                         preferred_element_type=F32)
            acc_ref[h] = alphas[h][0:1] * acc_ref[h] + pv
        return carry

    lax.fori_loop(0, n_tiles, attn_tile, 0)

    outs = [acc_ref[h] / jnp.sum(l_ref[h], axis=0, keepdims=True) for h in range(N_HEADS)]
    o_ref[...] = jnp.concatenate(outs, axis=0).T.astype(o_ref.dtype)


def _dsa_call(qt, qit, kw, k, vt, tri, batch, seq):
    nt, aw, tw = qt.shape
    bq = BQ
    assert tw == BK and bq % tw == 0 and seq % bq == 0
    nq = seq // bq
    n_sel = min(TOPK_ATTN, seq // 4)
    qtile = lambda b, i: (b * nq + i, 0)
    qtile_t = lambda b, i: (b * nq + i, 0, 0)
    whole = lambda b, i: (b, 0)
    kern = functools.partial(_dsa_kernel, n_sel=n_sel)
    return pl.pallas_call(
        kern,
        out_shape=jax.ShapeDtypeStruct((batch * seq, aw), BF16),
        grid=(batch, nq),
        in_specs=[
            pl.BlockSpec((bq // tw, aw, tw), qtile_t),
            pl.BlockSpec((bq // tw, aw, tw), qtile_t),
            pl.BlockSpec((bq, LANES), qtile),
            pl.BlockSpec((seq, aw), whole),
            pl.BlockSpec((seq // BK, aw, BK), lambda b, i: (b, 0, 0)),
            pl.BlockSpec((seq, LANES), whole),
            pl.BlockSpec(tri.shape, lambda b, i: (0, 0)),
        ],
        out_specs=pl.BlockSpec((bq, aw), qtile),
        scratch_shapes=[
            pltpu.VMEM((seq // BK, BK, bq), I32),
            pltpu.VMEM((seq // BK, BK, bq), F32),
            pltpu.VMEM((N_HEADS, LANES, bq), BF16),
            pltpu.VMEM((IDX_HEADS, LANES, bq), BF16),
            pltpu.VMEM((N_HEADS, BK, bq), F32),
            pltpu.VMEM((N_HEADS, 8, bq), F32),
            pltpu.VMEM((N_HEADS, 8, bq), F32),
            pltpu.VMEM((N_HEADS, HEAD_DIM, bq), F32),
        ],
        compiler_params=pltpu.CompilerParams(
            dimension_semantics=("arbitrary", "arbitrary"), vmem_limit_bytes=VMEM_LIMIT),
        name="dsa_attention",
    )(qt, qit, kw, k, vt, kw, tri)


def _post_kernel(x_ref, attn_ref, gyc_ref, sga_ref, mod_ref, wao_ref, wo_ref, o_ref):
    g1 = mod_ref[0, 2:3, :]
    y_attn = jnp.dot(attn_ref[...], wao_ref[...], preferred_element_type=F32)
    mix = gyc_ref[...] + sga_ref[...] * y_attn
    delta = jnp.dot(mix.astype(BF16), wo_ref[...], preferred_element_type=F32)
    o_ref[...] = x_ref[...] + g1 * delta


def _post_call(xf, attn, gyc, sga, mod3, wao, wo, seq):
    t, d = xf.shape
    bt = BT_PROJ
    tiles_per_seq = seq // bt
    aw = attn.shape[1]
    tile = lambda i: (i, 0)
    const = lambda i: (0, 0)
    return pl.pallas_call(
        _post_kernel,
        out_shape=jax.ShapeDtypeStruct((t, d), F32),
        grid=(t // bt,),
        in_specs=[
            pl.BlockSpec((bt, d), tile),
            pl.BlockSpec((bt, aw), tile),
            pl.BlockSpec((bt, d), tile),
            pl.BlockSpec((bt, d), tile),
            pl.BlockSpec((1, N_MOD, d), lambda i: (i // tiles_per_seq, 0, 0)),
            pl.BlockSpec(wao.shape, const),
            pl.BlockSpec(wo.shape, const),
        ],
        out_specs=pl.BlockSpec((bt, d), tile),
        compiler_params=pltpu.CompilerParams(
            dimension_semantics=("arbitrary",), vmem_limit_bytes=VMEM_LIMIT),
        name="post_mix",
    )(xf, attn, gyc, sga, mod3, wao, wo)


def _col_max(a):
    return jnp.max(a, axis=0, keepdims=True)


def _batcher_pairs(n):
    pairs = []
    p = 1
    while p < n:
        k = p
        while k >= 1:
            for j in range(k % p, n - k, 2 * k):
                for i in range(min(k, n - j - k)):
                    if (i + j) // (2 * p) == (i + j + k) // (2 * p):
                        pairs.append((i + j, i + j + k))
            k //= 2
        p *= 2
    return pairs


_SORT16 = _batcher_pairs(PEER_TOPK)
_BITONIC16 = [(k, k + d) for d in (8, 4, 2, 1) for k in range(PEER_TOPK) if not k & d]


def _sort_desc(v, pairs):
    v = list(v)
    for i, j in pairs:
        v[i], v[j] = jnp.maximum(v[i], v[j]), jnp.minimum(v[i], v[j])
    return v


def _peer_kernel(x_ref, mod_ref, n2_ref, wqt_ref, sk_ref, u_ref, vt_ref, o_ref,
                 h2t_ref, s1_ref, s2_ref, ap_ref, bp_ref, nsel_ref, rk_ref, w_ref, acc_ref):
    e = pl.program_id(1)
    n_e = pl.num_programs(1)
    tt = x_ref.shape[0]
    nk = PEER_N_KEYS
    n_i = u_ref.shape[0] // nk
    nslab = tt // LANES

    @pl.when(e == 0)
    def _route():
        xv = x_ref[...]
        sh2 = mod_ref[0, 3:4, :]
        sc2 = mod_ref[0, 4:5, :]
        ms = jnp.mean(xv * xv, axis=-1, keepdims=True)
        h2t = ((xv * lax.rsqrt(ms + EPS) * n2_ref[...]) * (1.0 + sc2) + sh2).T.astype(BF16)
        h2t_ref[...] = h2t
        qt = jnp.dot(wqt_ref[...], h2t, preferred_element_type=F32).astype(BF16)
        for h in range(PEER_HEADS):
            for p, dst in ((0, s1_ref), (1, s2_ref)):
                r0 = (h * 2 + p) * PEER_HALF
                dst[h] = jnp.dot(sk_ref[h, p], qt[r0:r0 + PEER_HALF, :],
                                 preferred_element_type=F32)

        neg_inf = jnp.float32(-jnp.inf)
        sub_id = lax.broadcasted_iota(I32, (8, LANES), 0)

        def top16(s):
            v = _sort_desc([s[g * 8:(g + 1) * 8] for g in range(nk // 8)], _SORT16)
            for shift in (4, 2, 1):
                v = [jnp.maximum(v[r], pltpu.roll(v[PEER_TOPK - 1 - r], shift, 0)) for r in range(PEER_TOPK)]
                v = _sort_desc(v, _BITONIC16)
            return v

        def stack8(vals):
            out = vals[0]
            for r in range(1, 8):
                out = jnp.where(sub_id == r, vals[r], out)
            return out

        def route_head(h, carry):
            for c in range(nslab):
                ls = slice(c * LANES, (c + 1) * LANES)
                s1 = s1_ref[h, :, ls]
                s2 = s2_ref[h, :, ls]
                a = top16(s1)
                b = top16(s2)
                b_lo, b_hi = stack8(b[:8]), stack8(b[8:])
                cands = [a[0] + b_lo, a[0] + b_hi] + [a[r] + b_lo for r in range(1, PEER_TOPK)]
                work, t = cands, []
                for _ in range(PEER_TOPK):
                    m = _col_max(functools.reduce(jnp.maximum, work))
                    t.append(m)
                    work = [jnp.where(w == m, neg_inf, w) for w in work]
                thr = t[PEER_TOPK - 1]
                z = functools.reduce(jnp.add, [jnp.exp(tk - t[0]) for tk in t])
                hits = [jnp.where(cd >= thr, 1.0, 0.0) for cd in cands]
                n_sel = [jnp.sum(hits[0] + hits[1], axis=0, keepdims=True)]
                n_sel += [jnp.sum(hits[r + 1], axis=0, keepdims=True) for r in range(1, PEER_TOPK)]
                n_b = [jnp.broadcast_to(n, (8, LANES)) for n in n_sel]
                cnts, ranks = [], []
                for g in range(nk // 8):
                    rows = slice(g * 8, (g + 1) * 8)
                    s1g, s2g = s1[rows], s2[rows]
                    cnt = jnp.zeros((8, LANES), F32)
                    rank = jnp.ones((8, LANES), F32)
                    for r in range(PEER_TOPK):
                        cnt = jnp.where(s1g == a[r], n_b[r], cnt)
                        rank = rank + jnp.where(b[r] > s2g, 1.0, 0.0)
                    cnts.append(cnt)
                    ranks.append(rank)
                nsel_ref[h, :, ls] = jnp.concatenate(cnts, axis=0)
                rk_ref[h, :, ls] = jnp.concatenate(ranks, axis=0).astype(BF16)
                ap_ref[h, :, ls] = jnp.exp(s1 - a[0][0:1])
                bp_ref[h, :, ls] = (jnp.exp(s2 - b[0][0:1]) / z).astype(BF16)
            return carry

        lax.fori_loop(0, PEER_HEADS, route_head, 0)
        acc_ref[...] = jnp.zeros(acc_ref.shape, F32)

    irow0 = pl.multiple_of(e * n_i, 8)
    pk = 16
    for c in range(nslab):
        ls = slice(c * LANES, (c + 1) * LANES)
        n_blk = [nsel_ref[h, pl.ds(irow0, n_i), ls] for h in range(PEER_HEADS)]
        ap_blk = [ap_ref[h, pl.ds(irow0, n_i), ls] for h in range(PEER_HEADS)]
        for ii in range(n_i):
            accs = [None] * (nk // pk)
            for h in range(PEER_HEADS):
                n16 = jnp.broadcast_to(n_blk[h][ii:ii + 1], (pk, LANES)).astype(BF16)
                a16 = jnp.broadcast_to(ap_blk[h][ii:ii + 1], (pk, LANES)).astype(BF16)
                for g in range(nk // pk):
                    rows = slice(g * pk, (g + 1) * pk)
                    hit = rk_ref[h, rows, ls] <= n16
                    term = jnp.where(hit, bp_ref[h, rows, ls], jnp.zeros((), BF16)) * a16
                    accs[g] = term if h == 0 else accs[g] + term
            for g in range(nk // pk):
                w_ref[ii * nk + g * pk:ii * nk + (g + 1) * pk, ls] = accs[g]

    act = jnp.dot(u_ref[...], h2t_ref[...], preferred_element_type=F32)
    gelu = 0.5 * act * (1.0 + lax.erf(act * (2.0 ** -0.5)))
    wa = (w_ref[...].astype(F32) * gelu).astype(BF16)
    acc_ref[...] += jnp.dot(vt_ref[0], wa, preferred_element_type=F32)

    @pl.when(e == n_e - 1)
    def _finish():
        g2 = mod_ref[0, 5:6, :]
        o_ref[...] = x_ref[...] + g2 * acc_ref[...].T


def _peer_call(x1, mod3, n2, wqt, sk, u, vt, seq):
    t, d = x1.shape
    tt = TT_PEER
    ec = EC_PEER
    n_chunks = u.shape[0] // ec
    assert ec % (8 * PEER_N_KEYS) == 0
    tiles_per_seq = seq // tt
    tile = lambda i, e: (i, 0)
    const2 = lambda i, e: (0, 0)
    return pl.pallas_call(
        _peer_kernel,
        out_shape=jax.ShapeDtypeStruct((t, d), F32),
        grid=(t // tt, n_chunks),
        in_specs=[
            pl.BlockSpec((tt, d), tile),
            pl.BlockSpec((1, N_MOD, d), lambda i, e: (i // tiles_per_seq, 0, 0)),
            pl.BlockSpec((1, d), const2),
            pl.BlockSpec(wqt.shape, const2),
            pl.BlockSpec(sk.shape, lambda i, e: (0, 0, 0, 0)),
            pl.BlockSpec((ec, d), lambda i, e: (e, 0)),
            pl.BlockSpec((1, d, ec), lambda i, e: (e, 0, 0)),
        ],
        out_specs=pl.BlockSpec((tt, d), tile),
        scratch_shapes=[
            pltpu.VMEM((d, tt), BF16),
            pltpu.VMEM((PEER_HEADS, PEER_N_KEYS, tt), F32),
            pltpu.VMEM((PEER_HEADS, PEER_N_KEYS, tt), F32),
            pltpu.VMEM((PEER_HEADS, PEER_N_KEYS, tt), F32),
            pltpu.VMEM((PEER_HEADS, PEER_N_KEYS, tt), BF16),
            pltpu.VMEM((PEER_HEADS, PEER_N_KEYS, tt), F32),
            pltpu.VMEM((PEER_HEADS, PEER_N_KEYS, tt), BF16),
            pltpu.VMEM((ec, tt), BF16),
            pltpu.VMEM((d, tt), F32),
        ],
        compiler_params=pltpu.CompilerParams(
            dimension_semantics=("arbitrary", "arbitrary"), vmem_limit_bytes=VMEM_LIMIT),
        name="peer",
    )(x1, mod3, n2, wqt, sk, u, vt)


def kernel(x, c, w_ada, b_ada, norm1_w, w_in, conv_w, w_conv_out, q_norm_w, k_norm_w,
           w_attn_out, w_o, norm2_w, w_peer_q, peer_sub_keys, peer_u, peer_v):
    b, s, d = x.shape
    depth = w_ada.shape[0]
    cw = conv_w.shape[2]
    aw = N_HEADS * HEAD_DIM
    iw = IDX_HEADS * IDX_DIM
    assert s % BQ == 0 and s % TT_PEER == 0 and s % BT_PROJ == 0
    assert peer_u.shape[1] == PEER_N_KEYS * PEER_N_KEYS

    gidx = jnp.arange(aw) // HEAD_DIM
    gmat = (gidx[:, None] == gidx[None, :]).astype(BF16)
    tri = (jnp.arange(BK)[:, None] >= jnp.arange(BK)[None, :]).astype(BF16)

    xf = x.reshape(b * s, d)
    for layer in range(depth):
        w = w_in[layer]
        o = 0
        wconv = w[:, o:o + 3 * cw].astype(BF16); o += 3 * cw
        wqkv = w[:, o:o + 3 * aw].astype(BF16); o += 3 * aw
        wqi = w[:, o:o + iw].astype(BF16); o += iw
        wkw = jnp.pad(w[:, o:o + IDX_DIM + IDX_HEADS], ((0, 0), (0, LANES - IDX_DIM - IDX_HEADS))).astype(BF16)
        o += IDX_DIM + IDX_HEADS
        wg = w[:, o:o + 2 * d].astype(BF16)
        qn_t = jnp.tile(q_norm_w[layer], N_HEADS).reshape(1, aw)
        kn_t = jnp.tile(k_norm_w[layer], N_HEADS).reshape(1, aw)

        mod3 = _mod_call(c, w_ada[layer], b_ada[layer]).reshape(b, N_MOD, d)
        gyc, sga, qt, k, vt, qit, kw = _inproj_call(
            xf, mod3, norm1_w[layer].reshape(1, d), wconv, wqkv, wqi, wkw, wg, conv_w[layer],
            w_conv_out[layer].astype(BF16), qn_t, kn_t, gmat, s)
        attn = _dsa_call(qt, qit, kw, k, vt, tri, b, s)
        x1 = _post_call(xf, attn, gyc, sga, mod3, w_attn_out[layer].astype(BF16),
                        w_o[layer].astype(BF16), s)
        xf = _peer_call(
            x1, mod3, norm2_w[layer].reshape(1, d), w_peer_q[layer].T.astype(BF16),
            peer_sub_keys[layer].astype(BF16), peer_u[layer].astype(BF16),
            peer_v[layer].astype(BF16).reshape(-1, EC_PEER, d).transpose(0, 2, 1), s)
    return xf.reshape(b, s, d)
```

```python
import functools

import jax
import jax.numpy as jnp
from jax import lax
from jax.experimental import pallas as pl
from jax.experimental.pallas import tpu as pltpu

F32 = jnp.float32
BF16 = jnp.bfloat16
I32 = jnp.int32

N_HEADS = 8
HEAD_DIM = 64
IDX_HEADS = 8
IDX_DIM = 64
TOPK_ATTN = 256
PEER_HEADS = 8
PEER_N_KEYS = 128
PEER_HALF = 64
PEER_TOPK = 16
CONV_K = 3
N_MOD = 6
EPS = 1e-6

LANES = 128
INT_MIN = -(2 ** 31)
NEG_BIG = -1e30
LOG2E = 1.4426950408889634
VMEM_LIMIT = 56 * 1024 * 1024

BT_PROJ = 512
BQ = 512
BK = 256
TT_PEER = 512
EC_PEER = 2048


def _split_bf16(a):
    hi = a.astype(BF16)
    lo = (a - hi.astype(F32)).astype(BF16)
    return hi, lo


def _nt_dot(a, b):
    return lax.dot_general(a, b, (((1,), (1,)), ((), ())), preferred_element_type=F32)


def _mod_kernel(c_ref, w_ref, b_ref, o_ref):
    c = c_ref[...]
    s = c * jax.nn.sigmoid(c)
    s_hi, s_lo = _split_bf16(s)
    w_hi, w_lo = _split_bf16(w_ref[...])
    acc = jnp.dot(s_hi, w_hi, preferred_element_type=F32)
    acc += jnp.dot(s_hi, w_lo, preferred_element_type=F32)
    acc += jnp.dot(s_lo, w_hi, preferred_element_type=F32)
    o_ref[...] = acc + b_ref[...]


def _mod_call(c, w_ada, b_ada):
    b, d = c.shape
    n = w_ada.shape[1]
    bn = d
    return pl.pallas_call(
        _mod_kernel,
        out_shape=jax.ShapeDtypeStruct((b, n), F32),
        grid=(n // bn,),
        in_specs=[
            pl.BlockSpec((b, d), lambda j: (0, 0)),
            pl.BlockSpec((d, bn), lambda j: (0, j)),
            pl.BlockSpec((1, bn), lambda j: (0, j)),
        ],
        out_specs=pl.BlockSpec((b, bn), lambda j: (0, j)),
        compiler_params=pltpu.CompilerParams(
            dimension_semantics=("arbitrary",), vmem_limit_bytes=VMEM_LIMIT),
        name="adaln_mod",
    )(c, w_ada, b_ada.reshape(1, n))


def _inproj_kernel(x_ref, xh_ref, mod_ref, n1_ref, wconv_ref, wqkv_ref, wqi_ref, wkw_ref,
                   wg_ref, convw_ref, wco_ref, qn_ref, kn_ref, gmat_ref,
                   gyc_ref, sga_ref, q_ref, k_ref, v_ref, qi_ref, kw_ref, *, tiles_per_seq):
    i = pl.program_id(0)
    cw = convw_ref.shape[1]
    aw = q_ref.shape[1]
    d = x_ref.shape[1]
    bt = x_ref.shape[0]
    sh1 = mod_ref[0, 0:1, :]
    sc1 = mod_ref[0, 1:2, :]

    def norm_mod(xv):
        ms = jnp.mean(xv * xv, axis=-1, keepdims=True)
        y = xv * lax.rsqrt(ms + EPS) * n1_ref[...]
        return (y * (1.0 + sc1) + sh1).astype(BF16)

    h = norm_mod(x_ref[...])
    hh = norm_mod(xh_ref[...])

    pc = jnp.dot(h, wconv_ref[...], preferred_element_type=F32)
    cb, cc, cx = pc[:, :cw], pc[:, cw:2 * cw], pc[:, 2 * cw:]
    u = cc * cx
    ph = jnp.dot(hh, wconv_ref[:, cw:], preferred_element_type=F32)
    uh = ph[:, :cw] * ph[:, cw:]
    uh = jnp.where(i % tiles_per_seq == 0, 0.0, uh)
    row = lax.broadcasted_iota(I32, (bt, cw), 0)
    u1 = jnp.where(row == 0, uh[7:8, :], pltpu.roll(u, 1, 0))
    u2 = pltpu.roll(u, 2, 0)
    u2 = jnp.where(row == 0, uh[6:7, :], jnp.where(row == 1, uh[7:8, :], u2))
    conv = convw_ref[0:1, :] * u2 + convw_ref[1:2, :] * u1 + convw_ref[2:3, :] * u
    yc = jnp.dot((cb * conv).astype(BF16), wco_ref[...], preferred_element_type=F32)

    pg = jnp.dot(h, wg_ref[...], preferred_element_type=F32)
    gyc_ref[...] = (jax.nn.sigmoid(pg[:, :d]) * yc).astype(gyc_ref.dtype)
    sga_ref[...] = jax.nn.sigmoid(pg[:, d:]).astype(sga_ref.dtype)

    pq = jnp.dot(h, wqkv_ref[...], preferred_element_type=F32)
    gmat = gmat_ref[...]

    def head_norm(t, w_row):
        hi, lo = _split_bf16(t * t)
        ss = (jnp.dot(hi, gmat, preferred_element_type=F32)
              + jnp.dot(lo, gmat, preferred_element_type=F32))
        return t * lax.rsqrt(ss * (1.0 / HEAD_DIM) + EPS) * w_row

    qn = head_norm(pq[:, :aw], qn_ref[...])
    qs = qn * (HEAD_DIM ** -0.5 * LOG2E)
    k_ref[...] = head_norm(pq[:, aw:2 * aw], kn_ref[...]).astype(BF16)
    vv = pq[:, 2 * aw:]
    qi = jnp.dot(h, wqi_ref[...], preferred_element_type=F32)
    tw = q_ref.shape[2]
    for s in range(bt // tw):
        rows = slice(s * tw, (s + 1) * tw)
        q_ref[s] = qs[rows].T.astype(BF16)
        v_ref[s] = vv[rows].T.astype(BF16)
        qi_ref[s] = qi[rows].T.astype(BF16)
    kw_ref[...] = jnp.dot(h, wkw_ref[...], preferred_element_type=F32)


def _inproj_call(xf, mod3, n1, wconv, wqkv, wqi, wkw, wg, convw, wco, qn_t, kn_t, gmat, seq):
    t, d = xf.shape
    bt = BT_PROJ
    tw = BK
    tiles_per_seq = seq // bt
    cw = convw.shape[1]
    aw = qn_t.shape[1]
    const = lambda i: (0, 0)
    tile = lambda i: (i, 0)
    tile_t = lambda i: (i, 0, 0)
    resident = lambda a: pl.BlockSpec(a.shape, const, pipeline_mode=pl.Buffered(1))
    halo = lambda i: (jnp.maximum(i * (bt // 8) - 1, 0), 0)
    kern = functools.partial(_inproj_kernel, tiles_per_seq=tiles_per_seq)
    return pl.pallas_call(
        kern,
        out_shape=[
            jax.ShapeDtypeStruct((t, d), BF16),
            jax.ShapeDtypeStruct((t, d), BF16),
            jax.ShapeDtypeStruct((t // tw, aw, tw), BF16),
            jax.ShapeDtypeStruct((t, aw), BF16),
            jax.ShapeDtypeStruct((t // tw, aw, tw), BF16),
            jax.ShapeDtypeStruct((t // tw, aw, tw), BF16),
            jax.ShapeDtypeStruct((t, LANES), F32),
        ],
        grid=(t // bt,),
        in_specs=[
            pl.BlockSpec((bt, d), tile),
            pl.BlockSpec((8, d), halo),
            pl.BlockSpec((1, N_MOD, d), lambda i: (i // tiles_per_seq, 0, 0)),
            pl.BlockSpec((1, d), const),
            resident(wconv),
            resident(wqkv),
            resident(wqi),
            resident(wkw),
            resident(wg),
            pl.BlockSpec(convw.shape, const),
            resident(wco),
            pl.BlockSpec((1, aw), const),
            pl.BlockSpec((1, aw), const),
            resident(gmat),
        ],
        out_specs=[
            pl.BlockSpec((bt, d), tile),
            pl.BlockSpec((bt, d), tile),
            pl.BlockSpec((bt // tw, aw, tw), tile_t),
            pl.BlockSpec((bt, aw), tile),
            pl.BlockSpec((bt // tw, aw, tw), tile_t),
            pl.BlockSpec((bt // tw, aw, tw), tile_t),
            pl.BlockSpec((bt, LANES), tile),
        ],
        compiler_params=pltpu.CompilerParams(
            dimension_semantics=("arbitrary",), vmem_limit_bytes=VMEM_LIMIT),
        name="inproj",
    )(xf, xf, mod3, n1, wconv, wqkv, wqi, wkw, wg, convw, wco, qn_t, kn_t, gmat)


def _tree_sum(parts):
    parts = list(parts)
    while len(parts) > 1:
        nxt = [parts[a] + parts[a + 1] for a in range(0, len(parts) - 1, 2)]
        if len(parts) % 2:
            nxt.append(parts[-1])
        parts = nxt
    return parts[0]


def _dsa_kernel(qt_ref, qit_ref, kwq_ref, k_ref, vt_ref, kwk_ref, tri_ref, o_ref,
                keys_ref, bias_ref, qz_ref, qiz_ref, s_ref, m_ref, l_ref, acc_ref, *, n_sel):
    i = pl.program_id(1)
    n_qt, _, tw = qt_ref.shape
    bq = n_qt * tw
    bk = BK
    n_tiles = (i * bq) // bk + bq // bk
    ngrp = bk // 8

    zpad = jnp.zeros((LANES - HEAD_DIM, bq), BF16)
    for h in range(N_HEADS):
        qh = jnp.concatenate([qt_ref[s, h * HEAD_DIM:(h + 1) * HEAD_DIM, :] for s in range(n_qt)], axis=1)
        qih = jnp.concatenate([qit_ref[s, h * IDX_DIM:(h + 1) * IDX_DIM, :] for s in range(n_qt)], axis=1)
        qz_ref[h] = jnp.concatenate([qh, zpad] if h % 2 == 0 else [zpad, qh], axis=0)
        qiz_ref[h] = jnp.concatenate([qih, zpad], axis=0)
    wt = kwq_ref[...].T

    q_pos = i * bq + lax.broadcasted_iota(I32, (bk, bq), 1)

    def score_tile(j, carry):
        kw = kwk_ref[pl.ds(pl.multiple_of(j * bk, bk), bk), :].astype(BF16)
        score = jnp.zeros((bk, bq), F32)
        for h in range(IDX_HEADS):
            logits = jnp.dot(kw, qiz_ref[h], preferred_element_type=F32)
            score = score + wt[IDX_DIM + h:IDX_DIM + h + 1, :] * jnp.maximum(logits, 0.0)
        bits = pltpu.bitcast(score, I32)
        key = jnp.where(bits < 0, bits ^ jnp.int32(0x7FFFFFFF), bits)
        key = jnp.where(bits == jnp.int32(INT_MIN), 0, key)
        key_pos = j * bk + lax.broadcasted_iota(I32, (bk, bq), 0)
        keys_ref[j] = jnp.where(key_pos <= q_pos, key, jnp.int32(INT_MIN))
        return carry

    lax.fori_loop(0, n_tiles, score_tile, 0)

    def count_ge(cand, strict):
        cand_b = jnp.broadcast_to(cand, (8, bq))

        def body(j, cnt):
            hits = []
            for g in range(ngrp):
                ks = keys_ref[j, g * 8:(g + 1) * 8, :]
                hit = (ks > cand_b) if strict else (ks >= cand_b)
                hits.append(jnp.where(hit, 1, 0))
            return cnt + _tree_sum(hits)

        cnt = lax.fori_loop(0, n_tiles, body, jnp.zeros((8, bq), I32))
        return jnp.sum(cnt, axis=0, keepdims=True)

    def bit_step(b, t_u):
        cand_u = t_u | lax.shift_left(jnp.int32(1), 31 - b)
        total = count_ge(cand_u ^ jnp.int32(INT_MIN), False)
        return jnp.where(total >= n_sel, cand_u, t_u)

    t_u = lax.fori_loop(0, 32, bit_step, jnp.zeros((1, bq), I32))
    thr = t_u ^ jnp.int32(INT_MIN)
    n_gt = count_ge(thr, True)
    need = jnp.where(thr == jnp.int32(INT_MIN), 0, n_sel - n_gt).astype(F32)

    tri = tri_ref[...]

    def bias_tile(j, carry):
        kt = keys_ref[j]
        eq = kt == thr
        eq_f = jnp.where(eq, 1.0, 0.0)
        prefix = jnp.dot(tri, eq_f.astype(BF16), preferred_element_type=F32) + carry
        sel = (kt > thr) | (eq & (prefix <= need))
        bias_ref[j] = jnp.where(sel, 0.0, NEG_BIG)
        return prefix[bk - 1:bk, :]

    lax.fori_loop(0, n_tiles, bias_tile, jnp.zeros((1, bq), F32))

    m_ref[...] = jnp.full(m_ref.shape, NEG_BIG, F32)
    l_ref[...] = jnp.zeros(l_ref.shape, F32)
    acc_ref[...] = jnp.zeros(acc_ref.shape, F32)

    def attn_tile(j, carry):
        kt = k_ref[pl.ds(pl.multiple_of(j * bk, bk), bk), :]
        bias = bias_ref[j]
        alphas = []
        for h in range(N_HEADS):
            pair = slice((h // 2) * LANES, (h // 2 + 1) * LANES)
            s = jnp.dot(kt[:, pair], qz_ref[h], preferred_element_type=F32) + bias
            s_ref[h] = s
            m_prev = m_ref[h]
            m_new = jnp.maximum(m_prev, jnp.broadcast_to(jnp.max(s, axis=0, keepdims=True), (8, bq)))
            m_ref[h] = m_new
            alphas.append(jnp.exp2(m_prev - m_new))
        for h in range(N_HEADS):
            p = jnp.exp2(s_ref[h] - m_ref[h, 0:1, :])
            l_ref[h] = alphas[h] * l_ref[h] + _tree_sum([p[g * 8:(g + 1) * 8] for g in range(ngrp)])
            pv = jnp.dot(vt_ref[j, h * HEAD_DIM:(h + 1) * HEAD_DIM, :], p.astype(BF16),
                         preferred_element_type=F32)
            acc_ref[h] = alphas[h][0:1] * acc_ref[h] + pv
        return carry

    lax.fori_loop(0, n_tiles, attn_tile, 0)

    outs = [acc_ref[h] / jnp.sum(l_ref[h], axis=0, keepdims=True) for h in range(N_HEADS)]
    o_ref[...] = jnp.concatenate(outs, axis=0).T.astype(o_ref.dtype)


def _dsa_call(qt, qit, kw, k, vt, tri, batch, seq):
    nt, aw, tw = qt.shape
    bq = BQ
    assert tw == BK and bq % tw == 0 and seq % bq == 0
    nq = seq // bq
    n_sel = min(TOPK_ATTN, seq // 4)
    qtile = lambda b, i: (b * nq + i, 0)
    qtile_t = lambda b, i: (b * nq + i, 0, 0)
    whole = lambda b, i: (b, 0)
    kern = functools.partial(_dsa_kernel, n_sel=n_sel)
    return pl.pallas_call(
        kern,
        out_shape=jax.ShapeDtypeStruct((batch * seq, aw), BF16),
        grid=(batch, nq),
        in_specs=[
            pl.BlockSpec((bq // tw, aw, tw), qtile_t),
            pl.BlockSpec((bq // tw, aw, tw), qtile_t),
            pl.BlockSpec((bq, LANES), qtile),
            pl.BlockSpec((seq, aw), whole),
            pl.BlockSpec((seq // BK, aw, BK), lambda b, i: (b, 0, 0)),
            pl.BlockSpec((seq, LANES), whole),
            pl.BlockSpec(tri.shape, lambda b, i: (0, 0)),
        ],
        out_specs=pl.BlockSpec((bq, aw), qtile),
        scratch_shapes=[
            pltpu.VMEM((seq // BK, BK, bq), I32),
            pltpu.VMEM((seq // BK, BK, bq), F32),
            pltpu.VMEM((N_HEADS, LANES, bq), BF16),
            pltpu.VMEM((IDX_HEADS, LANES, bq), BF16),
            pltpu.VMEM((N_HEADS, BK, bq), F32),
            pltpu.VMEM((N_HEADS, 8, bq), F32),
            pltpu.VMEM((N_HEADS, 8, bq), F32),
            pltpu.VMEM((N_HEADS, HEAD_DIM, bq), F32),
        ],
        compiler_params=pltpu.CompilerParams(
            dimension_semantics=("arbitrary", "arbitrary"), vmem_limit_bytes=VMEM_LIMIT),
        name="dsa_attention",
    )(qt, qit, kw, k, vt, kw, tri)


def _post_kernel(x_ref, attn_ref, gyc_ref, sga_ref, mod_ref, wao_ref, wo_ref, o_ref):
    g1 = mod_ref[0, 2:3, :]
    y_attn = jnp.dot(attn_ref[...], wao_ref[...], preferred_element_type=F32)
    mix = gyc_ref[...] + sga_ref[...] * y_attn
    delta = jnp.dot(mix.astype(BF16), wo_ref[...], preferred_element_type=F32)
    o_ref[...] = x_ref[...] + g1 * delta


def _post_call(xf, attn, gyc, sga, mod3, wao, wo, seq):
    t, d = xf.shape
    bt = BT_PROJ
    tiles_per_seq = seq // bt
    aw = attn.shape[1]
    tile = lambda i: (i, 0)
    const = lambda i: (0, 0)
    return pl.pallas_call(
        _post_kernel,
        out_shape=jax.ShapeDtypeStruct((t, d), F32),
        grid=(t // bt,),
        in_specs=[
            pl.BlockSpec((bt, d), tile),
            pl.BlockSpec((bt, aw), tile),
            pl.BlockSpec((bt, d), tile),
            pl.BlockSpec((bt, d), tile),
            pl.BlockSpec((1, N_MOD, d), lambda i: (i // tiles_per_seq, 0, 0)),
            pl.BlockSpec(wao.shape, const),
            pl.BlockSpec(wo.shape, const),
        ],
        out_specs=pl.BlockSpec((bt, d), tile),
        compiler_params=pltpu.CompilerParams(
            dimension_semantics=("arbitrary",), vmem_limit_bytes=VMEM_LIMIT),
        name="post_mix",
    )(xf, attn, gyc, sga, mod3, wao, wo)


def _col_max(a):
    return jnp.max(a, axis=0, keepdims=True)


def _batcher_pairs(n):
    pairs = []
    p = 1
    while p < n:
        k = p
        while k >= 1:
            for j in range(k % p, n - k, 2 * k):
                for i in range(min(k, n - j - k)):
                    if (i + j) // (2 * p) == (i + j + k) // (2 * p):
                        pairs.append((i + j, i + j + k))
            k //= 2
        p *= 2
    return pairs


_SORT16 = _batcher_pairs(PEER_TOPK)
_BITONIC16 = [(k, k + d) for d in (8, 4, 2, 1) for k in range(PEER_TOPK) if not k & d]


def _sort_desc(v, pairs):
    v = list(v)
    for i, j in pairs:
        v[i], v[j] = jnp.maximum(v[i], v[j]), jnp.minimum(v[i], v[j])
    return v


def _peer_kernel(x_ref, mod_ref, n2_ref, wqt_ref, sk_ref, u_ref, vt_ref, o_ref,
                 h2t_ref, s1_ref, s2_ref, ap_ref, bp_ref, nsel_ref, rk_ref, w_ref, acc_ref):
    e = pl.program_id(1)
    n_e = pl.num_programs(1)
    tt = x_ref.shape[0]
    nk = PEER_N_KEYS
    n_i = u_ref.shape[0] // nk
    nslab = tt // LANES

    @pl.when(e == 0)
    def _route():
        xv = x_ref[...]
        sh2 = mod_ref[0, 3:4, :]
        sc2 = mod_ref[0, 4:5, :]
        ms = jnp.mean(xv * xv, axis=-1, keepdims=True)
        h2t = ((xv * lax.rsqrt(ms + EPS) * n2_ref[...]) * (1.0 + sc2) + sh2).T.astype(BF16)
        h2t_ref[...] = h2t
        qt = jnp.dot(wqt_ref[...], h2t, preferred_element_type=F32).astype(BF16)
        for h in range(PEER_HEADS):
            for p, dst in ((0, s1_ref), (1, s2_ref)):
                r0 = (h * 2 + p) * PEER_HALF
                dst[h] = jnp.dot(sk_ref[h, p], qt[r0:r0 + PEER_HALF, :],
                                 preferred_element_type=F32)

        neg_inf = jnp.float32(-jnp.inf)
        sub_id = lax.broadcasted_iota(I32, (8, LANES), 0)

        def top16(s):
            v = _sort_desc([s[g * 8:(g + 1) * 8] for g in range(nk // 8)], _SORT16)
            for shift in (4, 2, 1):
                v = [jnp.maximum(v[r], pltpu.roll(v[PEER_TOPK - 1 - r], shift, 0)) for r in range(PEER_TOPK)]
                v = _sort_desc(v, _BITONIC16)
            return v

        def stack8(vals):
            out = vals[0]
            for r in range(1, 8):
                out = jnp.where(sub_id == r, vals[r], out)
            return out

        def route_head(h, carry):
            for c in range(nslab):
                ls = slice(c * LANES, (c + 1) * LANES)
                s1 = s1_ref[h, :, ls]
                s2 = s2_ref[h, :, ls]
                a = top16(s1)
                b = top16(s2)
                b_lo, b_hi = stack8(b[:8]), stack8(b[8:])
                cands = [a[0] + b_lo, a[0] + b_hi] + [a[r] + b_lo for r in range(1, PEER_TOPK)]
                work, t = cands, []
                for _ in range(PEER_TOPK):
                    m = _col_max(functools.reduce(jnp.maximum, work))
                    t.append(m)
                    work = [jnp.where(w == m, neg_inf, w) for w in work]
                thr = t[PEER_TOPK - 1]
                z = functools.reduce(jnp.add, [jnp.exp(tk - t[0]) for tk in t])
                hits = [jnp.where(cd >= thr, 1.0, 0.0) for cd in cands]
                n_sel = [jnp.sum(hits[0] + hits[1], axis=0, keepdims=True)]
                n_sel += [jnp.sum(hits[r + 1], axis=0, keepdims=True) for r in range(1, PEER_TOPK)]
                n_b = [jnp.broadcast_to(n, (8, LANES)) for n in n_sel]
                cnts, ranks = [], []
                for g in range(nk // 8):
                    rows = slice(g * 8, (g + 1) * 8)
                    s1g, s2g = s1[rows], s2[rows]
                    cnt = jnp.zeros((8, LANES), F32)
                    rank = jnp.ones((8, LANES), F32)
                    for r in range(PEER_TOPK):
                        cnt = jnp.where(s1g == a[r], n_b[r], cnt)
                        rank = rank + jnp.where(b[r] > s2g, 1.0, 0.0)
                    cnts.append(cnt)
                    ranks.append(rank)
                nsel_ref[h, :, ls] = jnp.concatenate(cnts, axis=0)
                rk_ref[h, :, ls] = jnp.concatenate(ranks, axis=0).astype(BF16)
                ap_ref[h, :, ls] = jnp.exp(s1 - a[0][0:1])
                bp_ref[h, :, ls] = (jnp.exp(s2 - b[0][0:1]) * (0.5 / z)).astype(BF16)
            return carry

        lax.fori_loop(0, PEER_HEADS, route_head, 0)
        acc_ref[...] = jnp.zeros(acc_ref.shape, F32)

    irow0 = pl.multiple_of(e * n_i, 8)
    pk = 16
    for c in range(nslab):
        ls = slice(c * LANES, (c + 1) * LANES)
        n_blk = [nsel_ref[h, pl.ds(irow0, n_i), ls] for h in range(PEER_HEADS)]
        ap_blk = [ap_ref[h, pl.ds(irow0, n_i), ls] for h in range(PEER_HEADS)]
        for ii in range(n_i):
            accs = [None] * (nk // pk)
            for h in range(PEER_HEADS):
                n16 = jnp.broadcast_to(n_blk[h][ii:ii + 1], (pk, LANES)).astype(BF16)
                a16 = jnp.broadcast_to(ap_blk[h][ii:ii + 1], (pk, LANES)).astype(BF16)
                for g in range(nk // pk):
                    rows = slice(g * pk, (g + 1) * pk)
                    hit = rk_ref[h, rows, ls] <= n16
                    term = jnp.where(hit, bp_ref[h, rows, ls], jnp.zeros((), BF16)) * a16
                    accs[g] = term if h == 0 else accs[g] + term
            for g in range(nk // pk):
                w_ref[ii * nk + g * pk:ii * nk + (g + 1) * pk, ls] = accs[g]

    act = jnp.dot(u_ref[...], h2t_ref[...], preferred_element_type=F32)
    gelu2 = act * (1.0 + lax.erf(act * (2.0 ** -0.5)))
    wa = w_ref[...] * gelu2.astype(BF16)
    acc_ref[...] += jnp.dot(vt_ref[0], wa, preferred_element_type=F32)

    @pl.when(e == n_e - 1)
    def _finish():
        g2 = mod_ref[0, 5:6, :]
        o_ref[...] = x_ref[...] + g2 * acc_ref[...].T


def _peer_call(x1, mod3, n2, wqt, sk, u, vt, seq):
    t, d = x1.shape
    tt = TT_PEER
    ec = EC_PEER
    n_chunks = u.shape[0] // ec
    assert ec % (8 * PEER_N_KEYS) == 0
    tiles_per_seq = seq // tt
    tile = lambda i, e: (i, 0)
    const2 = lambda i, e: (0, 0)
    return pl.pallas_call(
        _peer_kernel,
        out_shape=jax.ShapeDtypeStruct((t, d), F32),
        grid=(t // tt, n_chunks),
        in_specs=[
            pl.BlockSpec((tt, d), tile),
            pl.BlockSpec((1, N_MOD, d), lambda i, e: (i // tiles_per_seq, 0, 0)),
            pl.BlockSpec((1, d), const2),
            pl.BlockSpec(wqt.shape, const2),
            pl.BlockSpec(sk.shape, lambda i, e: (0, 0, 0, 0)),
            pl.BlockSpec((ec, d), lambda i, e: (e, 0)),
            pl.BlockSpec((1, d, ec), lambda i, e: (e, 0, 0)),
        ],
        out_specs=pl.BlockSpec((tt, d), tile),
        scratch_shapes=[
            pltpu.VMEM((d, tt), BF16),
            pltpu.VMEM((PEER_HEADS, PEER_N_KEYS, tt), F32),
            pltpu.VMEM((PEER_HEADS, PEER_N_KEYS, tt), F32),
            pltpu.VMEM((PEER_HEADS, PEER_N_KEYS, tt), F32),
            pltpu.VMEM((PEER_HEADS, PEER_N_KEYS, tt), BF16),
            pltpu.VMEM((PEER_HEADS, PEER_N_KEYS, tt), F32),
            pltpu.VMEM((PEER_HEADS, PEER_N_KEYS, tt), BF16),
            pltpu.VMEM((ec, tt), BF16),
            pltpu.VMEM((d, tt), F32),
        ],
        compiler_params=pltpu.CompilerParams(
            dimension_semantics=("arbitrary", "arbitrary"), vmem_limit_bytes=VMEM_LIMIT),
        name="peer",
    )(x1, mod3, n2, wqt, sk, u, vt)


def kernel(x, c, w_ada, b_ada, norm1_w, w_in, conv_w, w_conv_out, q_norm_w, k_norm_w,
           w_attn_out, w_o, norm2_w, w_peer_q, peer_sub_keys, peer_u, peer_v):
    b, s, d = x.shape
    depth = w_ada.shape[0]
    cw = conv_w.shape[2]
    aw = N_HEADS * HEAD_DIM
    iw = IDX_HEADS * IDX_DIM
    assert s % BQ == 0 and s % TT_PEER == 0 and s % BT_PROJ == 0
    assert peer_u.shape[1] == PEER_N_KEYS * PEER_N_KEYS

    gidx = jnp.arange(aw) // HEAD_DIM
    gmat = (gidx[:, None] == gidx[None, :]).astype(BF16)
    tri = (jnp.arange(BK)[:, None] >= jnp.arange(BK)[None, :]).astype(BF16)

    xf = x.reshape(b * s, d)
    for layer in range(depth):
        w = w_in[layer]
        o = 0
        wconv = w[:, o:o + 3 * cw].astype(BF16); o += 3 * cw
        wqkv = w[:, o:o + 3 * aw].astype(BF16); o += 3 * aw
        wqi = w[:, o:o + iw].astype(BF16); o += iw
        wkw = jnp.pad(w[:, o:o + IDX_DIM + IDX_HEADS], ((0, 0), (0, LANES - IDX_DIM - IDX_HEADS))).astype(BF16)
        o += IDX_DIM + IDX_HEADS
        wg = w[:, o:o + 2 * d].astype(BF16)
        qn_t = jnp.tile(q_norm_w[layer], N_HEADS).reshape(1, aw)
        kn_t = jnp.tile(k_norm_w[layer], N_HEADS).reshape(1, aw)

        mod3 = _mod_call(c, w_ada[layer], b_ada[layer]).reshape(b, N_MOD, d)
        gyc, sga, qt, k, vt, qit, kw = _inproj_call(
            xf, mod3, norm1_w[layer].reshape(1, d), wconv, wqkv, wqi, wkw, wg, conv_w[layer],
            w_conv_out[layer].astype(BF16), qn_t, kn_t, gmat, s)
        attn = _dsa_call(qt, qit, kw, k, vt, tri, b, s)
        x1 = _post_call(xf, attn, gyc, sga, mod3, w_attn_out[layer].astype(BF16),
                        w_o[layer].astype(BF16), s)
        xf = _peer_call(
            x1, mod3, norm2_w[layer].reshape(1, d), w_peer_q[layer].T.astype(BF16),
            peer_sub_keys[layer].astype(BF16), peer_u[layer].astype(BF16),
            peer_v[layer].astype(BF16).reshape(-1, EC_PEER, d).transpose(0, 2, 1), s)
    return xf.reshape(b, s, d)
```

```python
import functools

import jax
import jax.numpy as jnp
from jax import lax
from jax.experimental import pallas as pl
from jax.experimental.pallas import tpu as pltpu

F32 = jnp.float32
BF16 = jnp.bfloat16
I32 = jnp.int32

N_HEADS = 8
HEAD_DIM = 64
IDX_HEADS = 8
IDX_DIM = 64
TOPK_ATTN = 256
PEER_HEADS = 8
PEER_N_KEYS = 128
PEER_HALF = 64
PEER_TOPK = 16
N_MOD = 6
EPS = 1e-6

LANES = 128
SUBLANES = 8
BF16_ROWS = 16
INT_MIN = -(2 ** 31)
NEG_BIG = -1e30
LOG2E = 1.4426950408889634
VMEM_LIMIT = 56 * 1024 * 1024

BT_PROJ = 512
BQ = 512
BK = 256
TT_PEER = 512
EC_PEER = 2048


def _split_bf16(a):
    hi = a.astype(BF16)
    lo = (a - hi.astype(F32)).astype(BF16)
    return hi, lo


def _mod_kernel(c_ref, w_ref, b_ref, o_ref):
    c = c_ref[...]
    s = c * jax.nn.sigmoid(c)
    s_hi, s_lo = _split_bf16(s)
    w_hi, w_lo = _split_bf16(w_ref[...])
    acc = jnp.dot(s_hi, w_hi, preferred_element_type=F32)
    acc += jnp.dot(s_hi, w_lo, preferred_element_type=F32)
    acc += jnp.dot(s_lo, w_hi, preferred_element_type=F32)
    o_ref[...] = acc + b_ref[...]


def _mod_call(c, w_ada, b_ada):
    b, d = c.shape
    n = w_ada.shape[1]
    bn = d
    return pl.pallas_call(
        _mod_kernel,
        out_shape=jax.ShapeDtypeStruct((b, n), F32),
        grid=(n // bn,),
        in_specs=[
            pl.BlockSpec((b, d), lambda j: (0, 0)),
            pl.BlockSpec((d, bn), lambda j: (0, j)),
            pl.BlockSpec((1, bn), lambda j: (0, j)),
        ],
        out_specs=pl.BlockSpec((b, bn), lambda j: (0, j)),
        compiler_params=pltpu.CompilerParams(
            dimension_semantics=("arbitrary",), vmem_limit_bytes=VMEM_LIMIT),
        name="adaln_mod",
    )(c, w_ada, b_ada.reshape(1, n))


def _inproj_kernel(x_ref, xh_ref, mod_ref, n1_ref, wconv_ref, wqkv_ref, wqi_ref, wkw_ref,
                   wg_ref, convw_ref, wco_ref, qn_ref, kn_ref, gmat_ref,
                   gyc_ref, sga_ref, q_ref, k_ref, v_ref, qi_ref, kw_ref, *, tiles_per_seq):
    i = pl.program_id(0)
    cw = convw_ref.shape[1]
    aw = q_ref.shape[1]
    d = x_ref.shape[1]
    bt = x_ref.shape[0]
    sh1 = mod_ref[0, 0:1, :]
    sc1 = mod_ref[0, 1:2, :]

    def norm_mod(xv):
        ms = jnp.mean(xv * xv, axis=-1, keepdims=True)
        y = xv * lax.rsqrt(ms + EPS) * n1_ref[...]
        return (y * (1.0 + sc1) + sh1).astype(BF16)

    h = norm_mod(x_ref[...])
    hh = norm_mod(xh_ref[...])

    pc = jnp.dot(h, wconv_ref[...], preferred_element_type=F32)
    cb, cc, cx = pc[:, :cw], pc[:, cw:2 * cw], pc[:, 2 * cw:]
    u = cc * cx
    ph = jnp.dot(hh, wconv_ref[:, cw:], preferred_element_type=F32)
    uh = ph[:, :cw] * ph[:, cw:]
    uh = jnp.where(i % tiles_per_seq == 0, 0.0, uh)
    row = lax.broadcasted_iota(I32, (bt, cw), 0)
    prev1 = uh[SUBLANES - 1:SUBLANES, :]
    prev2 = uh[SUBLANES - 2:SUBLANES - 1, :]
    u1 = jnp.where(row == 0, prev1, pltpu.roll(u, 1, 0))
    u2 = pltpu.roll(u, 2, 0)
    u2 = jnp.where(row == 0, prev2, jnp.where(row == 1, prev1, u2))
    conv = convw_ref[0:1, :] * u2 + convw_ref[1:2, :] * u1 + convw_ref[2:3, :] * u
    yc = jnp.dot((cb * conv).astype(BF16), wco_ref[...], preferred_element_type=F32)

    pg = jnp.dot(h, wg_ref[...], preferred_element_type=F32)
    gyc_ref[...] = (jax.nn.sigmoid(pg[:, :d]) * yc).astype(gyc_ref.dtype)
    sga_ref[...] = jax.nn.sigmoid(pg[:, d:]).astype(sga_ref.dtype)

    pq = jnp.dot(h, wqkv_ref[...], preferred_element_type=F32)
    gmat = gmat_ref[...]

    def head_norm(t, w_row):
        hi, lo = _split_bf16(t * t)
        ss = (jnp.dot(hi, gmat, preferred_element_type=F32)
              + jnp.dot(lo, gmat, preferred_element_type=F32))
        return t * lax.rsqrt(ss * (1.0 / HEAD_DIM) + EPS) * w_row

    qn = head_norm(pq[:, :aw], qn_ref[...])
    qs = qn * (HEAD_DIM ** -0.5 * LOG2E)
    k_ref[...] = head_norm(pq[:, aw:2 * aw], kn_ref[...]).astype(BF16)
    vv = pq[:, 2 * aw:]
    qi = jnp.dot(h, wqi_ref[...], preferred_element_type=F32)
    tw = q_ref.shape[2]
    for s in range(bt // tw):
        rows = slice(s * tw, (s + 1) * tw)
        q_ref[s] = qs[rows].T.astype(BF16)
        v_ref[s] = vv[rows].T.astype(BF16)
        qi_ref[s] = qi[rows].T.astype(BF16)
    kw_ref[...] = jnp.dot(h, wkw_ref[...], preferred_element_type=F32)


def _inproj_call(xf, mod3, n1, wconv, wqkv, wqi, wkw, wg, convw, wco, qn_t, kn_t, gmat, seq):
    t, d = xf.shape
    bt = BT_PROJ
    tw = BK
    tiles_per_seq = seq // bt
    cw = convw.shape[1]
    aw = qn_t.shape[1]
    const = lambda i: (0, 0)
    tile = lambda i: (i, 0)
    tile_t = lambda i: (i, 0, 0)
    resident = lambda a: pl.BlockSpec(a.shape, const, pipeline_mode=pl.Buffered(1))
    halo = lambda i: (jnp.maximum(i * (bt // SUBLANES) - 1, 0), 0)
    kern = functools.partial(_inproj_kernel, tiles_per_seq=tiles_per_seq)
    return pl.pallas_call(
        kern,
        out_shape=[
            jax.ShapeDtypeStruct((t, d), BF16),
            jax.ShapeDtypeStruct((t, d), BF16),
            jax.ShapeDtypeStruct((t // tw, aw, tw), BF16),
            jax.ShapeDtypeStruct((t, aw), BF16),
            jax.ShapeDtypeStruct((t // tw, aw, tw), BF16),
            jax.ShapeDtypeStruct((t // tw, aw, tw), BF16),
            jax.ShapeDtypeStruct((t, LANES), F32),
        ],
        grid=(t // bt,),
        in_specs=[
            pl.BlockSpec((bt, d), tile),
            pl.BlockSpec((SUBLANES, d), halo),
            pl.BlockSpec((1, N_MOD, d), lambda i: (i // tiles_per_seq, 0, 0)),
            pl.BlockSpec((1, d), const),
            resident(wconv),
            resident(wqkv),
            resident(wqi),
            resident(wkw),
            resident(wg),
            pl.BlockSpec(convw.shape, const),
            resident(wco),
            pl.BlockSpec((1, aw), const),
            pl.BlockSpec((1, aw), const),
            resident(gmat),
        ],
        out_specs=[
            pl.BlockSpec((bt, d), tile),
            pl.BlockSpec((bt, d), tile),
            pl.BlockSpec((bt // tw, aw, tw), tile_t),
            pl.BlockSpec((bt, aw), tile),
            pl.BlockSpec((bt // tw, aw, tw), tile_t),
            pl.BlockSpec((bt // tw, aw, tw), tile_t),
            pl.BlockSpec((bt, LANES), tile),
        ],
        compiler_params=pltpu.CompilerParams(
            dimension_semantics=("arbitrary",), vmem_limit_bytes=VMEM_LIMIT),
        name="inproj",
    )(xf, xf, mod3, n1, wconv, wqkv, wqi, wkw, wg, convw, wco, qn_t, kn_t, gmat)


def _tree_sum(parts):
    parts = list(parts)
    while len(parts) > 1:
        nxt = [parts[a] + parts[a + 1] for a in range(0, len(parts) - 1, 2)]
        if len(parts) % 2:
            nxt.append(parts[-1])
        parts = nxt
    return parts[0]


def _dsa_kernel(qt_ref, qit_ref, kwq_ref, k_ref, vt_ref, kwk_ref, tri_ref, o_ref,
                keys_ref, bias_ref, qz_ref, qiz_ref, s_ref, m_ref, l_ref, acc_ref, *, n_sel):
    i = pl.program_id(1)
    n_qt, _, tw = qt_ref.shape
    bq = n_qt * tw
    bk = BK
    n_tiles = (i * bq) // bk + bq // bk
    ngrp = bk // SUBLANES

    zpad = jnp.zeros((LANES - HEAD_DIM, bq), BF16)
    for h in range(N_HEADS):
        qh = jnp.concatenate([qt_ref[s, h * HEAD_DIM:(h + 1) * HEAD_DIM, :] for s in range(n_qt)], axis=1)
        qih = jnp.concatenate([qit_ref[s, h * IDX_DIM:(h + 1) * IDX_DIM, :] for s in range(n_qt)], axis=1)
        qz_ref[h] = jnp.concatenate([qh, zpad] if h % 2 == 0 else [zpad, qh], axis=0)
        qiz_ref[h] = jnp.concatenate([qih, zpad], axis=0)
    wt = kwq_ref[...].T

    q_pos = i * bq + lax.broadcasted_iota(I32, (bk, bq), 1)

    def score_tile(j, carry):
        kw = kwk_ref[pl.ds(pl.multiple_of(j * bk, bk), bk), :].astype(BF16)
        score = jnp.zeros((bk, bq), F32)
        for h in range(IDX_HEADS):
            logits = jnp.dot(kw, qiz_ref[h], preferred_element_type=F32)
            score = score + wt[IDX_DIM + h:IDX_DIM + h + 1, :] * jnp.maximum(logits, 0.0)
        bits = pltpu.bitcast(score, I32)
        key = jnp.where(bits < 0, bits ^ jnp.int32(0x7FFFFFFF), bits)
        key = jnp.where(bits == jnp.int32(INT_MIN), 0, key)
        key_pos = j * bk + lax.broadcasted_iota(I32, (bk, bq), 0)
        keys_ref[j] = jnp.where(key_pos <= q_pos, key, jnp.int32(INT_MIN))
        return carry

    lax.fori_loop(0, n_tiles, score_tile, 0)

    def count_ge(cand, strict):
        cand_b = jnp.broadcast_to(cand, (SUBLANES, bq))

        def body(j, cnt):
            hits = []
            for g in range(ngrp):
                ks = keys_ref[j, g * SUBLANES:(g + 1) * SUBLANES, :]
                hit = (ks > cand_b) if strict else (ks >= cand_b)
                hits.append(jnp.where(hit, 1, 0))
            return cnt + _tree_sum(hits)

        cnt = lax.fori_loop(0, n_tiles, body, jnp.zeros((SUBLANES, bq), I32))
        return jnp.sum(cnt, axis=0, keepdims=True)

    def bit_step(b, t_u):
        cand_u = t_u | lax.shift_left(jnp.int32(1), 31 - b)
        total = count_ge(cand_u ^ jnp.int32(INT_MIN), False)
        return jnp.where(total >= n_sel, cand_u, t_u)

    t_u = lax.fori_loop(0, 32, bit_step, jnp.zeros((1, bq), I32))
    thr = t_u ^ jnp.int32(INT_MIN)
    n_gt = count_ge(thr, True)
    need = jnp.where(thr == jnp.int32(INT_MIN), 0, n_sel - n_gt).astype(F32)

    tri = tri_ref[...]

    def bias_tile(j, carry):
        kt = keys_ref[j]
        eq = kt == thr
        eq_f = jnp.where(eq, 1.0, 0.0)
        prefix = jnp.dot(tri, eq_f.astype(BF16), preferred_element_type=F32) + carry
        sel = (kt > thr) | (eq & (prefix <= need))
        bias_ref[j] = jnp.where(sel, 0.0, NEG_BIG)
        return prefix[bk - 1:bk, :]

    lax.fori_loop(0, n_tiles, bias_tile, jnp.zeros((1, bq), F32))

    m_ref[...] = jnp.full(m_ref.shape, NEG_BIG, F32)
    l_ref[...] = jnp.zeros(l_ref.shape, F32)
    acc_ref[...] = jnp.zeros(acc_ref.shape, F32)

    def attn_tile(j, carry):
        kt = k_ref[pl.ds(pl.multiple_of(j * bk, bk), bk), :]
        bias = bias_ref[j]
        alphas = []
        for h in range(N_HEADS):
            pair = slice((h // 2) * LANES, (h // 2 + 1) * LANES)
            s = jnp.dot(kt[:, pair], qz_ref[h], preferred_element_type=F32) + bias
            s_ref[h] = s
            m_prev = m_ref[h]
            m_new = jnp.maximum(m_prev, jnp.broadcast_to(jnp.max(s, axis=0, keepdims=True), (SUBLANES, bq)))
            m_ref[h] = m_new
            alphas.append(jnp.exp2(m_prev - m_new))
        for h in range(N_HEADS):
            p = jnp.exp2(s_ref[h] - m_ref[h, 0:1, :])
            l_ref[h] = alphas[h] * l_ref[h] + _tree_sum([p[g * SUBLANES:(g + 1) * SUBLANES] for g in range(ngrp)])
            pv = jnp.dot(vt_ref[j, h * HEAD_DIM:(h + 1) * HEAD_DIM, :], p.astype(BF16),
                         preferred_element_type=F32)
            acc_ref[h] = alphas[h][0:1] * acc_ref[h] + pv
        return carry

    lax.fori_loop(0, n_tiles, attn_tile, 0)

    outs = [acc_ref[h] / jnp.sum(l_ref[h], axis=0, keepdims=True) for h in range(N_HEADS)]
    o_ref[...] = jnp.concatenate(outs, axis=0).T.astype(o_ref.dtype)


def _dsa_call(qt, qit, kw, k, vt, tri, batch, seq):
    nt, aw, tw = qt.shape
    bq = BQ
    assert tw == BK and bq % tw == 0 and seq % bq == 0
    nq = seq // bq
    n_sel = min(TOPK_ATTN, seq // 4)
    qtile = lambda b, i: (b * nq + i, 0)
    qtile_t = lambda b, i: (b * nq + i, 0, 0)
    whole = lambda b, i: (b, 0)
    kern = functools.partial(_dsa_kernel, n_sel=n_sel)
    return pl.pallas_call(
        kern,
        out_shape=jax.ShapeDtypeStruct((batch * seq, aw), BF16),
        grid=(batch, nq),
        in_specs=[
            pl.BlockSpec((bq // tw, aw, tw), qtile_t),
            pl.BlockSpec((bq // tw, aw, tw), qtile_t),
            pl.BlockSpec((bq, LANES), qtile),
            pl.BlockSpec((seq, aw), whole),
            pl.BlockSpec((seq // BK, aw, BK), lambda b, i: (b, 0, 0)),
            pl.BlockSpec((seq, LANES), whole),
            pl.BlockSpec(tri.shape, lambda b, i: (0, 0)),
        ],
        out_specs=pl.BlockSpec((bq, aw), qtile),
        scratch_shapes=[
            pltpu.VMEM((seq // BK, BK, bq), I32),
            pltpu.VMEM((seq // BK, BK, bq), F32),
            pltpu.VMEM((N_HEADS, LANES, bq), BF16),
            pltpu.VMEM((IDX_HEADS, LANES, bq), BF16),
            pltpu.VMEM((N_HEADS, BK, bq), F32),
            pltpu.VMEM((N_HEADS, SUBLANES, bq), F32),
            pltpu.VMEM((N_HEADS, SUBLANES, bq), F32),
            pltpu.VMEM((N_HEADS, HEAD_DIM, bq), F32),
        ],
        compiler_params=pltpu.CompilerParams(
            dimension_semantics=("arbitrary", "arbitrary"), vmem_limit_bytes=VMEM_LIMIT),
        name="dsa_attention",
    )(qt, qit, kw, k, vt, kw, tri)


def _post_kernel(x_ref, attn_ref, gyc_ref, sga_ref, mod_ref, wao_ref, wo_ref, o_ref):
    g1 = mod_ref[0, 2:3, :]
    y_attn = jnp.dot(attn_ref[...], wao_ref[...], preferred_element_type=F32)
    mix = gyc_ref[...] + sga_ref[...] * y_attn
    delta = jnp.dot(mix.astype(BF16), wo_ref[...], preferred_element_type=F32)
    o_ref[...] = x_ref[...] + g1 * delta


def _post_call(xf, attn, gyc, sga, mod3, wao, wo, seq):
    t, d = xf.shape
    bt = BT_PROJ
    tiles_per_seq = seq // bt
    aw = attn.shape[1]
    tile = lambda i: (i, 0)
    const = lambda i: (0, 0)
    return pl.pallas_call(
        _post_kernel,
        out_shape=jax.ShapeDtypeStruct((t, d), F32),
        grid=(t // bt,),
        in_specs=[
            pl.BlockSpec((bt, d), tile),
            pl.BlockSpec((bt, aw), tile),
            pl.BlockSpec((bt, d), tile),
            pl.BlockSpec((bt, d), tile),
            pl.BlockSpec((1, N_MOD, d), lambda i: (i // tiles_per_seq, 0, 0)),
            pl.BlockSpec(wao.shape, const),
            pl.BlockSpec(wo.shape, const),
        ],
        out_specs=pl.BlockSpec((bt, d), tile),
        compiler_params=pltpu.CompilerParams(
            dimension_semantics=("arbitrary",), vmem_limit_bytes=VMEM_LIMIT),
        name="post_mix",
    )(xf, attn, gyc, sga, mod3, wao, wo)


def _col_max(a):
    return jnp.max(a, axis=0, keepdims=True)


def _batcher_pairs(n):
    pairs = []
    p = 1
    while p < n:
        k = p
        while k >= 1:
            for j in range(k % p, n - k, 2 * k):
                for i in range(min(k, n - j - k)):
                    if (i + j) // (2 * p) == (i + j + k) // (2 * p):
                        pairs.append((i + j, i + j + k))
            k //= 2
        p *= 2
    return pairs


_SORT16 = _batcher_pairs(PEER_TOPK)
_BITONIC16 = [(k, k + d) for d in (8, 4, 2, 1) for k in range(PEER_TOPK) if not k & d]


def _sort_desc(v, pairs):
    v = list(v)
    for i, j in pairs:
        v[i], v[j] = jnp.maximum(v[i], v[j]), jnp.minimum(v[i], v[j])
    return v


def _peer_kernel(x_ref, mod_ref, n2_ref, wqt_ref, sk_ref, u_ref, vt_ref, o_ref,
                 h2t_ref, s1_ref, s2_ref, ap_ref, bp_ref, nsel_ref, rk_ref, w_ref, acc_ref):
    e = pl.program_id(1)
    n_e = pl.num_programs(1)
    tt = x_ref.shape[0]
    nk = PEER_N_KEYS
    n_i = u_ref.shape[0] // nk
    nslab = tt // LANES

    @pl.when(e == 0)
    def _route():
        xv = x_ref[...]
        sh2 = mod_ref[0, 3:4, :]
        sc2 = mod_ref[0, 4:5, :]
        ms = jnp.mean(xv * xv, axis=-1, keepdims=True)
        h2t = ((xv * lax.rsqrt(ms + EPS) * n2_ref[...]) * (1.0 + sc2) + sh2).T.astype(BF16)
        h2t_ref[...] = h2t
        qt = jnp.dot(wqt_ref[...], h2t, preferred_element_type=F32).astype(BF16)
        for h in range(PEER_HEADS):
            for p, dst in ((0, s1_ref), (1, s2_ref)):
                r0 = (h * 2 + p) * PEER_HALF
                dst[h] = jnp.dot(sk_ref[h, p], qt[r0:r0 + PEER_HALF, :],
                                 preferred_element_type=F32)

        neg_inf = jnp.float32(-jnp.inf)
        sub_id = lax.broadcasted_iota(I32, (SUBLANES, LANES), 0)

        def top16(s):
            v = _sort_desc([s[g * SUBLANES:(g + 1) * SUBLANES] for g in range(nk // SUBLANES)], _SORT16)
            for shift in (4, 2, 1):
                v = [jnp.maximum(v[r], pltpu.roll(v[PEER_TOPK - 1 - r], shift, 0)) for r in range(PEER_TOPK)]
                v = _sort_desc(v, _BITONIC16)
            return v

        def stack8(vals):
            out = vals[0]
            for r in range(1, SUBLANES):
                out = jnp.where(sub_id == r, vals[r], out)
            return out

        def route_head(h, carry):
            for c in range(nslab):
                ls = slice(c * LANES, (c + 1) * LANES)
                s1 = s1_ref[h, :, ls]
                s2 = s2_ref[h, :, ls]
                a = top16(s1)
                b = top16(s2)
                b_lo, b_hi = stack8(b[:SUBLANES]), stack8(b[SUBLANES:])
                cands = [a[0] + b_lo, a[0] + b_hi] + [a[r] + b_lo for r in range(1, PEER_TOPK)]
                work, t = cands, []
                for _ in range(PEER_TOPK):
                    m = _col_max(functools.reduce(jnp.maximum, work))
                    t.append(m)
                    work = [jnp.where(w == m, neg_inf, w) for w in work]
                thr = t[PEER_TOPK - 1]
                z = functools.reduce(jnp.add, [jnp.exp(tk - t[0]) for tk in t])
                hits = [jnp.where(cd >= thr, 1.0, 0.0) for cd in cands]
                n_sel = [jnp.sum(hits[0] + hits[1], axis=0, keepdims=True)]
                n_sel += [jnp.sum(hits[r + 1], axis=0, keepdims=True) for r in range(1, PEER_TOPK)]
                n_b = [jnp.broadcast_to(n, (SUBLANES, LANES)) for n in n_sel]
                cnts, ranks = [], []
                for g in range(nk // SUBLANES):
                    rows = slice(g * 8, (g + 1) * 8)
                    s1g, s2g = s1[rows], s2[rows]
                    cnt = jnp.zeros((SUBLANES, LANES), F32)
                    rank = jnp.full((SUBLANES, LANES), PEER_TOPK + 1.0, F32)
                    for r in range(PEER_TOPK):
                        cnt = jnp.where(s1g == a[r], n_b[r], cnt)
                        rank = jnp.where(s2g == b[r], r + 1.0, rank)
                    cnts.append(cnt)
                    ranks.append(rank)
                nsel_ref[h, :, ls] = jnp.concatenate(cnts, axis=0)
                rk_ref[h, :, ls] = jnp.concatenate(ranks, axis=0).astype(BF16)
                ap_ref[h, :, ls] = jnp.exp(s1 - a[0][0:1])
                bp_ref[h, :, ls] = (jnp.exp(s2 - b[0][0:1]) * (0.5 / z)).astype(BF16)
            return carry

        lax.fori_loop(0, PEER_HEADS, route_head, 0)
        acc_ref[...] = jnp.zeros(acc_ref.shape, F32)

    irow0 = pl.multiple_of(e * n_i, SUBLANES)
    pk = BF16_ROWS
    for c in range(nslab):
        ls = slice(c * LANES, (c + 1) * LANES)
        n_blk = [nsel_ref[h, pl.ds(irow0, n_i), ls] for h in range(PEER_HEADS)]
        ap_blk = [ap_ref[h, pl.ds(irow0, n_i), ls] for h in range(PEER_HEADS)]
        for ii in range(n_i):
            accs = [None] * (nk // pk)
            for h in range(PEER_HEADS):
                n16 = jnp.broadcast_to(n_blk[h][ii:ii + 1], (pk, LANES)).astype(BF16)
                a16 = jnp.broadcast_to(ap_blk[h][ii:ii + 1], (pk, LANES)).astype(BF16)
                for g in range(nk // pk):
                    rows = slice(g * pk, (g + 1) * pk)
                    hit = rk_ref[h, rows, ls] <= n16
                    term = jnp.where(hit, bp_ref[h, rows, ls], jnp.zeros((), BF16)) * a16
                    accs[g] = term if h == 0 else accs[g] + term
            for g in range(nk // pk):
                w_ref[ii * nk + g * pk:ii * nk + (g + 1) * pk, ls] = accs[g]

    act = jnp.dot(u_ref[...], h2t_ref[...], preferred_element_type=F32)
    gelu2 = act * (1.0 + lax.erf(act * (2.0 ** -0.5)))
    wa = w_ref[...] * gelu2.astype(BF16)
    acc_ref[...] += jnp.dot(vt_ref[0], wa, preferred_element_type=F32)

    @pl.when(e == n_e - 1)
    def _finish():
        g2 = mod_ref[0, 5:6, :]
        o_ref[...] = x_ref[...] + g2 * acc_ref[...].T


def _peer_call(x1, mod3, n2, wqt, sk, u, vt, seq):
    t, d = x1.shape
    tt = TT_PEER
    ec = EC_PEER
    n_chunks = u.shape[0] // ec
    assert ec % (SUBLANES * PEER_N_KEYS) == 0
    tiles_per_seq = seq // tt
    tile = lambda i, e: (i, 0)
    const2 = lambda i, e: (0, 0)
    return pl.pallas_call(
        _peer_kernel,
        out_shape=jax.ShapeDtypeStruct((t, d), F32),
        grid=(t // tt, n_chunks),
        in_specs=[
            pl.BlockSpec((tt, d), tile),
            pl.BlockSpec((1, N_MOD, d), lambda i, e: (i // tiles_per_seq, 0, 0)),
            pl.BlockSpec((1, d), const2),
            pl.BlockSpec(wqt.shape, const2),
            pl.BlockSpec(sk.shape, lambda i, e: (0, 0, 0, 0)),
            pl.BlockSpec((ec, d), lambda i, e: (e, 0)),
            pl.BlockSpec((1, d, ec), lambda i, e: (e, 0, 0)),
        ],
        out_specs=pl.BlockSpec((tt, d), tile),
        scratch_shapes=[
            pltpu.VMEM((d, tt), BF16),
            pltpu.VMEM((PEER_HEADS, PEER_N_KEYS, tt), F32),
            pltpu.VMEM((PEER_HEADS, PEER_N_KEYS, tt), F32),
            pltpu.VMEM((PEER_HEADS, PEER_N_KEYS, tt), F32),
            pltpu.VMEM((PEER_HEADS, PEER_N_KEYS, tt), BF16),
            pltpu.VMEM((PEER_HEADS, PEER_N_KEYS, tt), F32),
            pltpu.VMEM((PEER_HEADS, PEER_N_KEYS, tt), BF16),
            pltpu.VMEM((ec, tt), BF16),
            pltpu.VMEM((d, tt), F32),
        ],
        compiler_params=pltpu.CompilerParams(
            dimension_semantics=("arbitrary", "arbitrary"), vmem_limit_bytes=VMEM_LIMIT),
        name="peer",
    )(x1, mod3, n2, wqt, sk, u, vt)


def kernel(x, c, w_ada, b_ada, norm1_w, w_in, conv_w, w_conv_out, q_norm_w, k_norm_w,
           w_attn_out, w_o, norm2_w, w_peer_q, peer_sub_keys, peer_u, peer_v):
    b, s, d = x.shape
    depth = w_ada.shape[0]
    cw = conv_w.shape[2]
    aw = N_HEADS * HEAD_DIM
    iw = IDX_HEADS * IDX_DIM
    assert s % BQ == 0 and s % TT_PEER == 0 and s % BT_PROJ == 0
    assert peer_u.shape[1] == PEER_N_KEYS * PEER_N_KEYS

    gidx = jnp.arange(aw) // HEAD_DIM
    gmat = (gidx[:, None] == gidx[None, :]).astype(BF16)
    tri = (jnp.arange(BK)[:, None] >= jnp.arange(BK)[None, :]).astype(BF16)

    xf = x.reshape(b * s, d)
    for layer in range(depth):
        w = w_in[layer]
        o = 0
        wconv = w[:, o:o + 3 * cw].astype(BF16); o += 3 * cw
        wqkv = w[:, o:o + 3 * aw].astype(BF16); o += 3 * aw
        wqi = w[:, o:o + iw].astype(BF16); o += iw
        wkw = jnp.pad(w[:, o:o + IDX_DIM + IDX_HEADS], ((0, 0), (0, LANES - IDX_DIM - IDX_HEADS))).astype(BF16)
        o += IDX_DIM + IDX_HEADS
        wg = w[:, o:o + 2 * d].astype(BF16)
        qn_t = jnp.tile(q_norm_w[layer], N_HEADS).reshape(1, aw)
        kn_t = jnp.tile(k_norm_w[layer], N_HEADS).reshape(1, aw)

        mod3 = _mod_call(c, w_ada[layer], b_ada[layer]).reshape(b, N_MOD, d)
        gyc, sga, qt, k, vt, qit, kw = _inproj_call(
            xf, mod3, norm1_w[layer].reshape(1, d), wconv, wqkv, wqi, wkw, wg, conv_w[layer],
            w_conv_out[layer].astype(BF16), qn_t, kn_t, gmat, s)
        attn = _dsa_call(qt, qit, kw, k, vt, tri, b, s)
        x1 = _post_call(xf, attn, gyc, sga, mod3, w_attn_out[layer].astype(BF16),
                        w_o[layer].astype(BF16), s)
        xf = _peer_call(
            x1, mod3, norm2_w[layer].reshape(1, d), w_peer_q[layer].T.astype(BF16),
            peer_sub_keys[layer].astype(BF16), peer_u[layer].astype(BF16),
            peer_v[layer].astype(BF16).reshape(-1, EC_PEER, d).transpose(0, 2, 1), s)
    return xf.reshape(b, s, d)
```

```python
import functools

import jax
import jax.numpy as jnp
from jax import lax
from jax.experimental import pallas as pl
from jax.experimental.pallas import tpu as pltpu

F32 = jnp.float32
BF16 = jnp.bfloat16
I32 = jnp.int32

N_HEADS = 8
HEAD_DIM = 64
IDX_HEADS = 8
IDX_DIM = 64
TOPK_ATTN = 256
PEER_HEADS = 8
PEER_N_KEYS = 128
PEER_HALF = 64
PEER_TOPK = 16
N_MOD = 6
EPS = 1e-6

LANES = 128
SUBLANES = 8
BF16_ROWS = 16
INT_MIN = -(2 ** 31)
NEG_BIG = -1e30
LOG2E = 1.4426950408889634
VMEM_LIMIT = 56 * 1024 * 1024

BT_PROJ = 512
BQ = 512
BK = 256
TT_PEER = 512
EC_PEER = 2048
PEER_IBLK = 4


def _split_bf16(a):
    hi = a.astype(BF16)
    lo = (a - hi.astype(F32)).astype(BF16)
    return hi, lo


def _mod_kernel(c_ref, w_ref, b_ref, o_ref):
    c = c_ref[...]
    s = c * jax.nn.sigmoid(c)
    s_hi, s_lo = _split_bf16(s)
    w_hi, w_lo = _split_bf16(w_ref[...])
    acc = jnp.dot(s_hi, w_hi, preferred_element_type=F32)
    acc += jnp.dot(s_hi, w_lo, preferred_element_type=F32)
    acc += jnp.dot(s_lo, w_hi, preferred_element_type=F32)
    o_ref[...] = acc + b_ref[...]


def _mod_call(c, w_ada, b_ada):
    b, d = c.shape
    n = w_ada.shape[1]
    bn = d
    return pl.pallas_call(
        _mod_kernel,
        out_shape=jax.ShapeDtypeStruct((b, n), F32),
        grid=(n // bn,),
        in_specs=[
            pl.BlockSpec((b, d), lambda j: (0, 0)),
            pl.BlockSpec((d, bn), lambda j: (0, j)),
            pl.BlockSpec((1, bn), lambda j: (0, j)),
        ],
        out_specs=pl.BlockSpec((b, bn), lambda j: (0, j)),
        compiler_params=pltpu.CompilerParams(
            dimension_semantics=("arbitrary",), vmem_limit_bytes=VMEM_LIMIT),
        name="adaln_mod",
    )(c, w_ada, b_ada.reshape(1, n))


def _inproj_kernel(x_ref, xh_ref, mod_ref, n1_ref, wconv_ref, wqkv_ref, wqi_ref, wkw_ref,
                   wg_ref, convw_ref, wco_ref, qn_ref, kn_ref, gmat_ref,
                   gyc_ref, sga_ref, q_ref, k_ref, v_ref, qi_ref, kw_ref, *, tiles_per_seq):
    i = pl.program_id(0)
    cw = convw_ref.shape[1]
    aw = q_ref.shape[1]
    d = x_ref.shape[1]
    bt = x_ref.shape[0]
    sh1 = mod_ref[0, 0:1, :]
    sc1 = mod_ref[0, 1:2, :]

    def norm_mod(xv):
        ms = jnp.mean(xv * xv, axis=-1, keepdims=True)
        y = xv * lax.rsqrt(ms + EPS) * n1_ref[...]
        return (y * (1.0 + sc1) + sh1).astype(BF16)

    h = norm_mod(x_ref[...])
    hh = norm_mod(xh_ref[...])

    pc = jnp.dot(h, wconv_ref[...], preferred_element_type=F32)
    cb, cc, cx = pc[:, :cw], pc[:, cw:2 * cw], pc[:, 2 * cw:]
    u = cc * cx
    ph = jnp.dot(hh, wconv_ref[:, cw:], preferred_element_type=F32)
    uh = ph[:, :cw] * ph[:, cw:]
    uh = jnp.where(i % tiles_per_seq == 0, 0.0, uh)
    row = lax.broadcasted_iota(I32, (bt, cw), 0)
    prev1 = uh[SUBLANES - 1:SUBLANES, :]
    prev2 = uh[SUBLANES - 2:SUBLANES - 1, :]
    u1 = jnp.where(row == 0, prev1, pltpu.roll(u, 1, 0))
    u2 = pltpu.roll(u, 2, 0)
    u2 = jnp.where(row == 0, prev2, jnp.where(row == 1, prev1, u2))
    conv = convw_ref[0:1, :] * u2 + convw_ref[1:2, :] * u1 + convw_ref[2:3, :] * u
    yc = jnp.dot((cb * conv).astype(BF16), wco_ref[...], preferred_element_type=F32)

    pg = jnp.dot(h, wg_ref[...], preferred_element_type=F32)
    gyc_ref[...] = (jax.nn.sigmoid(pg[:, :d]) * yc).astype(gyc_ref.dtype)
    sga_ref[...] = jax.nn.sigmoid(pg[:, d:]).astype(sga_ref.dtype)

    pq = jnp.dot(h, wqkv_ref[...], preferred_element_type=F32)
    gmat = gmat_ref[...]

    def head_norm(t, w_row):
        hi, lo = _split_bf16(t * t)
        ss = (jnp.dot(hi, gmat, preferred_element_type=F32)
              + jnp.dot(lo, gmat, preferred_element_type=F32))
        return t * lax.rsqrt(ss * (1.0 / HEAD_DIM) + EPS) * w_row

    qn = head_norm(pq[:, :aw], qn_ref[...])
    qs = qn * (HEAD_DIM ** -0.5 * LOG2E)
    k_ref[...] = head_norm(pq[:, aw:2 * aw], kn_ref[...]).astype(BF16)
    vv = pq[:, 2 * aw:]
    qi = jnp.dot(h, wqi_ref[...], preferred_element_type=F32)
    tw = q_ref.shape[2]
    for s in range(bt // tw):
        rows = slice(s * tw, (s + 1) * tw)
        q_ref[s] = qs[rows].T.astype(BF16)
        v_ref[s] = vv[rows].T.astype(BF16)
        qi_ref[s] = qi[rows].T.astype(BF16)
    kw_ref[...] = jnp.dot(h, wkw_ref[...], preferred_element_type=F32)


def _inproj_call(xf, mod3, n1, wconv, wqkv, wqi, wkw, wg, convw, wco, qn_t, kn_t, gmat, seq):
    t, d = xf.shape
    bt = BT_PROJ
    tw = BK
    tiles_per_seq = seq // bt
    cw = convw.shape[1]
    aw = qn_t.shape[1]
    const = lambda i: (0, 0)
    tile = lambda i: (i, 0)
    tile_t = lambda i: (i, 0, 0)
    resident = lambda a: pl.BlockSpec(a.shape, const, pipeline_mode=pl.Buffered(1))
    halo = lambda i: (jnp.maximum(i * (bt // SUBLANES) - 1, 0), 0)
    kern = functools.partial(_inproj_kernel, tiles_per_seq=tiles_per_seq)
    return pl.pallas_call(
        kern,
        out_shape=[
            jax.ShapeDtypeStruct((t, d), BF16),
            jax.ShapeDtypeStruct((t, d), BF16),
            jax.ShapeDtypeStruct((t // tw, aw, tw), BF16),
            jax.ShapeDtypeStruct((t, aw), BF16),
            jax.ShapeDtypeStruct((t // tw, aw, tw), BF16),
            jax.ShapeDtypeStruct((t // tw, aw, tw), BF16),
            jax.ShapeDtypeStruct((t, LANES), F32),
        ],
        grid=(t // bt,),
        in_specs=[
            pl.BlockSpec((bt, d), tile),
            pl.BlockSpec((SUBLANES, d), halo),
            pl.BlockSpec((1, N_MOD, d), lambda i: (i // tiles_per_seq, 0, 0)),
            pl.BlockSpec((1, d), const),
            resident(wconv),
            resident(wqkv),
            resident(wqi),
            resident(wkw),
            resident(wg),
            pl.BlockSpec(convw.shape, const),
            resident(wco),
            pl.BlockSpec((1, aw), const),
            pl.BlockSpec((1, aw), const),
            resident(gmat),
        ],
        out_specs=[
            pl.BlockSpec((bt, d), tile),
            pl.BlockSpec((bt, d), tile),
            pl.BlockSpec((bt // tw, aw, tw), tile_t),
            pl.BlockSpec((bt, aw), tile),
            pl.BlockSpec((bt // tw, aw, tw), tile_t),
            pl.BlockSpec((bt // tw, aw, tw), tile_t),
            pl.BlockSpec((bt, LANES), tile),
        ],
        compiler_params=pltpu.CompilerParams(
            dimension_semantics=("arbitrary",), vmem_limit_bytes=VMEM_LIMIT),
        name="inproj",
    )(xf, xf, mod3, n1, wconv, wqkv, wqi, wkw, wg, convw, wco, qn_t, kn_t, gmat)


def _tree_sum(parts):
    parts = list(parts)
    while len(parts) > 1:
        nxt = [parts[a] + parts[a + 1] for a in range(0, len(parts) - 1, 2)]
        if len(parts) % 2:
            nxt.append(parts[-1])
        parts = nxt
    return parts[0]


def _dsa_kernel(qt_ref, qit_ref, kwq_ref, k_ref, vt_ref, kwk_ref, tri_ref, o_ref,
                keys_ref, bias_ref, qz_ref, qiz_ref, s_ref, m_ref, l_ref, acc_ref, *, n_sel):
    i = pl.program_id(1)
    n_qt, _, tw = qt_ref.shape
    bq = n_qt * tw
    bk = BK
    n_tiles = (i * bq) // bk + bq // bk
    ngrp = bk // SUBLANES

    zpad = jnp.zeros((LANES - HEAD_DIM, bq), BF16)
    for h in range(N_HEADS):
        qh = jnp.concatenate([qt_ref[s, h * HEAD_DIM:(h + 1) * HEAD_DIM, :] for s in range(n_qt)], axis=1)
        qih = jnp.concatenate([qit_ref[s, h * IDX_DIM:(h + 1) * IDX_DIM, :] for s in range(n_qt)], axis=1)
        qz_ref[h] = jnp.concatenate([qh, zpad] if h % 2 == 0 else [zpad, qh], axis=0)
        qiz_ref[h] = jnp.concatenate([qih, zpad], axis=0)
    wt = kwq_ref[...].T

    q_pos = i * bq + lax.broadcasted_iota(I32, (bk, bq), 1)

    def score_tile(j, carry):
        kw = kwk_ref[pl.ds(pl.multiple_of(j * bk, bk), bk), :].astype(BF16)
        score = jnp.zeros((bk, bq), F32)
        for h in range(IDX_HEADS):
            logits = jnp.dot(kw, qiz_ref[h], preferred_element_type=F32)
            score = score + wt[IDX_DIM + h:IDX_DIM + h + 1, :] * jnp.maximum(logits, 0.0)
        bits = pltpu.bitcast(score, I32)
        key = jnp.where(bits < 0, bits ^ jnp.int32(0x7FFFFFFF), bits)
        key = jnp.where(bits == jnp.int32(INT_MIN), 0, key)
        key_pos = j * bk + lax.broadcasted_iota(I32, (bk, bq), 0)
        keys_ref[j] = jnp.where(key_pos <= q_pos, key, jnp.int32(INT_MIN))
        return carry

    lax.fori_loop(0, n_tiles, score_tile, 0)

    def count_ge(cand, strict):
        cand_b = jnp.broadcast_to(cand, (SUBLANES, bq))

        def body(j, cnt):
            hits = []
            for g in range(ngrp):
                ks = keys_ref[j, g * SUBLANES:(g + 1) * SUBLANES, :]
                hit = (ks > cand_b) if strict else (ks >= cand_b)
                hits.append(jnp.where(hit, 1, 0))
            return cnt + _tree_sum(hits)

        cnt = lax.fori_loop(0, n_tiles, body, jnp.zeros((SUBLANES, bq), I32))
        return jnp.sum(cnt, axis=0, keepdims=True)

    def bit_step(b, t_u):
        cand_u = t_u | lax.shift_left(jnp.int32(1), 31 - b)
        total = count_ge(cand_u ^ jnp.int32(INT_MIN), False)
        return jnp.where(total >= n_sel, cand_u, t_u)

    t_u = lax.fori_loop(0, 32, bit_step, jnp.zeros((1, bq), I32))
    thr = t_u ^ jnp.int32(INT_MIN)
    n_gt = count_ge(thr, True)
    need = jnp.where(thr == jnp.int32(INT_MIN), 0, n_sel - n_gt).astype(F32)

    tri = tri_ref[...]

    def bias_tile(j, carry):
        kt = keys_ref[j]
        eq = kt == thr
        eq_f = jnp.where(eq, 1.0, 0.0)
        prefix = jnp.dot(tri, eq_f.astype(BF16), preferred_element_type=F32) + carry
        sel = (kt > thr) | (eq & (prefix <= need))
        bias_ref[j] = jnp.where(sel, 0.0, NEG_BIG)
        return prefix[bk - 1:bk, :]

    lax.fori_loop(0, n_tiles, bias_tile, jnp.zeros((1, bq), F32))

    m_ref[...] = jnp.full(m_ref.shape, NEG_BIG, F32)
    l_ref[...] = jnp.zeros(l_ref.shape, F32)
    acc_ref[...] = jnp.zeros(acc_ref.shape, F32)

    def attn_tile(j, carry):
        kt = k_ref[pl.ds(pl.multiple_of(j * bk, bk), bk), :]
        bias = bias_ref[j]
        alphas = []
        for h in range(N_HEADS):
            pair = slice((h // 2) * LANES, (h // 2 + 1) * LANES)
            s = jnp.dot(kt[:, pair], qz_ref[h], preferred_element_type=F32) + bias
            s_ref[h] = s
            m_prev = m_ref[h]
            m_new = jnp.maximum(m_prev, jnp.broadcast_to(jnp.max(s, axis=0, keepdims=True), (SUBLANES, bq)))
            m_ref[h] = m_new
            alphas.append(jnp.exp2(m_prev - m_new))
        for h in range(N_HEADS):
            p = jnp.exp2(s_ref[h] - m_ref[h, 0:1, :])
            l_ref[h] = alphas[h] * l_ref[h] + _tree_sum([p[g * SUBLANES:(g + 1) * SUBLANES] for g in range(ngrp)])
            pv = jnp.dot(vt_ref[j, h * HEAD_DIM:(h + 1) * HEAD_DIM, :], p.astype(BF16),
                         preferred_element_type=F32)
            acc_ref[h] = alphas[h][0:1] * acc_ref[h] + pv
        return carry

    lax.fori_loop(0, n_tiles, attn_tile, 0)

    outs = [acc_ref[h] / jnp.sum(l_ref[h], axis=0, keepdims=True) for h in range(N_HEADS)]
    o_ref[...] = jnp.concatenate(outs, axis=0).T.astype(o_ref.dtype)


def _dsa_call(qt, qit, kw, k, vt, tri, batch, seq):
    nt, aw, tw = qt.shape
    bq = BQ
    assert tw == BK and bq % tw == 0 and seq % bq == 0
    nq = seq // bq
    n_sel = min(TOPK_ATTN, seq // 4)
    qtile = lambda b, i: (b * nq + i, 0)
    qtile_t = lambda b, i: (b * nq + i, 0, 0)
    whole = lambda b, i: (b, 0)
    kern = functools.partial(_dsa_kernel, n_sel=n_sel)
    return pl.pallas_call(
        kern,
        out_shape=jax.ShapeDtypeStruct((batch * seq, aw), BF16),
        grid=(batch, nq),
        in_specs=[
            pl.BlockSpec((bq // tw, aw, tw), qtile_t),
            pl.BlockSpec((bq // tw, aw, tw), qtile_t),
            pl.BlockSpec((bq, LANES), qtile),
            pl.BlockSpec((seq, aw), whole),
            pl.BlockSpec((seq // BK, aw, BK), lambda b, i: (b, 0, 0)),
            pl.BlockSpec((seq, LANES), whole),
            pl.BlockSpec(tri.shape, lambda b, i: (0, 0)),
        ],
        out_specs=pl.BlockSpec((bq, aw), qtile),
        scratch_shapes=[
            pltpu.VMEM((seq // BK, BK, bq), I32),
            pltpu.VMEM((seq // BK, BK, bq), F32),
            pltpu.VMEM((N_HEADS, LANES, bq), BF16),
            pltpu.VMEM((IDX_HEADS, LANES, bq), BF16),
            pltpu.VMEM((N_HEADS, BK, bq), F32),
            pltpu.VMEM((N_HEADS, SUBLANES, bq), F32),
            pltpu.VMEM((N_HEADS, SUBLANES, bq), F32),
            pltpu.VMEM((N_HEADS, HEAD_DIM, bq), F32),
        ],
        compiler_params=pltpu.CompilerParams(
            dimension_semantics=("arbitrary", "arbitrary"), vmem_limit_bytes=VMEM_LIMIT),
        name="dsa_attention",
    )(qt, qit, kw, k, vt, kw, tri)


def _post_kernel(x_ref, attn_ref, gyc_ref, sga_ref, mod_ref, wao_ref, wo_ref, o_ref):
    g1 = mod_ref[0, 2:3, :]
    y_attn = jnp.dot(attn_ref[...], wao_ref[...], preferred_element_type=F32)
    mix = gyc_ref[...] + sga_ref[...] * y_attn
    delta = jnp.dot(mix.astype(BF16), wo_ref[...], preferred_element_type=F32)
    o_ref[...] = x_ref[...] + g1 * delta


def _post_call(xf, attn, gyc, sga, mod3, wao, wo, seq):
    t, d = xf.shape
    bt = BT_PROJ
    tiles_per_seq = seq // bt
    aw = attn.shape[1]
    tile = lambda i: (i, 0)
    const = lambda i: (0, 0)
    return pl.pallas_call(
        _post_kernel,
        out_shape=jax.ShapeDtypeStruct((t, d), F32),
        grid=(t // bt,),
        in_specs=[
            pl.BlockSpec((bt, d), tile),
            pl.BlockSpec((bt, aw), tile),
            pl.BlockSpec((bt, d), tile),
            pl.BlockSpec((bt, d), tile),
            pl.BlockSpec((1, N_MOD, d), lambda i: (i // tiles_per_seq, 0, 0)),
            pl.BlockSpec(wao.shape, const),
            pl.BlockSpec(wo.shape, const),
        ],
        out_specs=pl.BlockSpec((bt, d), tile),
        compiler_params=pltpu.CompilerParams(
            dimension_semantics=("arbitrary",), vmem_limit_bytes=VMEM_LIMIT),
        name="post_mix",
    )(xf, attn, gyc, sga, mod3, wao, wo)


def _col_max(a):
    return jnp.max(a, axis=0, keepdims=True)


def _batcher_pairs(n):
    pairs = []
    p = 1
    while p < n:
        k = p
        while k >= 1:
            for j in range(k % p, n - k, 2 * k):
                for i in range(min(k, n - j - k)):
                    if (i + j) // (2 * p) == (i + j + k) // (2 * p):
                        pairs.append((i + j, i + j + k))
            k //= 2
        p *= 2
    return pairs


_SORT16 = _batcher_pairs(PEER_TOPK)
_BITONIC16 = [(k, k + d) for d in (8, 4, 2, 1) for k in range(PEER_TOPK) if not k & d]


def _sort_desc(v, pairs):
    v = list(v)
    for i, j in pairs:
        v[i], v[j] = jnp.maximum(v[i], v[j]), jnp.minimum(v[i], v[j])
    return v


def _peer_kernel(x_ref, mod_ref, n2_ref, wqt_ref, sk_ref, u_ref, vt_ref, o_ref,
                 h2t_ref, s1_ref, s2_ref, ap_ref, bp_ref, nsel_ref, rk_ref, w_ref, acc_ref):
    e = pl.program_id(1)
    n_e = pl.num_programs(1)
    tt = x_ref.shape[0]
    nk = PEER_N_KEYS
    n_i = u_ref.shape[0] // nk
    nslab = tt // LANES

    @pl.when(e == 0)
    def _route():
        xv = x_ref[...]
        sh2 = mod_ref[0, 3:4, :]
        sc2 = mod_ref[0, 4:5, :]
        ms = jnp.mean(xv * xv, axis=-1, keepdims=True)
        h2t = ((xv * lax.rsqrt(ms + EPS) * n2_ref[...]) * (1.0 + sc2) + sh2).T.astype(BF16)
        h2t_ref[...] = h2t
        qt = jnp.dot(wqt_ref[...], h2t, preferred_element_type=F32).astype(BF16)
        for h in range(PEER_HEADS):
            for p, dst in ((0, s1_ref), (1, s2_ref)):
                r0 = (h * 2 + p) * PEER_HALF
                dst[h] = jnp.dot(sk_ref[h, p], qt[r0:r0 + PEER_HALF, :],
                                 preferred_element_type=F32)

        neg_inf = jnp.float32(-jnp.inf)
        sub_id = lax.broadcasted_iota(I32, (SUBLANES, LANES), 0)

        def top16(s):
            v = _sort_desc([s[g * SUBLANES:(g + 1) * SUBLANES] for g in range(nk // SUBLANES)], _SORT16)
            for shift in (4, 2, 1):
                v = [jnp.maximum(v[r], pltpu.roll(v[PEER_TOPK - 1 - r], shift, 0)) for r in range(PEER_TOPK)]
                v = _sort_desc(v, _BITONIC16)
            return v

        def stack8(vals):
            out = vals[0]
            for r in range(1, SUBLANES):
                out = jnp.where(sub_id == r, vals[r], out)
            return out

        def route_head(h, carry):
            for c in range(nslab):
                ls = slice(c * LANES, (c + 1) * LANES)
                s1 = s1_ref[h, :, ls]
                s2 = s2_ref[h, :, ls]
                a = top16(s1)
                b = top16(s2)
                b_lo, b_hi = stack8(b[:SUBLANES]), stack8(b[SUBLANES:])
                cands = [a[0] + b_lo, a[0] + b_hi] + [a[r] + b_lo for r in range(1, PEER_TOPK)]
                work, t = cands, []
                for _ in range(PEER_TOPK):
                    m = _col_max(functools.reduce(jnp.maximum, work))
                    t.append(m)
                    work = [jnp.where(w == m, neg_inf, w) for w in work]
                thr = t[PEER_TOPK - 1]
                z = functools.reduce(jnp.add, [jnp.exp(tk - t[0]) for tk in t])
                hits = [jnp.where(cd >= thr, 1.0, 0.0) for cd in cands]
                n_sel = [jnp.sum(hits[0] + hits[1], axis=0, keepdims=True)]
                n_sel += [jnp.sum(hits[r + 1], axis=0, keepdims=True) for r in range(1, PEER_TOPK)]
                n_b = [jnp.broadcast_to(n, (SUBLANES, LANES)) for n in n_sel]
                cnts, ranks = [], []
                for g in range(nk // SUBLANES):
                    rows = slice(g * 8, (g + 1) * 8)
                    s1g, s2g = s1[rows], s2[rows]
                    cnt = jnp.zeros((SUBLANES, LANES), F32)
                    rank = jnp.full((SUBLANES, LANES), PEER_TOPK + 1.0, F32)
                    for r in range(PEER_TOPK):
                        cnt = jnp.where(s1g == a[r], n_b[r], cnt)
                        rank = jnp.where(s2g == b[r], r + 1.0, rank)
                    cnts.append(cnt)
                    ranks.append(rank)
                nsel_ref[h, :, ls] = jnp.concatenate(cnts, axis=0)
                rk_ref[h, :, ls] = jnp.concatenate(ranks, axis=0).astype(BF16)
                ap_ref[h, :, ls] = jnp.exp(s1 - a[0][0:1])
                bp_ref[h, :, ls] = (jnp.exp(s2 - b[0][0:1]) * (0.5 / z)).astype(BF16)
            return carry

        lax.fori_loop(0, PEER_HEADS, route_head, 0)
        acc_ref[...] = jnp.zeros(acc_ref.shape, F32)

    irow0 = pl.multiple_of(e * n_i, SUBLANES)
    pk = BF16_ROWS
    n_blk = [[nsel_ref[h, pl.ds(irow0, n_i), c * LANES:(c + 1) * LANES] for h in range(PEER_HEADS)]
             for c in range(nslab)]
    ap_blk = [[ap_ref[h, pl.ds(irow0, n_i), c * LANES:(c + 1) * LANES] for h in range(PEER_HEADS)]
              for c in range(nslab)]
    h2t = h2t_ref[...]
    was = []
    for i0 in range(0, n_i, PEER_IBLK):
        for c in range(nslab):
            ls = slice(c * LANES, (c + 1) * LANES)
            for ii in range(i0, i0 + PEER_IBLK):
                accs = [None] * (nk // pk)
                for h in range(PEER_HEADS):
                    n16 = jnp.broadcast_to(n_blk[c][h][ii:ii + 1], (pk, LANES)).astype(BF16)
                    a16 = jnp.broadcast_to(ap_blk[c][h][ii:ii + 1], (pk, LANES)).astype(BF16)
                    for g in range(nk // pk):
                        rows = slice(g * pk, (g + 1) * pk)
                        hit = rk_ref[h, rows, ls] <= n16
                        term = jnp.where(hit, bp_ref[h, rows, ls], jnp.zeros((), BF16)) * a16
                        accs[g] = term if h == 0 else accs[g] + term
                for g in range(nk // pk):
                    w_ref[ii * nk + g * pk:ii * nk + (g + 1) * pk, ls] = accs[g]
        rows = slice(i0 * nk, (i0 + PEER_IBLK) * nk)
        act = jnp.dot(u_ref[rows, :], h2t, preferred_element_type=F32)
        gelu2 = act * (1.0 + lax.erf(act * (2.0 ** -0.5)))
        was.append(w_ref[rows, :] * gelu2.astype(BF16))
    acc_ref[...] += jnp.dot(vt_ref[0], jnp.concatenate(was, axis=0), preferred_element_type=F32)

    @pl.when(e == n_e - 1)
    def _finish():
        g2 = mod_ref[0, 5:6, :]
        o_ref[...] = x_ref[...] + g2 * acc_ref[...].T


def _peer_call(x1, mod3, n2, wqt, sk, u, vt, seq):
    t, d = x1.shape
    tt = TT_PEER
    ec = EC_PEER
    n_chunks = u.shape[0] // ec
    assert ec % (SUBLANES * PEER_N_KEYS) == 0
    tiles_per_seq = seq // tt
    tile = lambda i, e: (i, 0)
    const2 = lambda i, e: (0, 0)
    return pl.pallas_call(
        _peer_kernel,
        out_shape=jax.ShapeDtypeStruct((t, d), F32),
        grid=(t // tt, n_chunks),
        in_specs=[
            pl.BlockSpec((tt, d), tile),
            pl.BlockSpec((1, N_MOD, d), lambda i, e: (i // tiles_per_seq, 0, 0)),
            pl.BlockSpec((1, d), const2),
            pl.BlockSpec(wqt.shape, const2),
            pl.BlockSpec(sk.shape, lambda i, e: (0, 0, 0, 0)),
            pl.BlockSpec((ec, d), lambda i, e: (e, 0)),
            pl.BlockSpec((1, d, ec), lambda i, e: (e, 0, 0)),
        ],
        out_specs=pl.BlockSpec((tt, d), tile),
        scratch_shapes=[
            pltpu.VMEM((d, tt), BF16),
            pltpu.VMEM((PEER_HEADS, PEER_N_KEYS, tt), F32),
            pltpu.VMEM((PEER_HEADS, PEER_N_KEYS, tt), F32),
            pltpu.VMEM((PEER_HEADS, PEER_N_KEYS, tt), F32),
            pltpu.VMEM((PEER_HEADS, PEER_N_KEYS, tt), BF16),
            pltpu.VMEM((PEER_HEADS, PEER_N_KEYS, tt), F32),
            pltpu.VMEM((PEER_HEADS, PEER_N_KEYS, tt), BF16),
            pltpu.VMEM((ec, tt), BF16),
            pltpu.VMEM((d, tt), F32),
        ],
        compiler_params=pltpu.CompilerParams(
            dimension_semantics=("arbitrary", "arbitrary"), vmem_limit_bytes=VMEM_LIMIT),
        name="peer",
    )(x1, mod3, n2, wqt, sk, u, vt)


def kernel(x, c, w_ada, b_ada, norm1_w, w_in, conv_w, w_conv_out, q_norm_w, k_norm_w,
           w_attn_out, w_o, norm2_w, w_peer_q, peer_sub_keys, peer_u, peer_v):
    b, s, d = x.shape
    depth = w_ada.shape[0]
    cw = conv_w.shape[2]
    aw = N_HEADS * HEAD_DIM
    iw = IDX_HEADS * IDX_DIM
    assert s % BQ == 0 and s % TT_PEER == 0 and s % BT_PROJ == 0
    assert peer_u.shape[1] == PEER_N_KEYS * PEER_N_KEYS

    gidx = jnp.arange(aw) // HEAD_DIM
    gmat = (gidx[:, None] == gidx[None, :]).astype(BF16)
    tri = (jnp.arange(BK)[:, None] >= jnp.arange(BK)[None, :]).astype(BF16)

    xf = x.reshape(b * s, d)
    for layer in range(depth):
        w = w_in[layer]
        o = 0
        wconv = w[:, o:o + 3 * cw].astype(BF16); o += 3 * cw
        wqkv = w[:, o:o + 3 * aw].astype(BF16); o += 3 * aw
        wqi = w[:, o:o + iw].astype(BF16); o += iw
        wkw = jnp.pad(w[:, o:o + IDX_DIM + IDX_HEADS], ((0, 0), (0, LANES - IDX_DIM - IDX_HEADS))).astype(BF16)
        o += IDX_DIM + IDX_HEADS
        wg = w[:, o:o + 2 * d].astype(BF16)
        qn_t = jnp.tile(q_norm_w[layer], N_HEADS).reshape(1, aw)
        kn_t = jnp.tile(k_norm_w[layer], N_HEADS).reshape(1, aw)

        mod3 = _mod_call(c, w_ada[layer], b_ada[layer]).reshape(b, N_MOD, d)
        gyc, sga, qt, k, vt, qit, kw = _inproj_call(
            xf, mod3, norm1_w[layer].reshape(1, d), wconv, wqkv, wqi, wkw, wg, conv_w[layer],
            w_conv_out[layer].astype(BF16), qn_t, kn_t, gmat, s)
        attn = _dsa_call(qt, qit, kw, k, vt, tri, b, s)
        x1 = _post_call(xf, attn, gyc, sga, mod3, w_attn_out[layer].astype(BF16),
                        w_o[layer].astype(BF16), s)
        xf = _peer_call(
            x1, mod3, norm2_w[layer].reshape(1, d), w_peer_q[layer].T.astype(BF16),
            peer_sub_keys[layer].astype(BF16), peer_u[layer].astype(BF16),
            peer_v[layer].astype(BF16).reshape(-1, EC_PEER, d).transpose(0, 2, 1), s)
    return xf.reshape(b, s, d)
```

```python
import functools

import jax
import jax.numpy as jnp
from jax import lax
from jax.experimental import pallas as pl
from jax.experimental.pallas import tpu as pltpu

F32 = jnp.float32
BF16 = jnp.bfloat16
I32 = jnp.int32

N_HEADS = 8
HEAD_DIM = 64
IDX_HEADS = 8
IDX_DIM = 64
TOPK_ATTN = 256
PEER_HEADS = 8
PEER_N_KEYS = 128
PEER_HALF = 64
PEER_TOPK = 16
N_MOD = 6
EPS = 1e-6

LANES = 128
SUBLANES = 8
BF16_ROWS = 16
INT_MIN = -(2 ** 31)
NEG_BIG = -1e30
LOG2E = 1.4426950408889634
VMEM_LIMIT = 56 * 1024 * 1024

BT_PROJ = 512
BQ = 512
BK = 256
TT_PEER = 512
EC_PEER = 2048
PEER_IBLK = 1


def _split_bf16(a):
    hi = a.astype(BF16)
    lo = (a - hi.astype(F32)).astype(BF16)
    return hi, lo


def _mod_kernel(c_ref, w_ref, b_ref, o_ref):
    c = c_ref[...]
    s = c * jax.nn.sigmoid(c)
    s_hi, s_lo = _split_bf16(s)
    w_hi, w_lo = _split_bf16(w_ref[...])
    acc = jnp.dot(s_hi, w_hi, preferred_element_type=F32)
    acc += jnp.dot(s_hi, w_lo, preferred_element_type=F32)
    acc += jnp.dot(s_lo, w_hi, preferred_element_type=F32)
    o_ref[...] = acc + b_ref[...]


def _mod_call(c, w_ada, b_ada):
    b, d = c.shape
    n = w_ada.shape[1]
    bn = d
    return pl.pallas_call(
        _mod_kernel,
        out_shape=jax.ShapeDtypeStruct((b, n), F32),
        grid=(n // bn,),
        in_specs=[
            pl.BlockSpec((b, d), lambda j: (0, 0)),
            pl.BlockSpec((d, bn), lambda j: (0, j)),
            pl.BlockSpec((1, bn), lambda j: (0, j)),
        ],
        out_specs=pl.BlockSpec((b, bn), lambda j: (0, j)),
        compiler_params=pltpu.CompilerParams(
            dimension_semantics=("arbitrary",), vmem_limit_bytes=VMEM_LIMIT),
        name="adaln_mod",
    )(c, w_ada, b_ada.reshape(1, n))


def _inproj_kernel(x_ref, xh_ref, mod_ref, n1_ref, wconv_ref, wqkv_ref, wqi_ref, wkw_ref,
                   wg_ref, convw_ref, wco_ref, qn_ref, kn_ref, gmat_ref,
                   gyc_ref, sga_ref, q_ref, k_ref, v_ref, qi_ref, kw_ref, *, tiles_per_seq):
    i = pl.program_id(0)
    cw = convw_ref.shape[1]
    aw = q_ref.shape[1]
    d = x_ref.shape[1]
    bt = x_ref.shape[0]
    sh1 = mod_ref[0, 0:1, :]
    sc1 = mod_ref[0, 1:2, :]

    def norm_mod(xv):
        ms = jnp.mean(xv * xv, axis=-1, keepdims=True)
        y = xv * lax.rsqrt(ms + EPS) * n1_ref[...]
        return (y * (1.0 + sc1) + sh1).astype(BF16)

    h = norm_mod(x_ref[...])
    hh = norm_mod(xh_ref[...])

    pc = jnp.dot(h, wconv_ref[...], preferred_element_type=F32)
    cb, cc, cx = pc[:, :cw], pc[:, cw:2 * cw], pc[:, 2 * cw:]
    u = cc * cx
    ph = jnp.dot(hh, wconv_ref[:, cw:], preferred_element_type=F32)
    uh = ph[:, :cw] * ph[:, cw:]
    uh = jnp.where(i % tiles_per_seq == 0, 0.0, uh)
    row = lax.broadcasted_iota(I32, (bt, cw), 0)
    prev1 = uh[SUBLANES - 1:SUBLANES, :]
    prev2 = uh[SUBLANES - 2:SUBLANES - 1, :]
    u1 = jnp.where(row == 0, prev1, pltpu.roll(u, 1, 0))
    u2 = pltpu.roll(u, 2, 0)
    u2 = jnp.where(row == 0, prev2, jnp.where(row == 1, prev1, u2))
    conv = convw_ref[0:1, :] * u2 + convw_ref[1:2, :] * u1 + convw_ref[2:3, :] * u
    yc = jnp.dot((cb * conv).astype(BF16), wco_ref[...], preferred_element_type=F32)

    pg = jnp.dot(h, wg_ref[...], preferred_element_type=F32)
    gyc_ref[...] = (jax.nn.sigmoid(pg[:, :d]) * yc).astype(gyc_ref.dtype)
    sga_ref[...] = jax.nn.sigmoid(pg[:, d:]).astype(sga_ref.dtype)

    pq = jnp.dot(h, wqkv_ref[...], preferred_element_type=F32)
    gmat = gmat_ref[...]

    def head_norm(t, w_row):
        hi, lo = _split_bf16(t * t)
        ss = (jnp.dot(hi, gmat, preferred_element_type=F32)
              + jnp.dot(lo, gmat, preferred_element_type=F32))
        return t * lax.rsqrt(ss * (1.0 / HEAD_DIM) + EPS) * w_row

    qn = head_norm(pq[:, :aw], qn_ref[...])
    qs = qn * (HEAD_DIM ** -0.5 * LOG2E)
    k_ref[...] = head_norm(pq[:, aw:2 * aw], kn_ref[...]).astype(BF16)
    vv = pq[:, 2 * aw:]
    qi = jnp.dot(h, wqi_ref[...], preferred_element_type=F32)
    tw = q_ref.shape[2]
    for s in range(bt // tw):
        rows = slice(s * tw, (s + 1) * tw)
        q_ref[s] = qs[rows].T.astype(BF16)
        v_ref[s] = vv[rows].T.astype(BF16)
        qi_ref[s] = qi[rows].T.astype(BF16)
    kw_ref[...] = jnp.dot(h, wkw_ref[...], preferred_element_type=F32)


def _inproj_call(xf, mod3, n1, wconv, wqkv, wqi, wkw, wg, convw, wco, qn_t, kn_t, gmat, seq):
    t, d = xf.shape
    bt = BT_PROJ
    tw = BK
    tiles_per_seq = seq // bt
    cw = convw.shape[1]
    aw = qn_t.shape[1]
    const = lambda i: (0, 0)
    tile = lambda i: (i, 0)
    tile_t = lambda i: (i, 0, 0)
    resident = lambda a: pl.BlockSpec(a.shape, const, pipeline_mode=pl.Buffered(1))
    halo = lambda i: (jnp.maximum(i * (bt // SUBLANES) - 1, 0), 0)
    kern = functools.partial(_inproj_kernel, tiles_per_seq=tiles_per_seq)
    return pl.pallas_call(
        kern,
        out_shape=[
            jax.ShapeDtypeStruct((t, d), BF16),
            jax.ShapeDtypeStruct((t, d), BF16),
            jax.ShapeDtypeStruct((t // tw, aw, tw), BF16),
            jax.ShapeDtypeStruct((t, aw), BF16),
            jax.ShapeDtypeStruct((t // tw, aw, tw), BF16),
            jax.ShapeDtypeStruct((t // tw, aw, tw), BF16),
            jax.ShapeDtypeStruct((t, LANES), F32),
        ],
        grid=(t // bt,),
        in_specs=[
            pl.BlockSpec((bt, d), tile),
            pl.BlockSpec((SUBLANES, d), halo),
            pl.BlockSpec((1, N_MOD, d), lambda i: (i // tiles_per_seq, 0, 0)),
            pl.BlockSpec((1, d), const),
            resident(wconv),
            resident(wqkv),
            resident(wqi),
            resident(wkw),
            resident(wg),
            pl.BlockSpec(convw.shape, const),
            resident(wco),
            pl.BlockSpec((1, aw), const),
            pl.BlockSpec((1, aw), const),
            resident(gmat),
        ],
        out_specs=[
            pl.BlockSpec((bt, d), tile),
            pl.BlockSpec((bt, d), tile),
            pl.BlockSpec((bt // tw, aw, tw), tile_t),
            pl.BlockSpec((bt, aw), tile),
            pl.BlockSpec((bt // tw, aw, tw), tile_t),
            pl.BlockSpec((bt // tw, aw, tw), tile_t),
            pl.BlockSpec((bt, LANES), tile),
        ],
        compiler_params=pltpu.CompilerParams(
            dimension_semantics=("arbitrary",), vmem_limit_bytes=VMEM_LIMIT),
        name="inproj",
    )(xf, xf, mod3, n1, wconv, wqkv, wqi, wkw, wg, convw, wco, qn_t, kn_t, gmat)


def _tree_sum(parts):
    parts = list(parts)
    while len(parts) > 1:
        nxt = [parts[a] + parts[a + 1] for a in range(0, len(parts) - 1, 2)]
        if len(parts) % 2:
            nxt.append(parts[-1])
        parts = nxt
    return parts[0]


def _dsa_kernel(qt_ref, qit_ref, kwq_ref, k_ref, vt_ref, kwk_ref, tri_ref, o_ref,
                keys_ref, bias_ref, qz_ref, qiz_ref, s_ref, m_ref, l_ref, acc_ref, *, n_sel):
    i = pl.program_id(1)
    n_qt, _, tw = qt_ref.shape
    bq = n_qt * tw
    bk = BK
    n_tiles = (i * bq) // bk + bq // bk
    ngrp = bk // SUBLANES

    zpad = jnp.zeros((LANES - HEAD_DIM, bq), BF16)
    for h in range(N_HEADS):
        qh = jnp.concatenate([qt_ref[s, h * HEAD_DIM:(h + 1) * HEAD_DIM, :] for s in range(n_qt)], axis=1)
        qih = jnp.concatenate([qit_ref[s, h * IDX_DIM:(h + 1) * IDX_DIM, :] for s in range(n_qt)], axis=1)
        qz_ref[h] = jnp.concatenate([qh, zpad] if h % 2 == 0 else [zpad, qh], axis=0)
        qiz_ref[h] = jnp.concatenate([qih, zpad], axis=0)
    wt = kwq_ref[...].T

    q_pos = i * bq + lax.broadcasted_iota(I32, (bk, bq), 1)

    def score_tile(j, carry):
        kw = kwk_ref[pl.ds(pl.multiple_of(j * bk, bk), bk), :].astype(BF16)
        score = jnp.zeros((bk, bq), F32)
        for h in range(IDX_HEADS):
            logits = jnp.dot(kw, qiz_ref[h], preferred_element_type=F32)
            score = score + wt[IDX_DIM + h:IDX_DIM + h + 1, :] * jnp.maximum(logits, 0.0)
        bits = pltpu.bitcast(score, I32)
        key = jnp.where(bits < 0, bits ^ jnp.int32(0x7FFFFFFF), bits)
        key = jnp.where(bits == jnp.int32(INT_MIN), 0, key)
        key_pos = j * bk + lax.broadcasted_iota(I32, (bk, bq), 0)
        keys_ref[j] = jnp.where(key_pos <= q_pos, key, jnp.int32(INT_MIN))
        return carry

    lax.fori_loop(0, n_tiles, score_tile, 0)

    def count_ge(cand, strict):
        cand_b = jnp.broadcast_to(cand, (SUBLANES, bq))

        def body(j, cnt):
            hits = []
            for g in range(ngrp):
                ks = keys_ref[j, g * SUBLANES:(g + 1) * SUBLANES, :]
                hit = (ks > cand_b) if strict else (ks >= cand_b)
                hits.append(jnp.where(hit, 1, 0))
            return cnt + _tree_sum(hits)

        cnt = lax.fori_loop(0, n_tiles, body, jnp.zeros((SUBLANES, bq), I32))
        return jnp.sum(cnt, axis=0, keepdims=True)

    def bit_step(b, t_u):
        cand_u = t_u | lax.shift_left(jnp.int32(1), 31 - b)
        total = count_ge(cand_u ^ jnp.int32(INT_MIN), False)
        return jnp.where(total >= n_sel, cand_u, t_u)

    t_u = lax.fori_loop(0, 32, bit_step, jnp.zeros((1, bq), I32))
    thr = t_u ^ jnp.int32(INT_MIN)
    n_gt = count_ge(thr, True)
    need = jnp.where(thr == jnp.int32(INT_MIN), 0, n_sel - n_gt).astype(F32)

    tri = tri_ref[...]

    def bias_tile(j, carry):
        kt = keys_ref[j]
        eq = kt == thr
        eq_f = jnp.where(eq, 1.0, 0.0)
        prefix = jnp.dot(tri, eq_f.astype(BF16), preferred_element_type=F32) + carry
        sel = (kt > thr) | (eq & (prefix <= need))
        bias_ref[j] = jnp.where(sel, 0.0, NEG_BIG)
        return prefix[bk - 1:bk, :]

    lax.fori_loop(0, n_tiles, bias_tile, jnp.zeros((1, bq), F32))

    m_ref[...] = jnp.full(m_ref.shape, NEG_BIG, F32)
    l_ref[...] = jnp.zeros(l_ref.shape, F32)
    acc_ref[...] = jnp.zeros(acc_ref.shape, F32)

    def attn_tile(j, carry):
        kt = k_ref[pl.ds(pl.multiple_of(j * bk, bk), bk), :]
        bias = bias_ref[j]
        alphas = []
        for h in range(N_HEADS):
            pair = slice((h // 2) * LANES, (h // 2 + 1) * LANES)
            s = jnp.dot(kt[:, pair], qz_ref[h], preferred_element_type=F32) + bias
            s_ref[h] = s
            m_prev = m_ref[h]
            m_new = jnp.maximum(m_prev, jnp.broadcast_to(jnp.max(s, axis=0, keepdims=True), (SUBLANES, bq)))
            m_ref[h] = m_new
            alphas.append(jnp.exp2(m_prev - m_new))
        for h in range(N_HEADS):
            p = jnp.exp2(s_ref[h] - m_ref[h, 0:1, :])
            l_ref[h] = alphas[h] * l_ref[h] + _tree_sum([p[g * SUBLANES:(g + 1) * SUBLANES] for g in range(ngrp)])
            pv = jnp.dot(vt_ref[j, h * HEAD_DIM:(h + 1) * HEAD_DIM, :], p.astype(BF16),
                         preferred_element_type=F32)
            acc_ref[h] = alphas[h][0:1] * acc_ref[h] + pv
        return carry

    lax.fori_loop(0, n_tiles, attn_tile, 0)

    outs = [acc_ref[h] / jnp.sum(l_ref[h], axis=0, keepdims=True) for h in range(N_HEADS)]
    o_ref[...] = jnp.concatenate(outs, axis=0).T.astype(o_ref.dtype)


def _dsa_call(qt, qit, kw, k, vt, tri, batch, seq):
    nt, aw, tw = qt.shape
    bq = BQ
    assert tw == BK and bq % tw == 0 and seq % bq == 0
    nq = seq // bq
    n_sel = min(TOPK_ATTN, seq // 4)
    qtile = lambda b, i: (b * nq + i, 0)
    qtile_t = lambda b, i: (b * nq + i, 0, 0)
    whole = lambda b, i: (b, 0)
    kern = functools.partial(_dsa_kernel, n_sel=n_sel)
    return pl.pallas_call(
        kern,
        out_shape=jax.ShapeDtypeStruct((batch * seq, aw), BF16),
        grid=(batch, nq),
        in_specs=[
            pl.BlockSpec((bq // tw, aw, tw), qtile_t),
            pl.BlockSpec((bq // tw, aw, tw), qtile_t),
            pl.BlockSpec((bq, LANES), qtile),
            pl.BlockSpec((seq, aw), whole),
            pl.BlockSpec((seq // BK, aw, BK), lambda b, i: (b, 0, 0)),
            pl.BlockSpec((seq, LANES), whole),
            pl.BlockSpec(tri.shape, lambda b, i: (0, 0)),
        ],
        out_specs=pl.BlockSpec((bq, aw), qtile),
        scratch_shapes=[
            pltpu.VMEM((seq // BK, BK, bq), I32),
            pltpu.VMEM((seq // BK, BK, bq), F32),
            pltpu.VMEM((N_HEADS, LANES, bq), BF16),
            pltpu.VMEM((IDX_HEADS, LANES, bq), BF16),
            pltpu.VMEM((N_HEADS, BK, bq), F32),
            pltpu.VMEM((N_HEADS, SUBLANES, bq), F32),
            pltpu.VMEM((N_HEADS, SUBLANES, bq), F32),
            pltpu.VMEM((N_HEADS, HEAD_DIM, bq), F32),
        ],
        compiler_params=pltpu.CompilerParams(
            dimension_semantics=("arbitrary", "arbitrary"), vmem_limit_bytes=VMEM_LIMIT),
        name="dsa_attention",
    )(qt, qit, kw, k, vt, kw, tri)


def _post_kernel(x_ref, attn_ref, gyc_ref, sga_ref, mod_ref, wao_ref, wo_ref, o_ref):
    g1 = mod_ref[0, 2:3, :]
    y_attn = jnp.dot(attn_ref[...], wao_ref[...], preferred_element_type=F32)
    mix = gyc_ref[...] + sga_ref[...] * y_attn
    delta = jnp.dot(mix.astype(BF16), wo_ref[...], preferred_element_type=F32)
    o_ref[...] = x_ref[...] + g1 * delta


def _post_call(xf, attn, gyc, sga, mod3, wao, wo, seq):
    t, d = xf.shape
    bt = BT_PROJ
    tiles_per_seq = seq // bt
    aw = attn.shape[1]
    tile = lambda i: (i, 0)
    const = lambda i: (0, 0)
    return pl.pallas_call(
        _post_kernel,
        out_shape=jax.ShapeDtypeStruct((t, d), F32),
        grid=(t // bt,),
        in_specs=[
            pl.BlockSpec((bt, d), tile),
            pl.BlockSpec((bt, aw), tile),
            pl.BlockSpec((bt, d), tile),
            pl.BlockSpec((bt, d), tile),
            pl.BlockSpec((1, N_MOD, d), lambda i: (i // tiles_per_seq, 0, 0)),
            pl.BlockSpec(wao.shape, const),
            pl.BlockSpec(wo.shape, const),
        ],
        out_specs=pl.BlockSpec((bt, d), tile),
        compiler_params=pltpu.CompilerParams(
            dimension_semantics=("arbitrary",), vmem_limit_bytes=VMEM_LIMIT),
        name="post_mix",
    )(xf, attn, gyc, sga, mod3, wao, wo)


def _col_max(a):
    return jnp.max(a, axis=0, keepdims=True)


def _batcher_pairs(n):
    pairs = []
    p = 1
    while p < n:
        k = p
        while k >= 1:
            for j in range(k % p, n - k, 2 * k):
                for i in range(min(k, n - j - k)):
                    if (i + j) // (2 * p) == (i + j + k) // (2 * p):
                        pairs.append((i + j, i + j + k))
            k //= 2
        p *= 2
    return pairs


_SORT16 = _batcher_pairs(PEER_TOPK)
_BITONIC16 = [(k, k + d) for d in (8, 4, 2, 1) for k in range(PEER_TOPK) if not k & d]


def _sort_desc(v, pairs):
    v = list(v)
    for i, j in pairs:
        v[i], v[j] = jnp.maximum(v[i], v[j]), jnp.minimum(v[i], v[j])
    return v


def _peer_kernel(x_ref, mod_ref, n2_ref, wqt_ref, sk_ref, u_ref, vt_ref, o_ref,
                 h2t_ref, s1_ref, s2_ref, ap_ref, bp_ref, nsel_ref, rk_ref, w_ref, acc_ref):
    e = pl.program_id(1)
    n_e = pl.num_programs(1)
    tt = x_ref.shape[0]
    nk = PEER_N_KEYS
    n_i = u_ref.shape[0] // nk
    nslab = tt // LANES

    @pl.when(e == 0)
    def _route():
        xv = x_ref[...]
        sh2 = mod_ref[0, 3:4, :]
        sc2 = mod_ref[0, 4:5, :]
        ms = jnp.mean(xv * xv, axis=-1, keepdims=True)
        h2t = ((xv * lax.rsqrt(ms + EPS) * n2_ref[...]) * (1.0 + sc2) + sh2).T.astype(BF16)
        h2t_ref[...] = h2t
        qt = jnp.dot(wqt_ref[...], h2t, preferred_element_type=F32).astype(BF16)
        for h in range(PEER_HEADS):
            for p, dst in ((0, s1_ref), (1, s2_ref)):
                r0 = (h * 2 + p) * PEER_HALF
                dst[h] = jnp.dot(sk_ref[h, p], qt[r0:r0 + PEER_HALF, :],
                                 preferred_element_type=F32)

        neg_inf = jnp.float32(-jnp.inf)
        sub_id = lax.broadcasted_iota(I32, (SUBLANES, LANES), 0)

        def top16(s):
            v = _sort_desc([s[g * SUBLANES:(g + 1) * SUBLANES] for g in range(nk // SUBLANES)], _SORT16)
            for shift in (4, 2, 1):
                v = [jnp.maximum(v[r], pltpu.roll(v[PEER_TOPK - 1 - r], shift, 0)) for r in range(PEER_TOPK)]
                v = _sort_desc(v, _BITONIC16)
            return v

        def stack8(vals):
            out = vals[0]
            for r in range(1, SUBLANES):
                out = jnp.where(sub_id == r, vals[r], out)
            return out

        def route_head(h, carry):
            for c in range(nslab):
                ls = slice(c * LANES, (c + 1) * LANES)
                s1 = s1_ref[h, :, ls]
                s2 = s2_ref[h, :, ls]
                a = top16(s1)
                b = top16(s2)
                b_lo, b_hi = stack8(b[:SUBLANES]), stack8(b[SUBLANES:])
                cands = [a[0] + b_lo, a[0] + b_hi] + [a[r] + b_lo for r in range(1, PEER_TOPK)]
                work, t = cands, []
                for _ in range(PEER_TOPK):
                    m = _col_max(functools.reduce(jnp.maximum, work))
                    t.append(m)
                    work = [jnp.where(w == m, neg_inf, w) for w in work]
                thr = t[PEER_TOPK - 1]
                z = functools.reduce(jnp.add, [jnp.exp(tk - t[0]) for tk in t])
                hits = [jnp.where(cd >= thr, 1.0, 0.0) for cd in cands]
                n_sel = [jnp.sum(hits[0] + hits[1], axis=0, keepdims=True)]
                n_sel += [jnp.sum(hits[r + 1], axis=0, keepdims=True) for r in range(1, PEER_TOPK)]
                n_b = [jnp.broadcast_to(n, (SUBLANES, LANES)) for n in n_sel]
                cnts, ranks = [], []
                for g in range(nk // SUBLANES):
                    rows = slice(g * 8, (g + 1) * 8)
                    s1g, s2g = s1[rows], s2[rows]
                    cnt = jnp.zeros((SUBLANES, LANES), F32)
                    rank = jnp.full((SUBLANES, LANES), PEER_TOPK + 1.0, F32)
                    for r in range(PEER_TOPK):
                        cnt = jnp.where(s1g == a[r], n_b[r], cnt)
                        rank = jnp.where(s2g == b[r], r + 1.0, rank)
                    cnts.append(cnt)
                    ranks.append(rank)
                nsel_ref[h, :, ls] = jnp.concatenate(cnts, axis=0)
                rk_ref[h, :, ls] = jnp.concatenate(ranks, axis=0).astype(BF16)
                ap_ref[h, :, ls] = jnp.exp(s1 - a[0][0:1])
                bp_ref[h, :, ls] = (jnp.exp(s2 - b[0][0:1]) * (0.5 / z)).astype(BF16)
            return carry

        lax.fori_loop(0, PEER_HEADS, route_head, 0)
        acc_ref[...] = jnp.zeros(acc_ref.shape, F32)

    irow0 = pl.multiple_of(e * n_i, SUBLANES)
    pk = BF16_ROWS
    n_blk = [[nsel_ref[h, pl.ds(irow0, n_i), c * LANES:(c + 1) * LANES] for h in range(PEER_HEADS)]
             for c in range(nslab)]
    ap_blk = [[ap_ref[h, pl.ds(irow0, n_i), c * LANES:(c + 1) * LANES] for h in range(PEER_HEADS)]
              for c in range(nslab)]
    h2t = h2t_ref[...]
    was = []
    for i0 in range(0, n_i, PEER_IBLK):
        for c in range(nslab):
            ls = slice(c * LANES, (c + 1) * LANES)
            for ii in range(i0, i0 + PEER_IBLK):
                accs = [None] * (nk // pk)
                for h in range(PEER_HEADS):
                    n16 = jnp.broadcast_to(n_blk[c][h][ii:ii + 1], (pk, LANES)).astype(BF16)
                    a16 = jnp.broadcast_to(ap_blk[c][h][ii:ii + 1], (pk, LANES)).astype(BF16)
                    for g in range(nk // pk):
                        rows = slice(g * pk, (g + 1) * pk)
                        hit = rk_ref[h, rows, ls] <= n16
                        term = jnp.where(hit, bp_ref[h, rows, ls], jnp.zeros((), BF16)) * a16
                        accs[g] = term if h == 0 else accs[g] + term
                for g in range(nk // pk):
                    w_ref[ii * nk + g * pk:ii * nk + (g + 1) * pk, ls] = accs[g]
        rows = slice(i0 * nk, (i0 + PEER_IBLK) * nk)
        act = jnp.dot(u_ref[rows, :], h2t, preferred_element_type=F32)
        gelu2 = act * (1.0 + lax.erf(act * (2.0 ** -0.5)))
        was.append(w_ref[rows, :] * gelu2.astype(BF16))
    acc_ref[...] += jnp.dot(vt_ref[0], jnp.concatenate(was, axis=0), preferred_element_type=F32)

    @pl.when(e == n_e - 1)
    def _finish():
        g2 = mod_ref[0, 5:6, :]
        o_ref[...] = x_ref[...] + g2 * acc_ref[...].T


def _peer_call(x1, mod3, n2, wqt, sk, u, vt, seq):
    t, d = x1.shape
    tt = TT_PEER
    ec = EC_PEER
    n_chunks = u.shape[0] // ec
    assert ec % (SUBLANES * PEER_N_KEYS) == 0
    tiles_per_seq = seq // tt
    tile = lambda i, e: (i, 0)
    const2 = lambda i, e: (0, 0)
    return pl.pallas_call(
        _peer_kernel,
        out_shape=jax.ShapeDtypeStruct((t, d), F32),
        grid=(t // tt, n_chunks),
        in_specs=[
            pl.BlockSpec((tt, d), tile),
            pl.BlockSpec((1, N_MOD, d), lambda i, e: (i // tiles_per_seq, 0, 0)),
            pl.BlockSpec((1, d), const2),
            pl.BlockSpec(wqt.shape, const2),
            pl.BlockSpec(sk.shape, lambda i, e: (0, 0, 0, 0)),
            pl.BlockSpec((ec, d), lambda i, e: (e, 0)),
            pl.BlockSpec((1, d, ec), lambda i, e: (e, 0, 0)),
        ],
        out_specs=pl.BlockSpec((tt, d), tile),
        scratch_shapes=[
            pltpu.VMEM((d, tt), BF16),
            pltpu.VMEM((PEER_HEADS, PEER_N_KEYS, tt), F32),
            pltpu.VMEM((PEER_HEADS, PEER_N_KEYS, tt), F32),
            pltpu.VMEM((PEER_HEADS, PEER_N_KEYS, tt), F32),
            pltpu.VMEM((PEER_HEADS, PEER_N_KEYS, tt), BF16),
            pltpu.VMEM((PEER_HEADS, PEER_N_KEYS, tt), F32),
            pltpu.VMEM((PEER_HEADS, PEER_N_KEYS, tt), BF16),
            pltpu.VMEM((ec, tt), BF16),
            pltpu.VMEM((d, tt), F32),
        ],
        compiler_params=pltpu.CompilerParams(
            dimension_semantics=("arbitrary", "arbitrary"), vmem_limit_bytes=VMEM_LIMIT),
        name="peer",
    )(x1, mod3, n2, wqt, sk, u, vt)


def kernel(x, c, w_ada, b_ada, norm1_w, w_in, conv_w, w_conv_out, q_norm_w, k_norm_w,
           w_attn_out, w_o, norm2_w, w_peer_q, peer_sub_keys, peer_u, peer_v):
    b, s, d = x.shape
    depth = w_ada.shape[0]
    cw = conv_w.shape[2]
    aw = N_HEADS * HEAD_DIM
    iw = IDX_HEADS * IDX_DIM
    assert s % BQ == 0 and s % TT_PEER == 0 and s % BT_PROJ == 0
    assert peer_u.shape[1] == PEER_N_KEYS * PEER_N_KEYS

    gidx = jnp.arange(aw) // HEAD_DIM
    gmat = (gidx[:, None] == gidx[None, :]).astype(BF16)
    tri = (jnp.arange(BK)[:, None] >= jnp.arange(BK)[None, :]).astype(BF16)

    xf = x.reshape(b * s, d)
    for layer in range(depth):
        w = w_in[layer]
        o = 0
        wconv = w[:, o:o + 3 * cw].astype(BF16); o += 3 * cw
        wqkv = w[:, o:o + 3 * aw].astype(BF16); o += 3 * aw
        wqi = w[:, o:o + iw].astype(BF16); o += iw
        wkw = jnp.pad(w[:, o:o + IDX_DIM + IDX_HEADS], ((0, 0), (0, LANES - IDX_DIM - IDX_HEADS))).astype(BF16)
        o += IDX_DIM + IDX_HEADS
        wg = w[:, o:o + 2 * d].astype(BF16)
        qn_t = jnp.tile(q_norm_w[layer], N_HEADS).reshape(1, aw)
        kn_t = jnp.tile(k_norm_w[layer], N_HEADS).reshape(1, aw)

        mod3 = _mod_call(c, w_ada[layer], b_ada[layer]).reshape(b, N_MOD, d)
        gyc, sga, qt, k, vt, qit, kw = _inproj_call(
            xf, mod3, norm1_w[layer].reshape(1, d), wconv, wqkv, wqi, wkw, wg, conv_w[layer],
            w_conv_out[layer].astype(BF16), qn_t, kn_t, gmat, s)
        attn = _dsa_call(qt, qit, kw, k, vt, tri, b, s)
        x1 = _post_call(xf, attn, gyc, sga, mod3, w_attn_out[layer].astype(BF16),
                        w_o[layer].astype(BF16), s)
        xf = _peer_call(
            x1, mod3, norm2_w[layer].reshape(1, d), w_peer_q[layer].T.astype(BF16),
            peer_sub_keys[layer].astype(BF16), peer_u[layer].astype(BF16),
            peer_v[layer].astype(BF16).reshape(-1, EC_PEER, d).transpose(0, 2, 1), s)
    return xf.reshape(b, s, d)
```

```python
import functools

import jax
import jax.numpy as jnp
from jax import lax
from jax.experimental import pallas as pl
from jax.experimental.pallas import tpu as pltpu

F32 = jnp.float32
BF16 = jnp.bfloat16
I32 = jnp.int32

N_HEADS = 8
HEAD_DIM = 64
IDX_HEADS = 8
IDX_DIM = 64
TOPK_ATTN = 256
PEER_HEADS = 8
PEER_N_KEYS = 128
PEER_HALF = 64
PEER_TOPK = 16
N_MOD = 6
EPS = 1e-6

LANES = 128
SUBLANES = 8
BF16_ROWS = 16
INT_MIN = -(2 ** 31)
NEG_BIG = -1e30
LOG2E = 1.4426950408889634
VMEM_LIMIT = 56 * 1024 * 1024

BT_PROJ = 512
BQ = 512
BK = 256
TT_PEER = 512
EC_PEER = 2048
PEER_IBLK = 1


def _split_bf16(a):
    hi = a.astype(BF16)
    lo = (a - hi.astype(F32)).astype(BF16)
    return hi, lo


def _mod_kernel(c_ref, w_ref, b_ref, o_ref):
    c = c_ref[...]
    s = c * jax.nn.sigmoid(c)
    s_hi, s_lo = _split_bf16(s)
    w_hi, w_lo = _split_bf16(w_ref[...])
    acc = jnp.dot(s_hi, w_hi, preferred_element_type=F32)
    acc += jnp.dot(s_hi, w_lo, preferred_element_type=F32)
    acc += jnp.dot(s_lo, w_hi, preferred_element_type=F32)
    o_ref[...] = acc + b_ref[...]


def _mod_call(c, w_ada, b_ada):
    b, d = c.shape
    n = w_ada.shape[1]
    bn = d
    return pl.pallas_call(
        _mod_kernel,
        out_shape=jax.ShapeDtypeStruct((b, n), F32),
        grid=(n // bn,),
        in_specs=[
            pl.BlockSpec((b, d), lambda j: (0, 0)),
            pl.BlockSpec((d, bn), lambda j: (0, j)),
            pl.BlockSpec((1, bn), lambda j: (0, j)),
        ],
        out_specs=pl.BlockSpec((b, bn), lambda j: (0, j)),
        compiler_params=pltpu.CompilerParams(
            dimension_semantics=("arbitrary",), vmem_limit_bytes=VMEM_LIMIT),
        name="adaln_mod",
    )(c, w_ada, b_ada.reshape(1, n))


def _inproj_kernel(x_ref, xh_ref, mod_ref, n1_ref, wconv_ref, wqkv_ref, wqi_ref, wkw_ref,
                   wg_ref, convw_ref, wco_ref, qn_ref, kn_ref, gmat_ref,
                   gyc_ref, sga_ref, q_ref, k_ref, v_ref, qi_ref, kw_ref, *, tiles_per_seq):
    i = pl.program_id(0)
    cw = convw_ref.shape[1]
    aw = q_ref.shape[1]
    d = x_ref.shape[1]
    bt = x_ref.shape[0]
    sh1 = mod_ref[0, 0:1, :]
    sc1 = mod_ref[0, 1:2, :]

    def norm_mod(xv):
        ms = jnp.mean(xv * xv, axis=-1, keepdims=True)
        y = xv * lax.rsqrt(ms + EPS) * n1_ref[...]
        return (y * (1.0 + sc1) + sh1).astype(BF16)

    h = norm_mod(x_ref[...])
    hh = norm_mod(xh_ref[...])

    pc = jnp.dot(h, wconv_ref[...], preferred_element_type=F32)
    cb, cc, cx = pc[:, :cw], pc[:, cw:2 * cw], pc[:, 2 * cw:]
    u = cc * cx
    ph = jnp.dot(hh, wconv_ref[:, cw:], preferred_element_type=F32)
    uh = ph[:, :cw] * ph[:, cw:]
    uh = jnp.where(i % tiles_per_seq == 0, 0.0, uh)
    row = lax.broadcasted_iota(I32, (bt, cw), 0)
    prev1 = uh[SUBLANES - 1:SUBLANES, :]
    prev2 = uh[SUBLANES - 2:SUBLANES - 1, :]
    u1 = jnp.where(row == 0, prev1, pltpu.roll(u, 1, 0))
    u2 = pltpu.roll(u, 2, 0)
    u2 = jnp.where(row == 0, prev2, jnp.where(row == 1, prev1, u2))
    conv = convw_ref[0:1, :] * u2 + convw_ref[1:2, :] * u1 + convw_ref[2:3, :] * u
    yc = jnp.dot((cb * conv).astype(BF16), wco_ref[...], preferred_element_type=F32)

    pg = jnp.dot(h, wg_ref[...], preferred_element_type=F32)
    gyc_ref[...] = (jax.nn.sigmoid(pg[:, :d]) * yc).astype(gyc_ref.dtype)
    sga_ref[...] = jax.nn.sigmoid(pg[:, d:]).astype(sga_ref.dtype)

    pq = jnp.dot(h, wqkv_ref[...], preferred_element_type=F32)
    gmat = gmat_ref[...]

    def head_norm(t, w_row):
        hi, lo = _split_bf16(t * t)
        ss = (jnp.dot(hi, gmat, preferred_element_type=F32)
              + jnp.dot(lo, gmat, preferred_element_type=F32))
        return t * lax.rsqrt(ss * (1.0 / HEAD_DIM) + EPS) * w_row

    qn = head_norm(pq[:, :aw], qn_ref[...])
    qs = qn * (HEAD_DIM ** -0.5 * LOG2E)
    k_ref[...] = head_norm(pq[:, aw:2 * aw], kn_ref[...]).astype(BF16)
    vv = pq[:, 2 * aw:]
    qi = jnp.dot(h, wqi_ref[...], preferred_element_type=F32)
    tw = q_ref.shape[2]
    for s in range(bt // tw):
        rows = slice(s * tw, (s + 1) * tw)
        q_ref[s] = qs[rows].T.astype(BF16)
        v_ref[s] = vv[rows].T.astype(BF16)
        qi_ref[s] = qi[rows].T.astype(BF16)
    kw_ref[...] = jnp.dot(h, wkw_ref[...], preferred_element_type=F32)


def _inproj_call(xf, mod3, n1, wconv, wqkv, wqi, wkw, wg, convw, wco, qn_t, kn_t, gmat, seq):
    t, d = xf.shape
    bt = BT_PROJ
    tw = BK
    tiles_per_seq = seq // bt
    cw = convw.shape[1]
    aw = qn_t.shape[1]
    const = lambda i: (0, 0)
    tile = lambda i: (i, 0)
    tile_t = lambda i: (i, 0, 0)
    resident = lambda a: pl.BlockSpec(a.shape, const, pipeline_mode=pl.Buffered(1))
    halo = lambda i: (jnp.maximum(i * (bt // SUBLANES) - 1, 0), 0)
    kern = functools.partial(_inproj_kernel, tiles_per_seq=tiles_per_seq)
    return pl.pallas_call(
        kern,
        out_shape=[
            jax.ShapeDtypeStruct((t, d), BF16),
            jax.ShapeDtypeStruct((t, d), BF16),
            jax.ShapeDtypeStruct((t // tw, aw, tw), BF16),
            jax.ShapeDtypeStruct((t, aw), BF16),
            jax.ShapeDtypeStruct((t // tw, aw, tw), BF16),
            jax.ShapeDtypeStruct((t // tw, aw, tw), BF16),
            jax.ShapeDtypeStruct((t, LANES), F32),
        ],
        grid=(t // bt,),
        in_specs=[
            pl.BlockSpec((bt, d), tile),
            pl.BlockSpec((SUBLANES, d), halo),
            pl.BlockSpec((1, N_MOD, d), lambda i: (i // tiles_per_seq, 0, 0)),
            pl.BlockSpec((1, d), const),
            resident(wconv),
            resident(wqkv),
            resident(wqi),
            resident(wkw),
            resident(wg),
            pl.BlockSpec(convw.shape, const),
            resident(wco),
            pl.BlockSpec((1, aw), const),
            pl.BlockSpec((1, aw), const),
            resident(gmat),
        ],
        out_specs=[
            pl.BlockSpec((bt, d), tile),
            pl.BlockSpec((bt, d), tile),
            pl.BlockSpec((bt // tw, aw, tw), tile_t),
            pl.BlockSpec((bt, aw), tile),
            pl.BlockSpec((bt // tw, aw, tw), tile_t),
            pl.BlockSpec((bt // tw, aw, tw), tile_t),
            pl.BlockSpec((bt, LANES), tile),
        ],
        compiler_params=pltpu.CompilerParams(
            dimension_semantics=("arbitrary",), vmem_limit_bytes=VMEM_LIMIT),
        name="inproj",
    )(xf, xf, mod3, n1, wconv, wqkv, wqi, wkw, wg, convw, wco, qn_t, kn_t, gmat)


def _tree_sum(parts):
    parts = list(parts)
    while len(parts) > 1:
        nxt = [parts[a] + parts[a + 1] for a in range(0, len(parts) - 1, 2)]
        if len(parts) % 2:
            nxt.append(parts[-1])
        parts = nxt
    return parts[0]


def _dsa_kernel(qt_ref, qit_ref, kwq_ref, k_ref, vt_ref, kwk_ref, tri_ref, o_ref,
                keys_ref, bias_ref, qz_ref, qiz_ref, s_ref, m_ref, l_ref, acc_ref, *, n_sel):
    i = pl.program_id(1)
    n_qt, _, tw = qt_ref.shape
    bq = n_qt * tw
    bk = BK
    n_tiles = (i * bq) // bk + bq // bk
    ngrp = bk // SUBLANES

    zpad = jnp.zeros((LANES - HEAD_DIM, bq), BF16)
    for h in range(N_HEADS):
        qh = jnp.concatenate([qt_ref[s, h * HEAD_DIM:(h + 1) * HEAD_DIM, :] for s in range(n_qt)], axis=1)
        qih = jnp.concatenate([qit_ref[s, h * IDX_DIM:(h + 1) * IDX_DIM, :] for s in range(n_qt)], axis=1)
        qz_ref[h] = jnp.concatenate([qh, zpad] if h % 2 == 0 else [zpad, qh], axis=0)
        qiz_ref[h] = jnp.concatenate([qih, zpad], axis=0)
    wt = kwq_ref[...].T

    q_pos = i * bq + lax.broadcasted_iota(I32, (bk, bq), 1)

    def score_tile(j, carry):
        kw = kwk_ref[pl.ds(pl.multiple_of(j * bk, bk), bk), :].astype(BF16)
        score = jnp.zeros((bk, bq), F32)
        for h in range(IDX_HEADS):
            logits = jnp.dot(kw, qiz_ref[h], preferred_element_type=F32)
            score = score + wt[IDX_DIM + h:IDX_DIM + h + 1, :] * jnp.maximum(logits, 0.0)
        bits = pltpu.bitcast(score, I32)
        key = jnp.where(bits < 0, bits ^ jnp.int32(0x7FFFFFFF), bits)
        key = jnp.where(bits == jnp.int32(INT_MIN), 0, key)
        key_pos = j * bk + lax.broadcasted_iota(I32, (bk, bq), 0)
        keys_ref[j] = jnp.where(key_pos <= q_pos, key, jnp.int32(INT_MIN))
        return carry

    lax.fori_loop(0, n_tiles, score_tile, 0)

    def count_ge(cand, strict):
        cand_b = jnp.broadcast_to(cand, (SUBLANES, bq))

        def body(j, cnt):
            hits = []
            for g in range(ngrp):
                ks = keys_ref[j, g * SUBLANES:(g + 1) * SUBLANES, :]
                hit = (ks > cand_b) if strict else (ks >= cand_b)
                hits.append(jnp.where(hit, 1, 0))
            return cnt + _tree_sum(hits)

        cnt = lax.fori_loop(0, n_tiles, body, jnp.zeros((SUBLANES, bq), I32))
        return jnp.sum(cnt, axis=0, keepdims=True)

    def bit_step(b, t_u):
        cand_u = t_u | lax.shift_left(jnp.int32(1), 31 - b)
        total = count_ge(cand_u ^ jnp.int32(INT_MIN), False)
        return jnp.where(total >= n_sel, cand_u, t_u)

    t_u = lax.fori_loop(0, 32, bit_step, jnp.zeros((1, bq), I32))
    thr = t_u ^ jnp.int32(INT_MIN)
    n_gt = count_ge(thr, True)
    need = jnp.where(thr == jnp.int32(INT_MIN), 0, n_sel - n_gt).astype(F32)

    tri = tri_ref[...]

    def bias_tile(j, carry):
        kt = keys_ref[j]
        eq = kt == thr
        eq_f = jnp.where(eq, 1.0, 0.0)
        prefix = jnp.dot(tri, eq_f.astype(BF16), preferred_element_type=F32) + carry
        sel = (kt > thr) | (eq & (prefix <= need))
        bias_ref[j] = jnp.where(sel, 0.0, NEG_BIG)
        return prefix[bk - 1:bk, :]

    lax.fori_loop(0, n_tiles, bias_tile, jnp.zeros((1, bq), F32))

    m_ref[...] = jnp.full(m_ref.shape, NEG_BIG, F32)
    l_ref[...] = jnp.zeros(l_ref.shape, F32)
    acc_ref[...] = jnp.zeros(acc_ref.shape, F32)

    def attn_tile(j, carry):
        kt = k_ref[pl.ds(pl.multiple_of(j * bk, bk), bk), :]
        bias = bias_ref[j]
        alphas = []
        for h in range(N_HEADS):
            pair = slice((h // 2) * LANES, (h // 2 + 1) * LANES)
            s = jnp.dot(kt[:, pair], qz_ref[h], preferred_element_type=F32) + bias
            s_ref[h] = s
            m_prev = m_ref[h]
            m_new = jnp.maximum(m_prev, jnp.broadcast_to(jnp.max(s, axis=0, keepdims=True), (SUBLANES, bq)))
            m_ref[h] = m_new
            alphas.append(jnp.exp2(m_prev - m_new))
        for h in range(N_HEADS):
            p = jnp.exp2(s_ref[h] - m_ref[h, 0:1, :])
            l_ref[h] = alphas[h] * l_ref[h] + _tree_sum([p[g * SUBLANES:(g + 1) * SUBLANES] for g in range(ngrp)])
            pv = jnp.dot(vt_ref[j, h * HEAD_DIM:(h + 1) * HEAD_DIM, :], p.astype(BF16),
                         preferred_element_type=F32)
            acc_ref[h] = alphas[h][0:1] * acc_ref[h] + pv
        return carry

    lax.fori_loop(0, n_tiles, attn_tile, 0)

    outs = [acc_ref[h] / jnp.sum(l_ref[h], axis=0, keepdims=True) for h in range(N_HEADS)]
    o_ref[...] = jnp.concatenate(outs, axis=0).T.astype(o_ref.dtype)


def _dsa_call(qt, qit, kw, k, vt, tri, batch, seq):
    nt, aw, tw = qt.shape
    bq = BQ
    assert tw == BK and bq % tw == 0 and seq % bq == 0
    nq = seq // bq
    n_sel = min(TOPK_ATTN, seq // 4)
    qtile = lambda b, i: (b * nq + i, 0)
    qtile_t = lambda b, i: (b * nq + i, 0, 0)
    whole = lambda b, i: (b, 0)
    kern = functools.partial(_dsa_kernel, n_sel=n_sel)
    return pl.pallas_call(
        kern,
        out_shape=jax.ShapeDtypeStruct((batch * seq, aw), BF16),
        grid=(batch, nq),
        in_specs=[
            pl.BlockSpec((bq // tw, aw, tw), qtile_t),
            pl.BlockSpec((bq // tw, aw, tw), qtile_t),
            pl.BlockSpec((bq, LANES), qtile),
            pl.BlockSpec((seq, aw), whole),
            pl.BlockSpec((seq // BK, aw, BK), lambda b, i: (b, 0, 0)),
            pl.BlockSpec((seq, LANES), whole),
            pl.BlockSpec(tri.shape, lambda b, i: (0, 0)),
        ],
        out_specs=pl.BlockSpec((bq, aw), qtile),
        scratch_shapes=[
            pltpu.VMEM((seq // BK, BK, bq), I32),
            pltpu.VMEM((seq // BK, BK, bq), F32),
            pltpu.VMEM((N_HEADS, LANES, bq), BF16),
            pltpu.VMEM((IDX_HEADS, LANES, bq), BF16),
            pltpu.VMEM((N_HEADS, BK, bq), F32),
            pltpu.VMEM((N_HEADS, SUBLANES, bq), F32),
            pltpu.VMEM((N_HEADS, SUBLANES, bq), F32),
            pltpu.VMEM((N_HEADS, HEAD_DIM, bq), F32),
        ],
        compiler_params=pltpu.CompilerParams(
            dimension_semantics=("arbitrary", "arbitrary"), vmem_limit_bytes=VMEM_LIMIT),
        name="dsa_attention",
    )(qt, qit, kw, k, vt, kw, tri)


def _post_kernel(x_ref, attn_ref, gyc_ref, sga_ref, mod_ref, wao_ref, wo_ref, o_ref):
    g1 = mod_ref[0, 2:3, :]
    y_attn = jnp.dot(attn_ref[...], wao_ref[...], preferred_element_type=F32)
    mix = gyc_ref[...] + sga_ref[...] * y_attn
    delta = jnp.dot(mix.astype(BF16), wo_ref[...], preferred_element_type=F32)
    o_ref[...] = x_ref[...] + g1 * delta


def _post_call(xf, attn, gyc, sga, mod3, wao, wo, seq):
    t, d = xf.shape
    bt = BT_PROJ
    tiles_per_seq = seq // bt
    aw = attn.shape[1]
    tile = lambda i: (i, 0)
    const = lambda i: (0, 0)
    return pl.pallas_call(
        _post_kernel,
        out_shape=jax.ShapeDtypeStruct((t, d), F32),
        grid=(t // bt,),
        in_specs=[
            pl.BlockSpec((bt, d), tile),
            pl.BlockSpec((bt, aw), tile),
            pl.BlockSpec((bt, d), tile),
            pl.BlockSpec((bt, d), tile),
            pl.BlockSpec((1, N_MOD, d), lambda i: (i // tiles_per_seq, 0, 0)),
            pl.BlockSpec(wao.shape, const),
            pl.BlockSpec(wo.shape, const),
        ],
        out_specs=pl.BlockSpec((bt, d), tile),
        compiler_params=pltpu.CompilerParams(
            dimension_semantics=("arbitrary",), vmem_limit_bytes=VMEM_LIMIT),
        name="post_mix",
    )(xf, attn, gyc, sga, mod3, wao, wo)


def _col_max(a):
    return jnp.max(a, axis=0, keepdims=True)


def _batcher_pairs(n):
    pairs = []
    p = 1
    while p < n:
        k = p
        while k >= 1:
            for j in range(k % p, n - k, 2 * k):
                for i in range(min(k, n - j - k)):
                    if (i + j) // (2 * p) == (i + j + k) // (2 * p):
                        pairs.append((i + j, i + j + k))
            k //= 2
        p *= 2
    return pairs


_SORT16 = _batcher_pairs(PEER_TOPK)
_BITONIC16 = [(k, k + d) for d in (8, 4, 2, 1) for k in range(PEER_TOPK) if not k & d]


def _sort_desc(v, pairs):
    v = list(v)
    for i, j in pairs:
        v[i], v[j] = jnp.maximum(v[i], v[j]), jnp.minimum(v[i], v[j])
    return v


def _peer_kernel(x_ref, mod_ref, n2_ref, wqt_ref, sk_ref, u_ref, vt_ref, o_ref,
                 h2t_ref, s1_ref, s2_ref, ap_ref, bp_ref, nsel_ref, rk_ref, w_ref, acc_ref):
    e = pl.program_id(1)
    n_e = pl.num_programs(1)
    tt = x_ref.shape[0]
    nk = PEER_N_KEYS
    n_i = u_ref.shape[0] // nk
    nslab = tt // LANES

    @pl.when(e == 0)
    def _route():
        xv = x_ref[...]
        sh2 = mod_ref[0, 3:4, :]
        sc2 = mod_ref[0, 4:5, :]
        ms = jnp.mean(xv * xv, axis=-1, keepdims=True)
        h2t = ((xv * lax.rsqrt(ms + EPS) * n2_ref[...]) * (1.0 + sc2) + sh2).T.astype(BF16)
        h2t_ref[...] = h2t
        qt = jnp.dot(wqt_ref[...], h2t, preferred_element_type=F32).astype(BF16)
        for h in range(PEER_HEADS):
            for p, dst in ((0, s1_ref), (1, s2_ref)):
                r0 = (h * 2 + p) * PEER_HALF
                dst[h] = jnp.dot(sk_ref[h, p], qt[r0:r0 + PEER_HALF, :],
                                 preferred_element_type=F32)

        neg_inf = jnp.float32(-jnp.inf)
        sub_id = lax.broadcasted_iota(I32, (SUBLANES, LANES), 0)

        def top16(s):
            v = _sort_desc([s[g * SUBLANES:(g + 1) * SUBLANES] for g in range(nk // SUBLANES)], _SORT16)
            for shift in (4, 2, 1):
                v = [jnp.maximum(v[r], pltpu.roll(v[PEER_TOPK - 1 - r], shift, 0)) for r in range(PEER_TOPK)]
                v = _sort_desc(v, _BITONIC16)
            return v

        def stack8(vals):
            out = vals[0]
            for r in range(1, SUBLANES):
                out = jnp.where(sub_id == r, vals[r], out)
            return out

        def route_head(h, carry):
            for c in range(nslab):
                ls = slice(c * LANES, (c + 1) * LANES)
                s1 = s1_ref[h, :, ls]
                s2 = s2_ref[h, :, ls]
                a = top16(s1)
                b = top16(s2)
                b_lo, b_hi = stack8(b[:SUBLANES]), stack8(b[SUBLANES:])
                cands = [a[0] + b_lo, a[0] + b_hi] + [a[r] + b_lo for r in range(1, PEER_TOPK)]
                work, t = cands, []
                for _ in range(PEER_TOPK):
                    m = _col_max(functools.reduce(jnp.maximum, work))
                    t.append(m)
                    work = [jnp.where(w == m, neg_inf, w) for w in work]
                thr = t[PEER_TOPK - 1]
                z = functools.reduce(jnp.add, [jnp.exp(tk - t[0]) for tk in t])
                hits = [jnp.where(cd >= thr, 1.0, 0.0) for cd in cands]
                n_sel = [jnp.sum(hits[0] + hits[1], axis=0, keepdims=True)]
                n_sel += [jnp.sum(hits[r + 1], axis=0, keepdims=True) for r in range(1, PEER_TOPK)]
                n_b = [jnp.broadcast_to(n, (SUBLANES, LANES)) for n in n_sel]
                cnts, ranks = [], []
                for g in range(nk // SUBLANES):
                    rows = slice(g * 8, (g + 1) * 8)
                    s1g, s2g = s1[rows], s2[rows]
                    cnt = jnp.zeros((SUBLANES, LANES), F32)
                    rank = jnp.full((SUBLANES, LANES), PEER_TOPK + 1.0, F32)
                    for r in range(PEER_TOPK):
                        cnt = jnp.where(s1g == a[r], n_b[r], cnt)
                        rank = jnp.where(s2g == b[r], r + 1.0, rank)
                    cnts.append(cnt)
                    ranks.append(rank)
                nsel_ref[h, c] = jnp.concatenate(cnts, axis=0)
                rk_ref[h, :, ls] = jnp.concatenate(ranks, axis=0).astype(BF16)
                ap_ref[h, c] = jnp.exp(s1 - a[0][0:1])
                bp_ref[h, :, ls] = (jnp.exp(s2 - b[0][0:1]) * (0.5 / z)).astype(BF16)
            return carry

        lax.fori_loop(0, PEER_HEADS, route_head, 0)
        acc_ref[...] = jnp.zeros(acc_ref.shape, F32)

    irow0 = pl.multiple_of(e * n_i, SUBLANES)
    pk = BF16_ROWS
    h2t = h2t_ref[...]
    was = []
    for i0 in range(0, n_i, PEER_IBLK):
        for c in range(nslab):
            ls = slice(c * LANES, (c + 1) * LANES)
            for ii in range(i0, i0 + PEER_IBLK):
                accs = [None] * (nk // pk)
                bcast_row = pl.ds(irow0 + ii, pk, stride=0)
                for h in range(PEER_HEADS):
                    n16 = nsel_ref[h, c, bcast_row, :].astype(BF16)
                    a16 = ap_ref[h, c, bcast_row, :].astype(BF16)
                    for g in range(nk // pk):
                        rows = slice(g * pk, (g + 1) * pk)
                        hit = rk_ref[h, rows, ls] <= n16
                        term = jnp.where(hit, bp_ref[h, rows, ls], jnp.zeros((), BF16)) * a16
                        accs[g] = term if h == 0 else accs[g] + term
                for g in range(nk // pk):
                    w_ref[ii * nk + g * pk:ii * nk + (g + 1) * pk, ls] = accs[g]
        rows = slice(i0 * nk, (i0 + PEER_IBLK) * nk)
        act = jnp.dot(u_ref[rows, :], h2t, preferred_element_type=F32)
        gelu2 = act * (1.0 + lax.erf(act * (2.0 ** -0.5)))
        was.append(w_ref[rows, :] * gelu2.astype(BF16))
    acc_ref[...] += jnp.dot(vt_ref[0], jnp.concatenate(was, axis=0), preferred_element_type=F32)

    @pl.when(e == n_e - 1)
    def _finish():
        g2 = mod_ref[0, 5:6, :]
        o_ref[...] = x_ref[...] + g2 * acc_ref[...].T


def _peer_call(x1, mod3, n2, wqt, sk, u, vt, seq):
    t, d = x1.shape
    tt = TT_PEER
    ec = EC_PEER
    n_chunks = u.shape[0] // ec
    assert ec % (SUBLANES * PEER_N_KEYS) == 0
    tiles_per_seq = seq // tt
    tile = lambda i, e: (i, 0)
    const2 = lambda i, e: (0, 0)
    return pl.pallas_call(
        _peer_kernel,
        out_shape=jax.ShapeDtypeStruct((t, d), F32),
        grid=(t // tt, n_chunks),
        in_specs=[
            pl.BlockSpec((tt, d), tile),
            pl.BlockSpec((1, N_MOD, d), lambda i, e: (i // tiles_per_seq, 0, 0)),
            pl.BlockSpec((1, d), const2),
            pl.BlockSpec(wqt.shape, const2),
            pl.BlockSpec(sk.shape, lambda i, e: (0, 0, 0, 0)),
            pl.BlockSpec((ec, d), lambda i, e: (e, 0)),
            pl.BlockSpec((1, d, ec), lambda i, e: (e, 0, 0)),
        ],
        out_specs=pl.BlockSpec((tt, d), tile),
        scratch_shapes=[
            pltpu.VMEM((d, tt), BF16),
            pltpu.VMEM((PEER_HEADS, PEER_N_KEYS, tt), F32),
            pltpu.VMEM((PEER_HEADS, PEER_N_KEYS, tt), F32),
            pltpu.VMEM((PEER_HEADS, tt // LANES, PEER_N_KEYS, LANES), F32),
            pltpu.VMEM((PEER_HEADS, PEER_N_KEYS, tt), BF16),
            pltpu.VMEM((PEER_HEADS, tt // LANES, PEER_N_KEYS, LANES), F32),
            pltpu.VMEM((PEER_HEADS, PEER_N_KEYS, tt), BF16),
            pltpu.VMEM((ec, tt), BF16),
            pltpu.VMEM((d, tt), F32),
        ],
        compiler_params=pltpu.CompilerParams(
            dimension_semantics=("arbitrary", "arbitrary"), vmem_limit_bytes=VMEM_LIMIT),
        name="peer",
    )(x1, mod3, n2, wqt, sk, u, vt)


def kernel(x, c, w_ada, b_ada, norm1_w, w_in, conv_w, w_conv_out, q_norm_w, k_norm_w,
           w_attn_out, w_o, norm2_w, w_peer_q, peer_sub_keys, peer_u, peer_v):
    b, s, d = x.shape
    depth = w_ada.shape[0]
    cw = conv_w.shape[2]
    aw = N_HEADS * HEAD_DIM
    iw = IDX_HEADS * IDX_DIM
    assert s % BQ == 0 and s % TT_PEER == 0 and s % BT_PROJ == 0
    assert peer_u.shape[1] == PEER_N_KEYS * PEER_N_KEYS

    gidx = jnp.arange(aw) // HEAD_DIM
    gmat = (gidx[:, None] == gidx[None, :]).astype(BF16)
    tri = (jnp.arange(BK)[:, None] >= jnp.arange(BK)[None, :]).astype(BF16)

    xf = x.reshape(b * s, d)
    for layer in range(depth):
        w = w_in[layer]
        o = 0
        wconv = w[:, o:o + 3 * cw].astype(BF16); o += 3 * cw
        wqkv = w[:, o:o + 3 * aw].astype(BF16); o += 3 * aw
        wqi = w[:, o:o + iw].astype(BF16); o += iw
        wkw = jnp.pad(w[:, o:o + IDX_DIM + IDX_HEADS], ((0, 0), (0, LANES - IDX_DIM - IDX_HEADS))).astype(BF16)
        o += IDX_DIM + IDX_HEADS
        wg = w[:, o:o + 2 * d].astype(BF16)
        qn_t = jnp.tile(q_norm_w[layer], N_HEADS).reshape(1, aw)
        kn_t = jnp.tile(k_norm_w[layer], N_HEADS).reshape(1, aw)

        mod3 = _mod_call(c, w_ada[layer], b_ada[layer]).reshape(b, N_MOD, d)
        gyc, sga, qt, k, vt, qit, kw = _inproj_call(
            xf, mod3, norm1_w[layer].reshape(1, d), wconv, wqkv, wqi, wkw, wg, conv_w[layer],
            w_conv_out[layer].astype(BF16), qn_t, kn_t, gmat, s)
        attn = _dsa_call(qt, qit, kw, k, vt, tri, b, s)
        x1 = _post_call(xf, attn, gyc, sga, mod3, w_attn_out[layer].astype(BF16),
                        w_o[layer].astype(BF16), s)
        xf = _peer_call(
            x1, mod3, norm2_w[layer].reshape(1, d), w_peer_q[layer].T.astype(BF16),
            peer_sub_keys[layer].astype(BF16), peer_u[layer].astype(BF16),
            peer_v[layer].astype(BF16).reshape(-1, EC_PEER, d).transpose(0, 2, 1), s)
    return xf.reshape(b, s, d)
```

```python
import functools

import jax
import jax.numpy as jnp
from jax import lax
from jax.experimental import pallas as pl
from jax.experimental.pallas import tpu as pltpu

F32 = jnp.float32
BF16 = jnp.bfloat16
I32 = jnp.int32

N_HEADS = 8
HEAD_DIM = 64
IDX_HEADS = 8
IDX_DIM = 64
TOPK_ATTN = 256
PEER_HEADS = 8
PEER_N_KEYS = 128
PEER_HALF = 64
PEER_TOPK = 16
N_MOD = 6
EPS = 1e-6

LANES = 128
SUBLANES = 8
BF16_ROWS = 16
INT_MIN = -(2 ** 31)
NEG_BIG = -1e30
LOG2E = 1.4426950408889634
VMEM_LIMIT = 56 * 1024 * 1024

BT_PROJ = 512
BQ = 512
BK = 256
TT_PEER = 512
EC_PEER = 2048
PEER_IBLK = 1


def _split_bf16(a):
    hi = a.astype(BF16)
    lo = (a - hi.astype(F32)).astype(BF16)
    return hi, lo


def _mod_kernel(c_ref, w_ref, b_ref, o_ref):
    c = c_ref[...]
    s = c * jax.nn.sigmoid(c)
    s_hi, s_lo = _split_bf16(s)
    w_hi, w_lo = _split_bf16(w_ref[...])
    acc = jnp.dot(s_hi, w_hi, preferred_element_type=F32)
    acc += jnp.dot(s_hi, w_lo, preferred_element_type=F32)
    acc += jnp.dot(s_lo, w_hi, preferred_element_type=F32)
    o_ref[...] = acc + b_ref[...]


def _mod_call(c, w_ada, b_ada):
    b, d = c.shape
    n = w_ada.shape[1]
    bn = d
    return pl.pallas_call(
        _mod_kernel,
        out_shape=jax.ShapeDtypeStruct((b, n), F32),
        grid=(n // bn,),
        in_specs=[
            pl.BlockSpec((b, d), lambda j: (0, 0)),
            pl.BlockSpec((d, bn), lambda j: (0, j)),
            pl.BlockSpec((1, bn), lambda j: (0, j)),
        ],
        out_specs=pl.BlockSpec((b, bn), lambda j: (0, j)),
        compiler_params=pltpu.CompilerParams(
            dimension_semantics=("arbitrary",), vmem_limit_bytes=VMEM_LIMIT),
        name="adaln_mod",
    )(c, w_ada, b_ada.reshape(1, n))


def _inproj_kernel(x_ref, xh_ref, mod_ref, n1_ref, wconv_ref, wqkv_ref, wqi_ref, wkw_ref,
                   wg_ref, convw_ref, wco_ref, qn_ref, kn_ref, gmat_ref,
                   gyc_ref, sga_ref, q_ref, k_ref, v_ref, qi_ref, kw_ref, *, tiles_per_seq):
    i = pl.program_id(0)
    cw = convw_ref.shape[1]
    aw = q_ref.shape[1]
    d = x_ref.shape[1]
    bt = x_ref.shape[0]
    sh1 = mod_ref[0, 0:1, :]
    sc1 = mod_ref[0, 1:2, :]

    def norm_mod(xv):
        ms = jnp.mean(xv * xv, axis=-1, keepdims=True)
        y = xv * lax.rsqrt(ms + EPS) * n1_ref[...]
        return (y * (1.0 + sc1) + sh1).astype(BF16)

    h = norm_mod(x_ref[...])
    hh = norm_mod(xh_ref[...])

    pc = jnp.dot(h, wconv_ref[...], preferred_element_type=F32)
    cb, cc, cx = pc[:, :cw], pc[:, cw:2 * cw], pc[:, 2 * cw:]
    u = cc * cx
    ph = jnp.dot(hh, wconv_ref[:, cw:], preferred_element_type=F32)
    uh = ph[:, :cw] * ph[:, cw:]
    uh = jnp.where(i % tiles_per_seq == 0, 0.0, uh)
    row = lax.broadcasted_iota(I32, (bt, cw), 0)
    prev1 = uh[SUBLANES - 1:SUBLANES, :]
    prev2 = uh[SUBLANES - 2:SUBLANES - 1, :]
    u1 = jnp.where(row == 0, prev1, pltpu.roll(u, 1, 0))
    u2 = pltpu.roll(u, 2, 0)
    u2 = jnp.where(row == 0, prev2, jnp.where(row == 1, prev1, u2))
    conv = convw_ref[0:1, :] * u2 + convw_ref[1:2, :] * u1 + convw_ref[2:3, :] * u
    yc = jnp.dot((cb * conv).astype(BF16), wco_ref[...], preferred_element_type=F32)

    pg = jnp.dot(h, wg_ref[...], preferred_element_type=F32)
    gyc_ref[...] = (jax.nn.sigmoid(pg[:, :d]) * yc).astype(gyc_ref.dtype)
    sga_ref[...] = jax.nn.sigmoid(pg[:, d:]).astype(sga_ref.dtype)

    pq = jnp.dot(h, wqkv_ref[...], preferred_element_type=F32)
    gmat = gmat_ref[...]

    def head_norm(t, w_row):
        hi, lo = _split_bf16(t * t)
        ss = (jnp.dot(hi, gmat, preferred_element_type=F32)
              + jnp.dot(lo, gmat, preferred_element_type=F32))
        return t * lax.rsqrt(ss * (1.0 / HEAD_DIM) + EPS) * w_row

    qn = head_norm(pq[:, :aw], qn_ref[...])
    qs = qn * (HEAD_DIM ** -0.5 * LOG2E)
    k_ref[...] = head_norm(pq[:, aw:2 * aw], kn_ref[...]).astype(BF16)
    vv = pq[:, 2 * aw:]
    qi = jnp.dot(h, wqi_ref[...], preferred_element_type=F32)
    tw = q_ref.shape[2]
    for s in range(bt // tw):
        rows = slice(s * tw, (s + 1) * tw)
        q_ref[s] = qs[rows].T.astype(BF16)
        v_ref[s] = vv[rows].T.astype(BF16)
        qi_ref[s] = qi[rows].T.astype(BF16)
    kw_ref[...] = jnp.dot(h, wkw_ref[...], preferred_element_type=F32)


def _inproj_call(xf, mod3, n1, wconv, wqkv, wqi, wkw, wg, convw, wco, qn_t, kn_t, gmat, seq):
    t, d = xf.shape
    bt = BT_PROJ
    tw = BK
    tiles_per_seq = seq // bt
    cw = convw.shape[1]
    aw = qn_t.shape[1]
    const = lambda i: (0, 0)
    tile = lambda i: (i, 0)
    tile_t = lambda i: (i, 0, 0)
    resident = lambda a: pl.BlockSpec(a.shape, const, pipeline_mode=pl.Buffered(1))
    halo = lambda i: (jnp.maximum(i * (bt // SUBLANES) - 1, 0), 0)
    kern = functools.partial(_inproj_kernel, tiles_per_seq=tiles_per_seq)
    return pl.pallas_call(
        kern,
        out_shape=[
            jax.ShapeDtypeStruct((t, d), BF16),
            jax.ShapeDtypeStruct((t, d), BF16),
            jax.ShapeDtypeStruct((t // tw, aw, tw), BF16),
            jax.ShapeDtypeStruct((t, aw), BF16),
            jax.ShapeDtypeStruct((t // tw, aw, tw), BF16),
            jax.ShapeDtypeStruct((t // tw, aw, tw), BF16),
            jax.ShapeDtypeStruct((t, LANES), F32),
        ],
        grid=(t // bt,),
        in_specs=[
            pl.BlockSpec((bt, d), tile),
            pl.BlockSpec((SUBLANES, d), halo),
            pl.BlockSpec((1, N_MOD, d), lambda i: (i // tiles_per_seq, 0, 0)),
            pl.BlockSpec((1, d), const),
            resident(wconv),
            resident(wqkv),
            resident(wqi),
            resident(wkw),
            resident(wg),
            pl.BlockSpec(convw.shape, const),
            resident(wco),
            pl.BlockSpec((1, aw), const),
            pl.BlockSpec((1, aw), const),
            resident(gmat),
        ],
        out_specs=[
            pl.BlockSpec((bt, d), tile),
            pl.BlockSpec((bt, d), tile),
            pl.BlockSpec((bt // tw, aw, tw), tile_t),
            pl.BlockSpec((bt, aw), tile),
            pl.BlockSpec((bt // tw, aw, tw), tile_t),
            pl.BlockSpec((bt // tw, aw, tw), tile_t),
            pl.BlockSpec((bt, LANES), tile),
        ],
        compiler_params=pltpu.CompilerParams(
            dimension_semantics=("arbitrary",), vmem_limit_bytes=VMEM_LIMIT),
        name="inproj",
    )(xf, xf, mod3, n1, wconv, wqkv, wqi, wkw, wg, convw, wco, qn_t, kn_t, gmat)


def _tree_sum(parts):
    parts = list(parts)
    while len(parts) > 1:
        nxt = [parts[a] + parts[a + 1] for a in range(0, len(parts) - 1, 2)]
        if len(parts) % 2:
            nxt.append(parts[-1])
        parts = nxt
    return parts[0]


def _dsa_kernel(qt_ref, qit_ref, kwq_ref, k_ref, vt_ref, kwk_ref, tri_ref, o_ref,
                keys_ref, bias_ref, qz_ref, qiz_ref, s_ref, m_ref, l_ref, acc_ref, *, n_sel):
    i = pl.program_id(1)
    n_qt, _, tw = qt_ref.shape
    bq = n_qt * tw
    bk = BK
    n_tiles = (i * bq) // bk + bq // bk
    ngrp = bk // SUBLANES

    zpad = jnp.zeros((LANES - HEAD_DIM, bq), BF16)
    for h in range(N_HEADS):
        qh = jnp.concatenate([qt_ref[s, h * HEAD_DIM:(h + 1) * HEAD_DIM, :] for s in range(n_qt)], axis=1)
        qih = jnp.concatenate([qit_ref[s, h * IDX_DIM:(h + 1) * IDX_DIM, :] for s in range(n_qt)], axis=1)
        qz_ref[h] = jnp.concatenate([qh, zpad] if h % 2 == 0 else [zpad, qh], axis=0)
        qiz_ref[h] = jnp.concatenate([qih, zpad], axis=0)
    wt = kwq_ref[...].T

    q_pos = i * bq + lax.broadcasted_iota(I32, (bk, bq), 1)

    def score_tile(j, carry):
        kw = kwk_ref[pl.ds(pl.multiple_of(j * bk, bk), bk), :].astype(BF16)
        score = jnp.zeros((bk, bq), F32)
        for h in range(IDX_HEADS):
            logits = jnp.dot(kw, qiz_ref[h], preferred_element_type=F32)
            score = score + wt[IDX_DIM + h:IDX_DIM + h + 1, :] * jnp.maximum(logits, 0.0)
        bits = pltpu.bitcast(score, I32)
        key = jnp.where(bits < 0, bits ^ jnp.int32(0x7FFFFFFF), bits)
        key = jnp.where(bits == jnp.int32(INT_MIN), 0, key)
        key_pos = j * bk + lax.broadcasted_iota(I32, (bk, bq), 0)
        keys_ref[j] = jnp.where(key_pos <= q_pos, key, jnp.int32(INT_MIN))
        return carry

    lax.fori_loop(0, n_tiles, score_tile, 0)

    def count_ge(cand, strict):
        cand_b = jnp.broadcast_to(cand, (SUBLANES, bq))

        def body(j, cnt):
            hits = []
            for g in range(ngrp):
                ks = keys_ref[j, g * SUBLANES:(g + 1) * SUBLANES, :]
                hit = (ks > cand_b) if strict else (ks >= cand_b)
                hits.append(jnp.where(hit, 1, 0))
            return cnt + _tree_sum(hits)

        cnt = lax.fori_loop(0, n_tiles, body, jnp.zeros((SUBLANES, bq), I32))
        return jnp.sum(cnt, axis=0, keepdims=True)

    def bit_step(b, t_u):
        cand_u = t_u | lax.shift_left(jnp.int32(1), 31 - b)
        total = count_ge(cand_u ^ jnp.int32(INT_MIN), False)
        return jnp.where(total >= n_sel, cand_u, t_u)

    t_u = lax.fori_loop(0, 32, bit_step, jnp.zeros((1, bq), I32))
    thr = t_u ^ jnp.int32(INT_MIN)
    n_gt = count_ge(thr, True)
    need = jnp.where(thr == jnp.int32(INT_MIN), 0, n_sel - n_gt).astype(F32)

    tri = tri_ref[...]

    def bias_tile(j, carry):
        kt = keys_ref[j]
        eq = kt == thr
        eq_f = jnp.where(eq, 1.0, 0.0)
        prefix = jnp.dot(tri, eq_f.astype(BF16), preferred_element_type=F32) + carry
        sel = (kt > thr) | (eq & (prefix <= need))
        bias_ref[j] = jnp.where(sel, 0.0, NEG_BIG)
        return prefix[bk - 1:bk, :]

    lax.fori_loop(0, n_tiles, bias_tile, jnp.zeros((1, bq), F32))

    m_ref[...] = jnp.full(m_ref.shape, NEG_BIG, F32)
    l_ref[...] = jnp.zeros(l_ref.shape, F32)
    acc_ref[...] = jnp.zeros(acc_ref.shape, F32)

    def attn_tile(j, carry):
        kt = k_ref[pl.ds(pl.multiple_of(j * bk, bk), bk), :]
        bias = bias_ref[j]
        alphas = []
        for h in range(N_HEADS):
            pair = slice((h // 2) * LANES, (h // 2 + 1) * LANES)
            s = jnp.dot(kt[:, pair], qz_ref[h], preferred_element_type=F32) + bias
            s_ref[h] = s
            m_prev = m_ref[h]
            m_new = jnp.maximum(m_prev, jnp.broadcast_to(jnp.max(s, axis=0, keepdims=True), (SUBLANES, bq)))
            m_ref[h] = m_new
            alphas.append(jnp.exp2(m_prev - m_new))
        for h in range(N_HEADS):
            p = jnp.exp2(s_ref[h] - m_ref[h, 0:1, :])
            l_ref[h] = alphas[h] * l_ref[h] + _tree_sum([p[g * SUBLANES:(g + 1) * SUBLANES] for g in range(ngrp)])
            pv = jnp.dot(vt_ref[j, h * HEAD_DIM:(h + 1) * HEAD_DIM, :], p.astype(BF16),
                         preferred_element_type=F32)
            acc_ref[h] = alphas[h][0:1] * acc_ref[h] + pv
        return carry

    lax.fori_loop(0, n_tiles, attn_tile, 0)

    outs = [acc_ref[h] / jnp.sum(l_ref[h], axis=0, keepdims=True) for h in range(N_HEADS)]
    o_ref[...] = jnp.concatenate(outs, axis=0).T.astype(o_ref.dtype)


def _dsa_call(qt, qit, kw, k, vt, tri, batch, seq):
    nt, aw, tw = qt.shape
    bq = BQ
    assert tw == BK and bq % tw == 0 and seq % bq == 0
    nq = seq // bq
    n_sel = min(TOPK_ATTN, seq // 4)
    qtile = lambda b, i: (b * nq + i, 0)
    qtile_t = lambda b, i: (b * nq + i, 0, 0)
    whole = lambda b, i: (b, 0)
    kern = functools.partial(_dsa_kernel, n_sel=n_sel)
    return pl.pallas_call(
        kern,
        out_shape=jax.ShapeDtypeStruct((batch * seq, aw), BF16),
        grid=(batch, nq),
        in_specs=[
            pl.BlockSpec((bq // tw, aw, tw), qtile_t),
            pl.BlockSpec((bq // tw, aw, tw), qtile_t),
            pl.BlockSpec((bq, LANES), qtile),
            pl.BlockSpec((seq, aw), whole),
            pl.BlockSpec((seq // BK, aw, BK), lambda b, i: (b, 0, 0)),
            pl.BlockSpec((seq, LANES), whole),
            pl.BlockSpec(tri.shape, lambda b, i: (0, 0)),
        ],
        out_specs=pl.BlockSpec((bq, aw), qtile),
        scratch_shapes=[
            pltpu.VMEM((seq // BK, BK, bq), I32),
            pltpu.VMEM((seq // BK, BK, bq), F32),
            pltpu.VMEM((N_HEADS, LANES, bq), BF16),
            pltpu.VMEM((IDX_HEADS, LANES, bq), BF16),
            pltpu.VMEM((N_HEADS, BK, bq), F32),
            pltpu.VMEM((N_HEADS, SUBLANES, bq), F32),
            pltpu.VMEM((N_HEADS, SUBLANES, bq), F32),
            pltpu.VMEM((N_HEADS, HEAD_DIM, bq), F32),
        ],
        compiler_params=pltpu.CompilerParams(
            dimension_semantics=("arbitrary", "arbitrary"), vmem_limit_bytes=VMEM_LIMIT),
        name="dsa_attention",
    )(qt, qit, kw, k, vt, kw, tri)


def _post_kernel(x_ref, attn_ref, gyc_ref, sga_ref, mod_ref, wao_ref, wo_ref, o_ref):
    g1 = mod_ref[0, 2:3, :]
    y_attn = jnp.dot(attn_ref[...], wao_ref[...], preferred_element_type=F32)
    mix = gyc_ref[...] + sga_ref[...] * y_attn
    delta = jnp.dot(mix.astype(BF16), wo_ref[...], preferred_element_type=F32)
    o_ref[...] = x_ref[...] + g1 * delta


def _post_call(xf, attn, gyc, sga, mod3, wao, wo, seq):
    t, d = xf.shape
    bt = BT_PROJ
    tiles_per_seq = seq // bt
    aw = attn.shape[1]
    tile = lambda i: (i, 0)
    const = lambda i: (0, 0)
    return pl.pallas_call(
        _post_kernel,
        out_shape=jax.ShapeDtypeStruct((t, d), F32),
        grid=(t // bt,),
        in_specs=[
            pl.BlockSpec((bt, d), tile),
            pl.BlockSpec((bt, aw), tile),
            pl.BlockSpec((bt, d), tile),
            pl.BlockSpec((bt, d), tile),
            pl.BlockSpec((1, N_MOD, d), lambda i: (i // tiles_per_seq, 0, 0)),
            pl.BlockSpec(wao.shape, const),
            pl.BlockSpec(wo.shape, const),
        ],
        out_specs=pl.BlockSpec((bt, d), tile),
        compiler_params=pltpu.CompilerParams(
            dimension_semantics=("arbitrary",), vmem_limit_bytes=VMEM_LIMIT),
        name="post_mix",
    )(xf, attn, gyc, sga, mod3, wao, wo)


def _col_max(a):
    return jnp.max(a, axis=0, keepdims=True)


def _batcher_pairs(n):
    pairs = []
    p = 1
    while p < n:
        k = p
        while k >= 1:
            for j in range(k % p, n - k, 2 * k):
                for i in range(min(k, n - j - k)):
                    if (i + j) // (2 * p) == (i + j + k) // (2 * p):
                        pairs.append((i + j, i + j + k))
            k //= 2
        p *= 2
    return pairs


_SORT16 = _batcher_pairs(PEER_TOPK)
_BITONIC16 = [(k, k + d) for d in (8, 4, 2, 1) for k in range(PEER_TOPK) if not k & d]


def _pair_tiles():
    tiles = []
    for r in range(2, PEER_TOPK):
        n = PEER_TOPK // (r + 1)
        for tl in tiles:
            used = sum(k for _, k, _ in tl)
            if used + n <= SUBLANES:
                tl.append((r, n, used))
                break
        else:
            tiles.append([(r, n, 0)])
    return tiles


_PAIR_TILES = _pair_tiles()


def _sort_desc(v, pairs):
    v = list(v)
    for i, j in pairs:
        v[i], v[j] = jnp.maximum(v[i], v[j]), jnp.minimum(v[i], v[j])
    return v


def _peer_kernel(x_ref, mod_ref, n2_ref, wqt_ref, sk_ref, u_ref, vt_ref, o_ref,
                 h2t_ref, s1_ref, s2_ref, ap_ref, bp_ref, nsel_ref, rk_ref, w_ref, acc_ref):
    e = pl.program_id(1)
    n_e = pl.num_programs(1)
    tt = x_ref.shape[0]
    nk = PEER_N_KEYS
    n_i = u_ref.shape[0] // nk
    nslab = tt // LANES

    @pl.when(e == 0)
    def _route():
        xv = x_ref[...]
        sh2 = mod_ref[0, 3:4, :]
        sc2 = mod_ref[0, 4:5, :]
        ms = jnp.mean(xv * xv, axis=-1, keepdims=True)
        h2t = ((xv * lax.rsqrt(ms + EPS) * n2_ref[...]) * (1.0 + sc2) + sh2).T.astype(BF16)
        h2t_ref[...] = h2t
        qt = jnp.dot(wqt_ref[...], h2t, preferred_element_type=F32).astype(BF16)
        for h in range(PEER_HEADS):
            for p, dst in ((0, s1_ref), (1, s2_ref)):
                r0 = (h * 2 + p) * PEER_HALF
                dst[h] = jnp.dot(sk_ref[h, p], qt[r0:r0 + PEER_HALF, :],
                                 preferred_element_type=F32)

        neg_inf = jnp.float32(-jnp.inf)
        sub_id = lax.broadcasted_iota(I32, (SUBLANES, LANES), 0)

        def top16(s):
            v = _sort_desc([s[g * SUBLANES:(g + 1) * SUBLANES] for g in range(nk // SUBLANES)], _SORT16)
            for shift in (4, 2, 1):
                v = [jnp.maximum(v[r], pltpu.roll(v[PEER_TOPK - 1 - r], shift, 0)) for r in range(PEER_TOPK)]
                v = _sort_desc(v, _BITONIC16)
            return v

        def stack8(vals):
            out = vals[0]
            for r in range(1, SUBLANES):
                out = jnp.where(sub_id == r, vals[r], out)
            return out

        def route_head(h, carry):
            for c in range(nslab):
                ls = slice(c * LANES, (c + 1) * LANES)
                s1 = s1_ref[h, :, ls]
                s2 = s2_ref[h, :, ls]
                a = top16(s1)
                b = top16(s2)
                b_lo, b_hi = stack8(b[:SUBLANES]), stack8(b[SUBLANES:])
                cands = [a[0] + b_lo, a[0] + b_hi, a[1] + b_lo]
                for segs in _PAIR_TILES:
                    a_pk, b_pk, used = a[segs[0][0]], None, 0
                    for r, n, start in segs:
                        if start:
                            a_pk = jnp.where(sub_id >= start, a[r], a_pk)
                        for cc in range(n):
                            b_pk = b[cc] if b_pk is None else jnp.where(sub_id == start + cc, b[cc], b_pk)
                        used = start + n
                    pk_sum = a_pk + b_pk
                    cands.append(pk_sum if used == SUBLANES else jnp.where(sub_id < used, pk_sum, neg_inf))
                work, t = cands, []
                for _ in range(PEER_TOPK):
                    m = _col_max(functools.reduce(jnp.maximum, work))
                    t.append(m)
                    work = [jnp.where(w == m, neg_inf, w) for w in work]
                thr = t[PEER_TOPK - 1]
                z = functools.reduce(jnp.add, [jnp.exp(tk - t[0]) for tk in t])
                hits = [jnp.where(cd >= thr, 1.0, 0.0) for cd in cands]
                n_sel = [None] * PEER_TOPK
                n_sel[0] = jnp.sum(hits[0] + hits[1], axis=0, keepdims=True)
                n_sel[1] = jnp.sum(hits[2], axis=0, keepdims=True)
                for ti, segs in enumerate(_PAIR_TILES):
                    for r, n, start in segs:
                        in_seg = (sub_id >= start) & (sub_id < start + n)
                        n_sel[r] = jnp.sum(jnp.where(in_seg, hits[3 + ti], 0.0), axis=0, keepdims=True)
                n_b = [jnp.broadcast_to(n, (SUBLANES, LANES)) for n in n_sel]
                cnts, ranks = [], []
                for g in range(nk // SUBLANES):
                    rows = slice(g * 8, (g + 1) * 8)
                    s1g, s2g = s1[rows], s2[rows]
                    cnt = jnp.zeros((SUBLANES, LANES), F32)
                    rank = jnp.full((SUBLANES, LANES), PEER_TOPK + 1.0, F32)
                    for r in range(PEER_TOPK):
                        cnt = jnp.where(s1g == a[r], n_b[r], cnt)
                        rank = jnp.where(s2g == b[r], r + 1.0, rank)
                    cnts.append(cnt)
                    ranks.append(rank)
                nsel_ref[h, c] = jnp.concatenate(cnts, axis=0)
                rk_ref[h, :, ls] = jnp.concatenate(ranks, axis=0).astype(BF16)
                ap_ref[h, c] = jnp.exp(s1 - a[0][0:1])
                bp_ref[h, :, ls] = (jnp.exp(s2 - b[0][0:1]) * (0.5 / z)).astype(BF16)
            return carry

        lax.fori_loop(0, PEER_HEADS, route_head, 0)
        acc_ref[...] = jnp.zeros(acc_ref.shape, F32)

    irow0 = pl.multiple_of(e * n_i, SUBLANES)
    pk = BF16_ROWS
    n_blk = [[nsel_ref[h, c, pl.ds(irow0, n_i), :] for h in range(PEER_HEADS)] for c in range(nslab)]
    ap_blk = [[ap_ref[h, c, pl.ds(irow0, n_i), :] for h in range(PEER_HEADS)] for c in range(nslab)]
    h2t = h2t_ref[...]
    was = []
    for i0 in range(0, n_i, PEER_IBLK):
        for c in range(nslab):
            ls = slice(c * LANES, (c + 1) * LANES)
            for ii in range(i0, i0 + PEER_IBLK):
                accs = [None] * (nk // pk)
                for h in range(PEER_HEADS):
                    n16 = jnp.broadcast_to(n_blk[c][h][ii:ii + 1], (pk, LANES)).astype(BF16)
                    a16 = jnp.broadcast_to(ap_blk[c][h][ii:ii + 1], (pk, LANES)).astype(BF16)
                    for g in range(nk // pk):
                        rows = slice(g * pk, (g + 1) * pk)
                        hit = rk_ref[h, rows, ls] <= n16
                        term = jnp.where(hit, bp_ref[h, rows, ls], jnp.zeros((), BF16)) * a16
                        accs[g] = term if h == 0 else accs[g] + term
                for g in range(nk // pk):
                    w_ref[ii * nk + g * pk:ii * nk + (g + 1) * pk, ls] = accs[g]
        rows = slice(i0 * nk, (i0 + PEER_IBLK) * nk)
        act = jnp.dot(u_ref[rows, :], h2t, preferred_element_type=F32)
        gelu2 = act * (1.0 + lax.erf(act * (2.0 ** -0.5)))
        was.append(w_ref[rows, :] * gelu2.astype(BF16))
    acc_ref[...] += jnp.dot(vt_ref[0], jnp.concatenate(was, axis=0), preferred_element_type=F32)

    @pl.when(e == n_e - 1)
    def _finish():
        g2 = mod_ref[0, 5:6, :]
        o_ref[...] = x_ref[...] + g2 * acc_ref[...].T


def _peer_call(x1, mod3, n2, wqt, sk, u, vt, seq):
    t, d = x1.shape
    tt = TT_PEER
    ec = EC_PEER
    n_chunks = u.shape[0] // ec
    assert ec % (SUBLANES * PEER_N_KEYS) == 0
    tiles_per_seq = seq // tt
    tile = lambda i, e: (i, 0)
    const2 = lambda i, e: (0, 0)
    return pl.pallas_call(
        _peer_kernel,
        out_shape=jax.ShapeDtypeStruct((t, d), F32),
        grid=(t // tt, n_chunks),
        in_specs=[
            pl.BlockSpec((tt, d), tile),
            pl.BlockSpec((1, N_MOD, d), lambda i, e: (i // tiles_per_seq, 0, 0)),
            pl.BlockSpec((1, d), const2),
            pl.BlockSpec(wqt.shape, const2),
            pl.BlockSpec(sk.shape, lambda i, e: (0, 0, 0, 0)),
            pl.BlockSpec((ec, d), lambda i, e: (e, 0)),
            pl.BlockSpec((1, d, ec), lambda i, e: (e, 0, 0)),
        ],
        out_specs=pl.BlockSpec((tt, d), tile),
        scratch_shapes=[
            pltpu.VMEM((d, tt), BF16),
            pltpu.VMEM((PEER_HEADS, PEER_N_KEYS, tt), F32),
            pltpu.VMEM((PEER_HEADS, PEER_N_KEYS, tt), F32),
            pltpu.VMEM((PEER_HEADS, tt // LANES, PEER_N_KEYS, LANES), F32),
            pltpu.VMEM((PEER_HEADS, PEER_N_KEYS, tt), BF16),
            pltpu.VMEM((PEER_HEADS, tt // LANES, PEER_N_KEYS, LANES), F32),
            pltpu.VMEM((PEER_HEADS, PEER_N_KEYS, tt), BF16),
            pltpu.VMEM((ec, tt), BF16),
            pltpu.VMEM((d, tt), F32),
        ],
        compiler_params=pltpu.CompilerParams(
            dimension_semantics=("arbitrary", "arbitrary"), vmem_limit_bytes=VMEM_LIMIT),
        name="peer",
    )(x1, mod3, n2, wqt, sk, u, vt)


def kernel(x, c, w_ada, b_ada, norm1_w, w_in, conv_w, w_conv_out, q_norm_w, k_norm_w,
           w_attn_out, w_o, norm2_w, w_peer_q, peer_sub_keys, peer_u, peer_v):
    b, s, d = x.shape
    depth = w_ada.shape[0]
    cw = conv_w.shape[2]
    aw = N_HEADS * HEAD_DIM
    iw = IDX_HEADS * IDX_DIM
    assert s % BQ == 0 and s % TT_PEER == 0 and s % BT_PROJ == 0
    assert peer_u.shape[1] == PEER_N_KEYS * PEER_N_KEYS

    gidx = jnp.arange(aw) // HEAD_DIM
    gmat = (gidx[:, None] == gidx[None, :]).astype(BF16)
    tri = (jnp.arange(BK)[:, None] >= jnp.arange(BK)[None, :]).astype(BF16)

    xf = x.reshape(b * s, d)
    for layer in range(depth):
        w = w_in[layer]
        o = 0
        wconv = w[:, o:o + 3 * cw].astype(BF16); o += 3 * cw
        wqkv = w[:, o:o + 3 * aw].astype(BF16); o += 3 * aw
        wqi = w[:, o:o + iw].astype(BF16); o += iw
        wkw = jnp.pad(w[:, o:o + IDX_DIM + IDX_HEADS], ((0, 0), (0, LANES - IDX_DIM - IDX_HEADS))).astype(BF16)
        o += IDX_DIM + IDX_HEADS
        wg = w[:, o:o + 2 * d].astype(BF16)
        qn_t = jnp.tile(q_norm_w[layer], N_HEADS).reshape(1, aw)
        kn_t = jnp.tile(k_norm_w[layer], N_HEADS).reshape(1, aw)

        mod3 = _mod_call(c, w_ada[layer], b_ada[layer]).reshape(b, N_MOD, d)
        gyc, sga, qt, k, vt, qit, kw = _inproj_call(
            xf, mod3, norm1_w[layer].reshape(1, d), wconv, wqkv, wqi, wkw, wg, conv_w[layer],
            w_conv_out[layer].astype(BF16), qn_t, kn_t, gmat, s)
        attn = _dsa_call(qt, qit, kw, k, vt, tri, b, s)
        x1 = _post_call(xf, attn, gyc, sga, mod3, w_attn_out[layer].astype(BF16),
                        w_o[layer].astype(BF16), s)
        xf = _peer_call(
            x1, mod3, norm2_w[layer].reshape(1, d), w_peer_q[layer].T.astype(BF16),
            peer_sub_keys[layer].astype(BF16), peer_u[layer].astype(BF16),
            peer_v[layer].astype(BF16).reshape(-1, EC_PEER, d).transpose(0, 2, 1), s)
    return xf.reshape(b, s, d)
```

```python
import functools

import jax
import jax.numpy as jnp
from jax import lax
from jax.experimental import pallas as pl
from jax.experimental.pallas import tpu as pltpu

F32 = jnp.float32
BF16 = jnp.bfloat16
I32 = jnp.int32

N_HEADS = 8
HEAD_DIM = 64
IDX_HEADS = 8
IDX_DIM = 64
TOPK_ATTN = 256
PEER_HEADS = 8
PEER_N_KEYS = 128
PEER_HALF = 64
PEER_TOPK = 16
N_MOD = 6
EPS = 1e-6

LANES = 128
SUBLANES = 8
BF16_ROWS = 16
INT_MIN = -(2 ** 31)
NEG_BIG = -1e30
LOG2E = 1.4426950408889634
VMEM_LIMIT = 56 * 1024 * 1024

BT_PROJ = 512
BQ = 512
BK = 256
TT_PEER = 512
EC_PEER = 2048
PEER_IBLK = 1


def _split_bf16(a):
    hi = a.astype(BF16)
    lo = (a - hi.astype(F32)).astype(BF16)
    return hi, lo


def _mod_kernel(c_ref, w_ref, b_ref, o_ref):
    c = c_ref[...]
    s = c * jax.nn.sigmoid(c)
    s_hi, s_lo = _split_bf16(s)
    w_hi, w_lo = _split_bf16(w_ref[...])
    acc = jnp.dot(s_hi, w_hi, preferred_element_type=F32)
    acc += jnp.dot(s_hi, w_lo, preferred_element_type=F32)
    acc += jnp.dot(s_lo, w_hi, preferred_element_type=F32)
    o_ref[...] = acc + b_ref[...]


def _mod_call(c, w_ada, b_ada):
    b, d = c.shape
    n = w_ada.shape[1]
    bn = d
    return pl.pallas_call(
        _mod_kernel,
        out_shape=jax.ShapeDtypeStruct((b, n), F32),
        grid=(n // bn,),
        in_specs=[
            pl.BlockSpec((b, d), lambda j: (0, 0)),
            pl.BlockSpec((d, bn), lambda j: (0, j)),
            pl.BlockSpec((1, bn), lambda j: (0, j)),
        ],
        out_specs=pl.BlockSpec((b, bn), lambda j: (0, j)),
        compiler_params=pltpu.CompilerParams(
            dimension_semantics=("arbitrary",), vmem_limit_bytes=VMEM_LIMIT),
        name="adaln_mod",
    )(c, w_ada, b_ada.reshape(1, n))


def _inproj_kernel(x_ref, xh_ref, mod_ref, n1_ref, wconv_ref, wqkv_ref, wqi_ref, wkw_ref,
                   wg_ref, convw_ref, wco_ref, qn_ref, kn_ref, gmat_ref,
                   gyc_ref, sga_ref, q_ref, k_ref, v_ref, qi_ref, kw_ref, *, tiles_per_seq):
    i = pl.program_id(0)
    cw = convw_ref.shape[1]
    aw = q_ref.shape[1]
    d = x_ref.shape[1]
    bt = x_ref.shape[0]
    sh1 = mod_ref[0, 0:1, :]
    sc1 = mod_ref[0, 1:2, :]

    def norm_mod(xv):
        ms = jnp.mean(xv * xv, axis=-1, keepdims=True)
        y = xv * lax.rsqrt(ms + EPS) * n1_ref[...]
        return (y * (1.0 + sc1) + sh1).astype(BF16)

    h = norm_mod(x_ref[...])
    hh = norm_mod(xh_ref[...])

    pc = jnp.dot(h, wconv_ref[...], preferred_element_type=F32)
    cb, cc, cx = pc[:, :cw], pc[:, cw:2 * cw], pc[:, 2 * cw:]
    u = cc * cx
    ph = jnp.dot(hh, wconv_ref[:, cw:], preferred_element_type=F32)
    uh = ph[:, :cw] * ph[:, cw:]
    uh = jnp.where(i % tiles_per_seq == 0, 0.0, uh)
    row = lax.broadcasted_iota(I32, (bt, cw), 0)
    prev1 = uh[SUBLANES - 1:SUBLANES, :]
    prev2 = uh[SUBLANES - 2:SUBLANES - 1, :]
    u1 = jnp.where(row == 0, prev1, pltpu.roll(u, 1, 0))
    u2 = pltpu.roll(u, 2, 0)
    u2 = jnp.where(row == 0, prev2, jnp.where(row == 1, prev1, u2))
    conv = convw_ref[0:1, :] * u2 + convw_ref[1:2, :] * u1 + convw_ref[2:3, :] * u
    yc = jnp.dot((cb * conv).astype(BF16), wco_ref[...], preferred_element_type=F32)

    pg = jnp.dot(h, wg_ref[...], preferred_element_type=F32)
    gyc_ref[...] = (jax.nn.sigmoid(pg[:, :d]) * yc).astype(gyc_ref.dtype)
    sga_ref[...] = jax.nn.sigmoid(pg[:, d:]).astype(sga_ref.dtype)

    pq = jnp.dot(h, wqkv_ref[...], preferred_element_type=F32)
    gmat = gmat_ref[...]

    def head_norm(t, w_row):
        hi, lo = _split_bf16(t * t)
        ss = (jnp.dot(hi, gmat, preferred_element_type=F32)
              + jnp.dot(lo, gmat, preferred_element_type=F32))
        return t * lax.rsqrt(ss * (1.0 / HEAD_DIM) + EPS) * w_row

    qn = head_norm(pq[:, :aw], qn_ref[...])
    qs = qn * (HEAD_DIM ** -0.5 * LOG2E)
    k_ref[...] = head_norm(pq[:, aw:2 * aw], kn_ref[...]).astype(BF16)
    vv = pq[:, 2 * aw:]
    qi = jnp.dot(h, wqi_ref[...], preferred_element_type=F32)
    tw = q_ref.shape[2]
    for s in range(bt // tw):
        rows = slice(s * tw, (s + 1) * tw)
        q_ref[s] = qs[rows].T.astype(BF16)
        v_ref[s] = vv[rows].T.astype(BF16)
        qi_ref[s] = qi[rows].T.astype(BF16)
    kw_ref[...] = jnp.dot(h, wkw_ref[...], preferred_element_type=F32)


def _inproj_call(xf, mod3, n1, wconv, wqkv, wqi, wkw, wg, convw, wco, qn_t, kn_t, gmat, seq):
    t, d = xf.shape
    bt = BT_PROJ
    tw = BK
    tiles_per_seq = seq // bt
    cw = convw.shape[1]
    aw = qn_t.shape[1]
    const = lambda i: (0, 0)
    tile = lambda i: (i, 0)
    tile_t = lambda i: (i, 0, 0)
    resident = lambda a: pl.BlockSpec(a.shape, const, pipeline_mode=pl.Buffered(1))
    halo = lambda i: (jnp.maximum(i * (bt // SUBLANES) - 1, 0), 0)
    kern = functools.partial(_inproj_kernel, tiles_per_seq=tiles_per_seq)
    return pl.pallas_call(
        kern,
        out_shape=[
            jax.ShapeDtypeStruct((t, d), BF16),
            jax.ShapeDtypeStruct((t, d), BF16),
            jax.ShapeDtypeStruct((t // tw, aw, tw), BF16),
            jax.ShapeDtypeStruct((t, aw), BF16),
            jax.ShapeDtypeStruct((t // tw, aw, tw), BF16),
            jax.ShapeDtypeStruct((t // tw, aw, tw), BF16),
            jax.ShapeDtypeStruct((t, LANES), F32),
        ],
        grid=(t // bt,),
        in_specs=[
            pl.BlockSpec((bt, d), tile),
            pl.BlockSpec((SUBLANES, d), halo),
            pl.BlockSpec((1, N_MOD, d), lambda i: (i // tiles_per_seq, 0, 0)),
            pl.BlockSpec((1, d), const),
            resident(wconv),
            resident(wqkv),
            resident(wqi),
            resident(wkw),
            resident(wg),
            pl.BlockSpec(convw.shape, const),
            resident(wco),
            pl.BlockSpec((1, aw), const),
            pl.BlockSpec((1, aw), const),
            resident(gmat),
        ],
        out_specs=[
            pl.BlockSpec((bt, d), tile),
            pl.BlockSpec((bt, d), tile),
            pl.BlockSpec((bt // tw, aw, tw), tile_t),
            pl.BlockSpec((bt, aw), tile),
            pl.BlockSpec((bt // tw, aw, tw), tile_t),
            pl.BlockSpec((bt // tw, aw, tw), tile_t),
            pl.BlockSpec((bt, LANES), tile),
        ],
        compiler_params=pltpu.CompilerParams(
            dimension_semantics=("arbitrary",), vmem_limit_bytes=VMEM_LIMIT),
        name="inproj",
    )(xf, xf, mod3, n1, wconv, wqkv, wqi, wkw, wg, convw, wco, qn_t, kn_t, gmat)


def _tree_sum(parts):
    parts = list(parts)
    while len(parts) > 1:
        nxt = [parts[a] + parts[a + 1] for a in range(0, len(parts) - 1, 2)]
        if len(parts) % 2:
            nxt.append(parts[-1])
        parts = nxt
    return parts[0]


def _dsa_kernel(qt_ref, qit_ref, kwq_ref, k_ref, vt_ref, kwk_ref, tri_ref, o_ref,
                keys_ref, bias_ref, qz_ref, qiz_ref, s_ref, m_ref, l_ref, acc_ref, *, n_sel):
    i = pl.program_id(1)
    n_qt, _, tw = qt_ref.shape
    bq = n_qt * tw
    bk = BK
    n_tiles = (i * bq) // bk + bq // bk
    ngrp = bk // SUBLANES

    zpad = jnp.zeros((LANES - HEAD_DIM, bq), BF16)
    for h in range(N_HEADS):
        qh = jnp.concatenate([qt_ref[s, h * HEAD_DIM:(h + 1) * HEAD_DIM, :] for s in range(n_qt)], axis=1)
        qih = jnp.concatenate([qit_ref[s, h * IDX_DIM:(h + 1) * IDX_DIM, :] for s in range(n_qt)], axis=1)
        qz_ref[h] = jnp.concatenate([qh, zpad] if h % 2 == 0 else [zpad, qh], axis=0)
        qiz_ref[h] = jnp.concatenate([qih, zpad], axis=0)
    wt = kwq_ref[...].T

    q_pos = i * bq + lax.broadcasted_iota(I32, (bk, bq), 1)

    def score_tile(j, carry):
        kw = kwk_ref[pl.ds(pl.multiple_of(j * bk, bk), bk), :].astype(BF16)
        score = jnp.zeros((bk, bq), F32)
        for h in range(IDX_HEADS):
            logits = jnp.dot(kw, qiz_ref[h], preferred_element_type=F32)
            score = score + wt[IDX_DIM + h:IDX_DIM + h + 1, :] * jnp.maximum(logits, 0.0)
        bits = pltpu.bitcast(score, I32)
        key = jnp.where(bits < 0, bits ^ jnp.int32(0x7FFFFFFF), bits)
        key = jnp.where(bits == jnp.int32(INT_MIN), 0, key)
        key_pos = j * bk + lax.broadcasted_iota(I32, (bk, bq), 0)
        keys_ref[j] = jnp.where(key_pos <= q_pos, key, jnp.int32(INT_MIN))
        return carry

    lax.fori_loop(0, n_tiles, score_tile, 0)

    def count_ge(cand, strict):
        cand_b = jnp.broadcast_to(cand, (SUBLANES, bq))

        def body(j, cnt):
            parts = [cnt, None, None, None]
            for g in range(ngrp):
                ks = keys_ref[j, g * SUBLANES:(g + 1) * SUBLANES, :]
                hit = (ks > cand_b) if strict else (ks >= cand_b)
                one = jnp.where(hit, 1, 0)
                k = g % len(parts)
                parts[k] = one if parts[k] is None else parts[k] + one
            return _tree_sum(parts)

        cnt = lax.fori_loop(0, n_tiles, body, jnp.zeros((SUBLANES, bq), I32))
        return jnp.sum(cnt, axis=0, keepdims=True)

    def bit_step(b, t_u):
        cand_u = t_u | lax.shift_left(jnp.int32(1), 31 - b)
        total = count_ge(cand_u ^ jnp.int32(INT_MIN), False)
        return jnp.where(total >= n_sel, cand_u, t_u)

    t_u = lax.fori_loop(0, 32, bit_step, jnp.zeros((1, bq), I32))
    thr = t_u ^ jnp.int32(INT_MIN)
    n_gt = count_ge(thr, True)
    need = jnp.where(thr == jnp.int32(INT_MIN), 0, n_sel - n_gt).astype(F32)

    tri = tri_ref[...]

    def bias_tile(j, carry):
        kt = keys_ref[j]
        eq = kt == thr
        eq_f = jnp.where(eq, 1.0, 0.0)
        prefix = jnp.dot(tri, eq_f.astype(BF16), preferred_element_type=F32) + carry
        sel = (kt > thr) | (eq & (prefix <= need))
        bias_ref[j] = jnp.where(sel, 0.0, NEG_BIG)
        return prefix[bk - 1:bk, :]

    lax.fori_loop(0, n_tiles, bias_tile, jnp.zeros((1, bq), F32))

    m_ref[...] = jnp.full(m_ref.shape, NEG_BIG, F32)
    l_ref[...] = jnp.zeros(l_ref.shape, F32)
    acc_ref[...] = jnp.zeros(acc_ref.shape, F32)

    def attn_tile(j, carry):
        kt = k_ref[pl.ds(pl.multiple_of(j * bk, bk), bk), :]
        bias = bias_ref[j]
        alphas = []
        for h in range(N_HEADS):
            pair = slice((h // 2) * LANES, (h // 2 + 1) * LANES)
            s = jnp.dot(kt[:, pair], qz_ref[h], preferred_element_type=F32) + bias
            s_ref[h] = s
            m_prev = m_ref[h]
            m_new = jnp.maximum(m_prev, jnp.broadcast_to(jnp.max(s, axis=0, keepdims=True), (SUBLANES, bq)))
            m_ref[h] = m_new
            alphas.append(jnp.exp2(m_prev - m_new))
        for h in range(N_HEADS):
            p = jnp.exp2(s_ref[h] - m_ref[h, 0:1, :])
            l_ref[h] = alphas[h] * l_ref[h] + _tree_sum([p[g * SUBLANES:(g + 1) * SUBLANES] for g in range(ngrp)])
            pv = jnp.dot(vt_ref[j, h * HEAD_DIM:(h + 1) * HEAD_DIM, :], p.astype(BF16),
                         preferred_element_type=F32)
            acc_ref[h] = alphas[h][0:1] * acc_ref[h] + pv
        return carry

    lax.fori_loop(0, n_tiles, attn_tile, 0)

    outs = [acc_ref[h] / jnp.sum(l_ref[h], axis=0, keepdims=True) for h in range(N_HEADS)]
    o_ref[...] = jnp.concatenate(outs, axis=0).T.astype(o_ref.dtype)


def _dsa_call(qt, qit, kw, k, vt, tri, batch, seq):
    nt, aw, tw = qt.shape
    bq = BQ
    assert tw == BK and bq % tw == 0 and seq % bq == 0
    nq = seq // bq
    n_sel = min(TOPK_ATTN, seq // 4)
    qtile = lambda b, i: (b * nq + i, 0)
    qtile_t = lambda b, i: (b * nq + i, 0, 0)
    whole = lambda b, i: (b, 0)
    kern = functools.partial(_dsa_kernel, n_sel=n_sel)
    return pl.pallas_call(
        kern,
        out_shape=jax.ShapeDtypeStruct((batch * seq, aw), BF16),
        grid=(batch, nq),
        in_specs=[
            pl.BlockSpec((bq // tw, aw, tw), qtile_t),
            pl.BlockSpec((bq // tw, aw, tw), qtile_t),
            pl.BlockSpec((bq, LANES), qtile),
            pl.BlockSpec((seq, aw), whole),
            pl.BlockSpec((seq // BK, aw, BK), lambda b, i: (b, 0, 0)),
            pl.BlockSpec((seq, LANES), whole),
            pl.BlockSpec(tri.shape, lambda b, i: (0, 0)),
        ],
        out_specs=pl.BlockSpec((bq, aw), qtile),
        scratch_shapes=[
            pltpu.VMEM((seq // BK, BK, bq), I32),
            pltpu.VMEM((seq // BK, BK, bq), F32),
            pltpu.VMEM((N_HEADS, LANES, bq), BF16),
            pltpu.VMEM((IDX_HEADS, LANES, bq), BF16),
            pltpu.VMEM((N_HEADS, BK, bq), F32),
            pltpu.VMEM((N_HEADS, SUBLANES, bq), F32),
            pltpu.VMEM((N_HEADS, SUBLANES, bq), F32),
            pltpu.VMEM((N_HEADS, HEAD_DIM, bq), F32),
        ],
        compiler_params=pltpu.CompilerParams(
            dimension_semantics=("arbitrary", "arbitrary"), vmem_limit_bytes=VMEM_LIMIT),
        name="dsa_attention",
    )(qt, qit, kw, k, vt, kw, tri)


def _post_kernel(x_ref, attn_ref, gyc_ref, sga_ref, mod_ref, wao_ref, wo_ref, o_ref):
    g1 = mod_ref[0, 2:3, :]
    y_attn = jnp.dot(attn_ref[...], wao_ref[...], preferred_element_type=F32)
    mix = gyc_ref[...] + sga_ref[...] * y_attn
    delta = jnp.dot(mix.astype(BF16), wo_ref[...], preferred_element_type=F32)
    o_ref[...] = x_ref[...] + g1 * delta


def _post_call(xf, attn, gyc, sga, mod3, wao, wo, seq):
    t, d = xf.shape
    bt = BT_PROJ
    tiles_per_seq = seq // bt
    aw = attn.shape[1]
    tile = lambda i: (i, 0)
    const = lambda i: (0, 0)
    return pl.pallas_call(
        _post_kernel,
        out_shape=jax.ShapeDtypeStruct((t, d), F32),
        grid=(t // bt,),
        in_specs=[
            pl.BlockSpec((bt, d), tile),
            pl.BlockSpec((bt, aw), tile),
            pl.BlockSpec((bt, d), tile),
            pl.BlockSpec((bt, d), tile),
            pl.BlockSpec((1, N_MOD, d), lambda i: (i // tiles_per_seq, 0, 0)),
            pl.BlockSpec(wao.shape, const),
            pl.BlockSpec(wo.shape, const),
        ],
        out_specs=pl.BlockSpec((bt, d), tile),
        compiler_params=pltpu.CompilerParams(
            dimension_semantics=("arbitrary",), vmem_limit_bytes=VMEM_LIMIT),
        name="post_mix",
    )(xf, attn, gyc, sga, mod3, wao, wo)


def _col_max(a):
    return jnp.max(a, axis=0, keepdims=True)


def _batcher_pairs(n):
    pairs = []
    p = 1
    while p < n:
        k = p
        while k >= 1:
            for j in range(k % p, n - k, 2 * k):
                for i in range(min(k, n - j - k)):
                    if (i + j) // (2 * p) == (i + j + k) // (2 * p):
                        pairs.append((i + j, i + j + k))
            k //= 2
        p *= 2
    return pairs


_SORT16 = _batcher_pairs(PEER_TOPK)
_BITONIC16 = [(k, k + d) for d in (8, 4, 2, 1) for k in range(PEER_TOPK) if not k & d]


def _pair_tiles():
    tiles = []
    for r in range(2, PEER_TOPK):
        n = PEER_TOPK // (r + 1)
        for tl in tiles:
            used = sum(k for _, k, _ in tl)
            if used + n <= SUBLANES:
                tl.append((r, n, used))
                break
        else:
            tiles.append([(r, n, 0)])
    return tiles


_PAIR_TILES = _pair_tiles()


def _sort_desc(v, pairs):
    v = list(v)
    for i, j in pairs:
        v[i], v[j] = jnp.maximum(v[i], v[j]), jnp.minimum(v[i], v[j])
    return v


def _peer_kernel(x_ref, mod_ref, n2_ref, wqt_ref, sk_ref, u_ref, vt_ref, o_ref,
                 h2t_ref, s1_ref, s2_ref, ap_ref, bp_ref, nsel_ref, rk_ref, w_ref, acc_ref):
    e = pl.program_id(1)
    n_e = pl.num_programs(1)
    tt = x_ref.shape[0]
    nk = PEER_N_KEYS
    n_i = u_ref.shape[0] // nk
    nslab = tt // LANES

    @pl.when(e == 0)
    def _route():
        xv = x_ref[...]
        sh2 = mod_ref[0, 3:4, :]
        sc2 = mod_ref[0, 4:5, :]
        ms = jnp.mean(xv * xv, axis=-1, keepdims=True)
        h2t = ((xv * lax.rsqrt(ms + EPS) * n2_ref[...]) * (1.0 + sc2) + sh2).T.astype(BF16)
        h2t_ref[...] = h2t
        qt = jnp.dot(wqt_ref[...], h2t, preferred_element_type=F32).astype(BF16)
        for h in range(PEER_HEADS):
            for p, dst in ((0, s1_ref), (1, s2_ref)):
                r0 = (h * 2 + p) * PEER_HALF
                dst[h] = jnp.dot(sk_ref[h, p], qt[r0:r0 + PEER_HALF, :],
                                 preferred_element_type=F32)

        neg_inf = jnp.float32(-jnp.inf)
        sub_id = lax.broadcasted_iota(I32, (SUBLANES, LANES), 0)

        def top16(s):
            v = _sort_desc([s[g * SUBLANES:(g + 1) * SUBLANES] for g in range(nk // SUBLANES)], _SORT16)
            for shift in (4, 2, 1):
                v = [jnp.maximum(v[r], pltpu.roll(v[PEER_TOPK - 1 - r], shift, 0)) for r in range(PEER_TOPK)]
                v = _sort_desc(v, _BITONIC16)
            return v

        def stack8(vals):
            out = vals[0]
            for r in range(1, SUBLANES):
                out = jnp.where(sub_id == r, vals[r], out)
            return out

        def route_head(h, carry):
            for c in range(nslab):
                ls = slice(c * LANES, (c + 1) * LANES)
                s1 = s1_ref[h, :, ls]
                s2 = s2_ref[h, :, ls]
                a = top16(s1)
                b = top16(s2)
                b_lo, b_hi = stack8(b[:SUBLANES]), stack8(b[SUBLANES:])
                cands = [a[0] + b_lo, a[0] + b_hi, a[1] + b_lo]
                for segs in _PAIR_TILES:
                    a_pk, b_pk, used = a[segs[0][0]], None, 0
                    for r, n, start in segs:
                        if start:
                            a_pk = jnp.where(sub_id >= start, a[r], a_pk)
                        for cc in range(n):
                            b_pk = b[cc] if b_pk is None else jnp.where(sub_id == start + cc, b[cc], b_pk)
                        used = start + n
                    pk_sum = a_pk + b_pk
                    cands.append(pk_sum if used == SUBLANES else jnp.where(sub_id < used, pk_sum, neg_inf))
                work, t = cands, []
                for _ in range(PEER_TOPK):
                    m = _col_max(functools.reduce(jnp.maximum, work))
                    t.append(m)
                    work = [jnp.where(w == m, neg_inf, w) for w in work]
                thr = t[PEER_TOPK - 1]
                z = functools.reduce(jnp.add, [jnp.exp(tk - t[0]) for tk in t])
                hits = [jnp.where(cd >= thr, 1.0, 0.0) for cd in cands]
                n_sel = [None] * PEER_TOPK
                n_sel[0] = jnp.sum(hits[0] + hits[1], axis=0, keepdims=True)
                n_sel[1] = jnp.sum(hits[2], axis=0, keepdims=True)
                for ti, segs in enumerate(_PAIR_TILES):
                    for r, n, start in segs:
                        in_seg = (sub_id >= start) & (sub_id < start + n)
                        n_sel[r] = jnp.sum(jnp.where(in_seg, hits[3 + ti], 0.0), axis=0, keepdims=True)
                n_b = [jnp.broadcast_to(n, (SUBLANES, LANES)) for n in n_sel]
                cnts, ranks = [], []
                for g in range(nk // SUBLANES):
                    rows = slice(g * 8, (g + 1) * 8)
                    s1g, s2g = s1[rows], s2[rows]
                    cnt = jnp.zeros((SUBLANES, LANES), F32)
                    rank = jnp.full((SUBLANES, LANES), PEER_TOPK + 1.0, F32)
                    for r in range(PEER_TOPK):
                        cnt = jnp.where(s1g == a[r], n_b[r], cnt)
                        rank = jnp.where(s2g == b[r], r + 1.0, rank)
                    cnts.append(cnt)
                    ranks.append(rank)
                nsel_ref[h, c] = jnp.concatenate(cnts, axis=0)
                rk_ref[h, :, ls] = jnp.concatenate(ranks, axis=0).astype(BF16)
                ap_ref[h, c] = jnp.exp(s1 - a[0][0:1])
                bp_ref[h, :, ls] = (jnp.exp(s2 - b[0][0:1]) * (0.5 / z)).astype(BF16)
            return carry

        lax.fori_loop(0, PEER_HEADS, route_head, 0)
        acc_ref[...] = jnp.zeros(acc_ref.shape, F32)

    irow0 = pl.multiple_of(e * n_i, SUBLANES)
    pk = BF16_ROWS
    n_blk = [[nsel_ref[h, c, pl.ds(irow0, n_i), :] for h in range(PEER_HEADS)] for c in range(nslab)]
    ap_blk = [[ap_ref[h, c, pl.ds(irow0, n_i), :] for h in range(PEER_HEADS)] for c in range(nslab)]
    h2t = h2t_ref[...]
    was = []
    for i0 in range(0, n_i, PEER_IBLK):
        for c in range(nslab):
            ls = slice(c * LANES, (c + 1) * LANES)
            for ii in range(i0, i0 + PEER_IBLK):
                accs = [None] * (nk // pk)
                for h in range(PEER_HEADS):
                    n16 = jnp.broadcast_to(n_blk[c][h][ii:ii + 1], (pk, LANES)).astype(BF16)
                    a16 = jnp.broadcast_to(ap_blk[c][h][ii:ii + 1], (pk, LANES)).astype(BF16)
                    for g in range(nk // pk):
                        rows = slice(g * pk, (g + 1) * pk)
                        hit = rk_ref[h, rows, ls] <= n16
                        term = jnp.where(hit, bp_ref[h, rows, ls], jnp.zeros((), BF16)) * a16
                        accs[g] = term if h == 0 else accs[g] + term
                for g in range(nk // pk):
                    w_ref[ii * nk + g * pk:ii * nk + (g + 1) * pk, ls] = accs[g]
        rows = slice(i0 * nk, (i0 + PEER_IBLK) * nk)
        act = jnp.dot(u_ref[rows, :], h2t, preferred_element_type=F32)
        gelu2 = act * (1.0 + lax.erf(act * (2.0 ** -0.5)))
        was.append(w_ref[rows, :] * gelu2.astype(BF16))
    acc_ref[...] += jnp.dot(vt_ref[0], jnp.concatenate(was, axis=0), preferred_element_type=F32)

    @pl.when(e == n_e - 1)
    def _finish():
        g2 = mod_ref[0, 5:6, :]
        o_ref[...] = x_ref[...] + g2 * acc_ref[...].T


def _peer_call(x1, mod3, n2, wqt, sk, u, vt, seq):
    t, d = x1.shape
    tt = TT_PEER
    ec = EC_PEER
    n_chunks = u.shape[0] // ec
    assert ec % (SUBLANES * PEER_N_KEYS) == 0
    tiles_per_seq = seq // tt
    tile = lambda i, e: (i, 0)
    const2 = lambda i, e: (0, 0)
    return pl.pallas_call(
        _peer_kernel,
        out_shape=jax.ShapeDtypeStruct((t, d), F32),
        grid=(t // tt, n_chunks),
        in_specs=[
            pl.BlockSpec((tt, d), tile),
            pl.BlockSpec((1, N_MOD, d), lambda i, e: (i // tiles_per_seq, 0, 0)),
            pl.BlockSpec((1, d), const2),
            pl.BlockSpec(wqt.shape, const2),
            pl.BlockSpec(sk.shape, lambda i, e: (0, 0, 0, 0)),
            pl.BlockSpec((ec, d), lambda i, e: (e, 0)),
            pl.BlockSpec((1, d, ec), lambda i, e: (e, 0, 0)),
        ],
        out_specs=pl.BlockSpec((tt, d), tile),
        scratch_shapes=[
            pltpu.VMEM((d, tt), BF16),
            pltpu.VMEM((PEER_HEADS, PEER_N_KEYS, tt), F32),
            pltpu.VMEM((PEER_HEADS, PEER_N_KEYS, tt), F32),
            pltpu.VMEM((PEER_HEADS, tt // LANES, PEER_N_KEYS, LANES), F32),
            pltpu.VMEM((PEER_HEADS, PEER_N_KEYS, tt), BF16),
            pltpu.VMEM((PEER_HEADS, tt // LANES, PEER_N_KEYS, LANES), F32),
            pltpu.VMEM((PEER_HEADS, PEER_N_KEYS, tt), BF16),
            pltpu.VMEM((ec, tt), BF16),
            pltpu.VMEM((d, tt), F32),
        ],
        compiler_params=pltpu.CompilerParams(
            dimension_semantics=("arbitrary", "arbitrary"), vmem_limit_bytes=VMEM_LIMIT),
        name="peer",
    )(x1, mod3, n2, wqt, sk, u, vt)


def kernel(x, c, w_ada, b_ada, norm1_w, w_in, conv_w, w_conv_out, q_norm_w, k_norm_w,
           w_attn_out, w_o, norm2_w, w_peer_q, peer_sub_keys, peer_u, peer_v):
    b, s, d = x.shape
    depth = w_ada.shape[0]
    cw = conv_w.shape[2]
    aw = N_HEADS * HEAD_DIM
    iw = IDX_HEADS * IDX_DIM
    assert s % BQ == 0 and s % TT_PEER == 0 and s % BT_PROJ == 0
    assert peer_u.shape[1] == PEER_N_KEYS * PEER_N_KEYS

    gidx = jnp.arange(aw) // HEAD_DIM
    gmat = (gidx[:, None] == gidx[None, :]).astype(BF16)
    tri = (jnp.arange(BK)[:, None] >= jnp.arange(BK)[None, :]).astype(BF16)

    xf = x.reshape(b * s, d)
    for layer in range(depth):
        w = w_in[layer]
        o = 0
        wconv = w[:, o:o + 3 * cw].astype(BF16); o += 3 * cw
        wqkv = w[:, o:o + 3 * aw].astype(BF16); o += 3 * aw
        wqi = w[:, o:o + iw].astype(BF16); o += iw
        wkw = jnp.pad(w[:, o:o + IDX_DIM + IDX_HEADS], ((0, 0), (0, LANES - IDX_DIM - IDX_HEADS))).astype(BF16)
        o += IDX_DIM + IDX_HEADS
        wg = w[:, o:o + 2 * d].astype(BF16)
        qn_t = jnp.tile(q_norm_w[layer], N_HEADS).reshape(1, aw)
        kn_t = jnp.tile(k_norm_w[layer], N_HEADS).reshape(1, aw)

        mod3 = _mod_call(c, w_ada[layer], b_ada[layer]).reshape(b, N_MOD, d)
        gyc, sga, qt, k, vt, qit, kw = _inproj_call(
            xf, mod3, norm1_w[layer].reshape(1, d), wconv, wqkv, wqi, wkw, wg, conv_w[layer],
            w_conv_out[layer].astype(BF16), qn_t, kn_t, gmat, s)
        attn = _dsa_call(qt, qit, kw, k, vt, tri, b, s)
        x1 = _post_call(xf, attn, gyc, sga, mod3, w_attn_out[layer].astype(BF16),
                        w_o[layer].astype(BF16), s)
        xf = _peer_call(
            x1, mod3, norm2_w[layer].reshape(1, d), w_peer_q[layer].T.astype(BF16),
            peer_sub_keys[layer].astype(BF16), peer_u[layer].astype(BF16),
            peer_v[layer].astype(BF16).reshape(-1, EC_PEER, d).transpose(0, 2, 1), s)
    return xf.reshape(b, s, d)
```
